```python
import math
import jax, jax.numpy as jnp
from jax import lax
import numpy as np

D_MODEL = 1024
BATCH = 8
SEQ = 4096
DEPTH = 4

POOL_WINDOWS = (2, 4, 8, 16)
POOL_GROUP = D_MODEL // 16
D_A = POOL_GROUP * len(POOL_WINDOWS)
N_HEADS = 8
HEAD_DIM = 64
D_B = N_HEADS * HEAD_DIM
Q_BLOCK = 128
SSM_GROUP = 16
D_C = D_MODEL // 4
SSM_GROUPS = D_C // SSM_GROUP
SSM_STATE = 64
DT_MIN = 0.001
DT_MAX = 0.1
DEEPNORM_ALPHA = (2.0 * DEPTH) ** 0.25
DEEPNORM_BETA = (8.0 * DEPTH) ** -0.25
LN_EPS = 1e-5
N_BRANCHES = 3

IN_SIZES = (D_A, D_A,
            D_B, D_B, D_B,
            N_HEADS,
            D_B,
            D_C, D_C,
            N_BRANCHES * D_MODEL)
IN_SPLITS = tuple(int(v) for v in np.cumsum(IN_SIZES)[:-1])
D_IN = int(sum(IN_SIZES))
F_OFFSET = D_A * 2 + D_B * 3

kernel_name = "hybrid_pool_fox_s5_gated_deepnorm"


def layer_norm(x, g, b):
    xf = x.astype(jnp.float32)
    mu = jnp.mean(xf, axis=-1, keepdims=True)
    var = jnp.mean(jnp.square(xf - mu), axis=-1, keepdims=True)
    y = (xf - mu) * lax.rsqrt(var + LN_EPS) * g.astype(jnp.float32) + b.astype(jnp.float32)
    return y.astype(x.dtype)


def pool_mixer(u, pool_w, pool_scale):
    b, s, _ = u.shape
    uf = u.astype(jnp.float32).reshape(b, s, len(POOL_WINDOWS), POOL_GROUP)
    csum = jnp.cumsum(uf, axis=1)
    pos = jnp.arange(s)
    outs = []
    for gi, w in enumerate(POOL_WINDOWS):
        cg = csum[:, :, gi]
        shifted = jnp.concatenate([jnp.zeros((b, w, POOL_GROUP), jnp.float32), cg], axis=1)[:, :s]
        count = jnp.minimum(pos + 1, w).astype(jnp.float32)[None, :, None]
        pooled = (cg - shifted) / count - uf[:, :, gi]
        outs.append(jnp.einsum('bsc,ce->bse', pooled, pool_w[gi].astype(jnp.float32)))
    y = jnp.concatenate(outs, axis=-1) * pool_scale.astype(jnp.float32)
    return y


def fox_attention(q, k, v, f_logit):
    b, s, h, dh = q.shape
    nb = s // Q_BLOCK
    logf = jax.nn.log_sigmoid(f_logit.astype(jnp.float32))
    fcum = jnp.cumsum(logf, axis=1)
    fk = jnp.transpose(fcum, (0, 2, 1))
    qb = jnp.moveaxis(q.reshape(b, nb, Q_BLOCK, h, dh), 1, 0)
    fqb = jnp.moveaxis(fcum.reshape(b, nb, Q_BLOCK, h), 1, 0)
    key_pos = jnp.arange(s)
    scale = 1.0 / math.sqrt(dh)

    def one_block(args):
        qi, fqi, bi = args
        scores = jnp.einsum('bqhd,bkhd->bhqk', qi, k,
                            preferred_element_type=jnp.float32) * scale
        bias = jnp.transpose(fqi, (0, 2, 1))[..., None] - fk[:, :, None, :]
        q_pos = bi * Q_BLOCK + jnp.arange(Q_BLOCK)
        mask = key_pos[None, :] <= q_pos[:, None]
        logits = jnp.where(mask[None, None], scores + bias, -jnp.inf)
        p = jax.nn.softmax(logits, axis=-1)
        return jnp.einsum('bhqk,bkhd->bqhd', p, v.astype(jnp.float32))

    out = lax.map(one_block, (qb, fqb, jnp.arange(nb)))
    return jnp.moveaxis(out, 0, 1).reshape(b, s, h * dh)


def _ssm_combine(e1, e2):
    a1r, a1i, b1r, b1i = e1
    a2r, a2i, b2r, b2i = e2
    ar = a1r * a2r - a1i * a2i
    ai = a1r * a2i + a1i * a2r
    br = a2r * b1r - a2i * b1i + b2r
    bi = a2r * b1i + a2i * b1r + b2i
    return (ar, ai, br, bi)


def s5_mixer(u, a_re, a_im, log_dt, b_re, b_im, c_re, c_im, d_skip, w_glu, b_glu):
    b, s, _ = u.shape
    uf = u.astype(jnp.float32)
    ug = uf.reshape(b, s, SSM_GROUPS, SSM_GROUP)
    are = a_re.astype(jnp.float32)
    aim = a_im.astype(jnp.float32)
    dt = jnp.exp(log_dt.astype(jnp.float32))[:, None]
    mag = jnp.exp(are * dt)
    ang = aim * dt
    abar_re = mag * jnp.cos(ang)
    abar_im = mag * jnp.sin(ang)
    den = are * are + aim * aim
    nr = abar_re - 1.0
    coef_re = (nr * are + abar_im * aim) / den
    coef_im = (abar_im * are - nr * aim) / den
    br_ = b_re.astype(jnp.float32)
    bi_ = b_im.astype(jnp.float32)
    bb_re = coef_re[..., None] * br_ - coef_im[..., None] * bi_
    bb_im = coef_re[..., None] * bi_ + coef_im[..., None] * br_
    bu_re = jnp.einsum('bsgh,gph->bsgp', ug, bb_re)
    bu_im = jnp.einsum('bsgh,gph->bsgp', ug, bb_im)
    a_full_re = jnp.broadcast_to(abar_re, bu_re.shape)
    a_full_im = jnp.broadcast_to(abar_im, bu_im.shape)
    _, _, xr, xi = lax.associative_scan(_ssm_combine, (a_full_re, a_full_im, bu_re, bu_im), axis=1)
    y = (jnp.einsum('ghp,bsgp->bsgh', c_re.astype(jnp.float32), xr)
         - jnp.einsum('ghp,bsgp->bsgh', c_im.astype(jnp.float32), xi)).reshape(b, s, D_C)
    y = y + d_skip.astype(jnp.float32) * uf
    y = jax.nn.gelu(y)
    return y * jax.nn.sigmoid(jnp.einsum('bsc,ce->bse', y, w_glu.astype(jnp.float32))
                              + b_glu.astype(jnp.float32))


def hybrid_layer(x, w_in, b_in, pool_w, pool_scale, a_re, a_im, log_dt, b_re, b_im, c_re, c_im,
                 d_skip, w_glu, b_glu, w_up_a, w_up_b, w_up_c, w_out, ln_g, ln_b):
    b, s, _ = x.shape
    z = jnp.einsum('bsd,de->bse', x, w_in) + b_in
    (za, ga, zq, zk, zv, zf, gb, zc, gc, zg) = jnp.split(z, IN_SPLITS, axis=-1)
    gaf = ga.astype(jnp.float32)
    gbf = gb.astype(jnp.float32)
    gcf = gc.astype(jnp.float32)
    y_a = pool_mixer(za, pool_w, pool_scale) * jax.nn.silu(gaf)
    y_b = fox_attention(zq.reshape(b, s, N_HEADS, HEAD_DIM), zk.reshape(b, s, N_HEADS, HEAD_DIM),
                        zv.reshape(b, s, N_HEADS, HEAD_DIM), zf) * jax.nn.silu(gbf)
    y_c = s5_mixer(zc, a_re, a_im, log_dt, b_re, b_im, c_re, c_im, d_skip, w_glu, b_glu) \
        * jax.nn.silu(gcf)
    gates = jax.nn.sigmoid(zg.astype(jnp.float32)).reshape(b, s, N_BRANCHES, D_MODEL)
    merged = (gates[:, :, 0] * jnp.einsum('bsc,cd->bsd', y_a, w_up_a.astype(jnp.float32))
              + gates[:, :, 1] * jnp.einsum('bsc,cd->bsd', y_b, w_up_b.astype(jnp.float32))
              + gates[:, :, 2] * jnp.einsum('bsc,cd->bsd', y_c, w_up_c.astype(jnp.float32)))
    out = jnp.einsum('bsd,de->bse', merged, w_out.astype(jnp.float32))
    return layer_norm((DEEPNORM_ALPHA * x.astype(jnp.float32) + out).astype(x.dtype), ln_g, ln_b)


def setup_inputs(seed: int = 0) -> dict:
    key = jax.random.key(seed)
    ks = jax.random.split(key, 24)
    L = DEPTH
    f32 = jnp.float32
    x = jax.random.normal(ks[0], (BATCH, SEQ, D_MODEL), f32)
    w_in = jax.random.normal(ks[1], (L, D_MODEL, D_IN), f32) * D_MODEL ** -0.5
    b_in = 0.02 * jax.random.normal(ks[2], (L, D_IN), f32)
    f_bias = jnp.linspace(1.0, 5.0, N_HEADS)[None, :] + 0.1 * jax.random.normal(ks[3], (L, N_HEADS), f32)
    b_in = b_in.at[:, F_OFFSET:F_OFFSET + N_HEADS].set(f_bias)
    pool_w = jax.random.normal(ks[4], (L, len(POOL_WINDOWS), POOL_GROUP, POOL_GROUP), f32) * POOL_GROUP ** -0.5
    pool_scale = 1.0 + 0.1 * jax.random.normal(ks[5], (L, D_A), f32)
    n_idx = jnp.arange(SSM_STATE, dtype=f32)[None, None, :]
    ssm_a_re = -0.5 + 0.01 * jax.random.normal(ks[6], (L, SSM_GROUPS, SSM_STATE), f32)
    ssm_a_im = math.pi * n_idx + 0.01 * jax.random.normal(ks[7], (L, SSM_GROUPS, SSM_STATE), f32)
    ssm_log_dt = jax.random.uniform(ks[8], (L, SSM_GROUPS), f32, math.log(DT_MIN), math.log(DT_MAX))
    bs = (2.0 * SSM_GROUP) ** -0.5
    ssm_b_re = jax.random.normal(ks[9], (L, SSM_GROUPS, SSM_STATE, SSM_GROUP), f32) * bs
    ssm_b_im = jax.random.normal(ks[10], (L, SSM_GROUPS, SSM_STATE, SSM_GROUP), f32) * bs
    cs = (2.0 * SSM_STATE) ** -0.5
    ssm_c_re = jax.random.normal(ks[11], (L, SSM_GROUPS, SSM_GROUP, SSM_STATE), f32) * cs
    ssm_c_im = jax.random.normal(ks[12], (L, SSM_GROUPS, SSM_GROUP, SSM_STATE), f32) * cs
    ssm_d = jax.random.normal(ks[13], (L, D_C), f32)
    w_glu = jax.random.normal(ks[14], (L, D_C, D_C), f32) * D_C ** -0.5
    b_glu = 0.02 * jax.random.normal(ks[15], (L, D_C), f32)
    w_up_a = jax.random.normal(ks[16], (L, D_A, D_MODEL), f32) * (D_A ** -0.5) * DEEPNORM_BETA
    w_up_b = jax.random.normal(ks[17], (L, D_B, D_MODEL), f32) * (D_B ** -0.5) * DEEPNORM_BETA
    w_up_c = jax.random.normal(ks[18], (L, D_C, D_MODEL), f32) * (D_C ** -0.5) * DEEPNORM_BETA
    w_out = jax.random.normal(ks[19], (L, D_MODEL, D_MODEL), f32) * (D_MODEL ** -0.5) * DEEPNORM_BETA
    ln_g = 1.0 + 0.05 * jax.random.normal(ks[20], (L, D_MODEL), f32)
    ln_b = 0.02 * jax.random.normal(ks[21], (L, D_MODEL), f32)
    return {"x": x, "w_in": w_in, "b_in": b_in, "pool_w": pool_w, "pool_scale": pool_scale,
            "ssm_a_re": ssm_a_re, "ssm_a_im": ssm_a_im, "ssm_log_dt": ssm_log_dt,
            "ssm_b_re": ssm_b_re, "ssm_b_im": ssm_b_im, "ssm_c_re": ssm_c_re, "ssm_c_im": ssm_c_im,
            "ssm_d": ssm_d, "w_glu": w_glu, "b_glu": b_glu, "w_up_a": w_up_a, "w_up_b": w_up_b,
            "w_up_c": w_up_c, "w_out": w_out, "ln_g": ln_g, "ln_b": ln_b}


def reference(x, w_in, b_in, pool_w, pool_scale, ssm_a_re, ssm_a_im, ssm_log_dt, ssm_b_re, ssm_b_im,
              ssm_c_re, ssm_c_im, ssm_d, w_glu, b_glu, w_up_a, w_up_b, w_up_c, w_out, ln_g, ln_b):
    h = x
    for l in range(DEPTH):
        h = hybrid_layer(h, w_in[l], b_in[l], pool_w[l], pool_scale[l], ssm_a_re[l], ssm_a_im[l],
                         ssm_log_dt[l], ssm_b_re[l], ssm_b_im[l], ssm_c_re[l], ssm_c_im[l], ssm_d[l],
                         w_glu[l], b_glu[l], w_up_a[l], w_up_b[l], w_up_c[l], w_out[l], ln_g[l], ln_b[l])
    return h
```

```python
import functools
import math

import jax
import jax.numpy as jnp
import numpy as np
from jax import lax
from jax.experimental import pallas as pl
from jax.experimental.pallas import tpu as pltpu

F32 = jnp.float32
BF16 = jnp.bfloat16

D_MODEL = 1024
N_HEADS = 8
HEAD_DIM = 64
D_A = 256
D_B = N_HEADS * HEAD_DIM
D_C = 256
POOL_WINDOWS = (2, 4, 8, 16)
POOL_GROUP = 64
SSM_GROUPS = 16
SSM_GROUP = 16
SSM_STATE = 64
N_STATE = SSM_GROUPS * SSM_STATE
LN_EPS = 1e-5
LANES = 128
SUBLANES = 8
NEG_BIG = -1e30

ZA_OFF, ZC_OFF, ZF_OFF, Q_OFF, K_OFF, V_OFF = 0, 256, 512, 640, 1152, 1664
C1 = 2176
C3 = 4096
AUG = HEAD_DIM
HALO = 32

VMEM_LIMIT = 56 * 1024 * 1024


def _sigmoid(x):
    return 0.5 * (jnp.tanh(0.5 * x) + 1.0)


def _silu(x):
    return x * _sigmoid(x)


def _gelu_tanh(x):
    return 0.5 * x * (1.0 + jnp.tanh(math.sqrt(2.0 / math.pi) * (x + 0.044715 * (x * x * x))))


def _log_sigmoid(x):
    return jnp.minimum(x, 0.0) - jnp.log(1.0 + jnp.exp(-jnp.abs(x)))


def _ssm_prep_body(are_ref, aim_ref, ldt_ref, bre_ref, bim_ref, abr_ref, abi_ref, bbr_ref, bbi_ref):
    are = are_ref[...]
    aim = aim_ref[...]
    dt = jnp.exp(ldt_ref[...])
    mag = jnp.exp(are * dt)
    ang = aim * dt
    abr = mag * jnp.cos(ang)
    abi = mag * jnp.sin(ang)
    den = are * are + aim * aim
    nr = abr - 1.0
    cre = (nr * are + abi * aim) / den
    cim = (abi * are - nr * aim) / den
    bre = bre_ref[...]
    bim = bim_ref[...]
    abr_ref[...] = abr
    abi_ref[...] = abi
    bbr_ref[...] = cre * bre - cim * bim
    bbi_ref[...] = cre * bim + cim * bre


def _ssm_prep(a_re, a_im, log_dt, b_re, b_im):
    L = a_re.shape[0]
    rows = L * SSM_GROUPS
    wide = SSM_GROUP * SSM_STATE
    tile = lambda a: jnp.tile(a.reshape(rows, SSM_STATE), (1, SSM_GROUP))
    ldt = jnp.broadcast_to(log_dt.reshape(rows, 1), (rows, wide))
    bt = lambda b: jnp.swapaxes(b, 2, 3).reshape(rows, wide)
    outs = pl.pallas_call(
        _ssm_prep_body,
        name="ssm_prep",
        out_shape=[jax.ShapeDtypeStruct((rows, wide), F32)] * 4,
    )(tile(a_re), tile(a_im), ldt, bt(b_re), bt(b_im))
    abr, abi, bbr, bbi = outs
    abr = abr[:, :SSM_STATE].reshape(L, SSM_GROUPS, SSM_STATE)
    abi = abi[:, :SSM_STATE].reshape(L, SSM_GROUPS, SSM_STATE)
    bbr = bbr.reshape(L, SSM_GROUPS, SSM_GROUP, SSM_STATE)
    bbi = bbi.reshape(L, SSM_GROUPS, SSM_GROUP, SSM_STATE)
    return abr, abi, bbr, bbi


def _mix_in_body(x_ref, w1_ref, b1_ref, pw_ref, ps_ref, are_ref, aim_ref, bbr_ref, bbi_ref,
                 cre_ref, cim_ref, d_ref, wglu_ref, bglu_ref, eq_ref, ek_ref,
                 ya_ref, yc_ref, q_ref, k_ref, v_ref,
                 z_scr, ext_scr, lv2_scr, lv4_scr, lv8_scr, slin_scr, slout_scr, utm_scr,
                 xr_scr, xi_scr, stre_scr, stim_scr, stf_scr, *, tt, nb):
    rows = nb * tt
    pitch = tt + 4
    step = pl.program_id(0)

    @pl.when(step == 0)
    def _():
        ext_scr[:, 0:HALO, :] = jnp.zeros((nb, HALO, D_A), F32)
        stre_scr[...] = jnp.zeros_like(stre_scr)
        stim_scr[...] = jnp.zeros_like(stim_scr)
        stf_scr[...] = jnp.zeros_like(stf_scr)

    xb = x_ref[...].reshape(rows, D_MODEL).astype(BF16)
    for c0 in range(0, C1, 256):
        c1 = min(c0 + 256, C1)
        z_scr[:, c0:c1] = (jnp.dot(xb, w1_ref[:, c0:c1], preferred_element_type=F32)
                           + b1_ref[:, c0:c1])

    lane_a = lax.broadcasted_iota(jnp.int32, (tt, D_A), 1)
    row_a = lax.broadcasted_iota(jnp.int32, (tt, D_A), 0)
    win = jnp.where(lane_a < 64, 2, jnp.where(lane_a < 128, 4, jnp.where(lane_a < 192, 8, 16)))
    cnt = jnp.minimum(step * tt + row_a + 1, win).astype(F32)
    ext_len = tt + HALO

    def pool_one(b, carry):
        u = z_scr[pl.ds(pl.multiple_of(b * tt, tt), tt), ZA_OFF:ZA_OFF + D_A]
        ext_scr[b, HALO:ext_len, :] = u
        lv2_scr[8:ext_len, :] = ext_scr[b, 8:ext_len, :] + ext_scr[b, 7:ext_len - 1, :]
        lv4_scr[16:ext_len, :] = lv2_scr[16:ext_len, :] + lv2_scr[14:ext_len - 2, :]
        lv8_scr[24:ext_len, :] = lv4_scr[24:ext_len, :] + lv4_scr[20:ext_len - 4, :]
        s16 = lv8_scr[HALO:ext_len, :] + lv8_scr[HALO - 8:ext_len - 8, :]
        wsum = jnp.where(lane_a < 64, lv2_scr[HALO:ext_len, :],
                         jnp.where(lane_a < 128, lv4_scr[HALO:ext_len, :],
                                   jnp.where(lane_a < 192, lv8_scr[HALO:ext_len, :], s16)))
        pooled = wsum / cnt - u
        ya = jnp.dot(pooled.astype(BF16), pw_ref[...], preferred_element_type=F32) * ps_ref[...]
        ya_ref[b] = ya
        ext_scr[b, 0:HALO, :] = ext_scr[b, tt:ext_len, :]
        return carry

    lax.fori_loop(0, nb, pool_one, 0)

    for b in range(nb):
        r0 = b * tt
        slin_scr[0, b * pitch:b * pitch + tt, :] = z_scr[r0:r0 + tt, ZC_OFF:ZC_OFF + LANES]
        slin_scr[1, b * pitch:b * pitch + tt, :] = z_scr[r0:r0 + tt, ZC_OFF + LANES:ZC_OFF + 2 * LANES]
        slin_scr[2, b * pitch:b * pitch + tt, :] = _log_sigmoid(z_scr[r0:r0 + tt, ZF_OFF:ZF_OFF + LANES])

    def gather_t(t, carry):
        r = pl.ds(pl.multiple_of(t * nb, nb), nb)
        utm_scr[r, 0:LANES] = slin_scr[0, pl.ds(t, nb, stride=pitch), :]
        utm_scr[r, LANES:2 * LANES] = slin_scr[1, pl.ds(t, nb, stride=pitch), :]
        return carry

    lax.fori_loop(0, tt, gather_t, 0, unroll=4)

    u_tm = utm_scr[...]
    ub = u_tm.astype(BF16)
    xr_scr[...] = jnp.dot(ub, bbr_ref[...], preferred_element_type=F32)
    xi_scr[...] = jnp.dot(ub, bbi_ref[...], preferred_element_type=F32)

    ar = are_ref[...]
    ai = aim_ref[...]

    def scan_t(t, carry):
        sr, si, f = carry
        r = pl.ds(pl.multiple_of(t * nb, nb), nb)
        nr = ar * sr - ai * si + xr_scr[r, :]
        ni = ar * si + ai * sr + xi_scr[r, :]
        xr_scr[r, :] = nr
        xi_scr[r, :] = ni
        f = f + slin_scr[2, pl.ds(t, nb, stride=pitch), :]
        slout_scr[2, r, :] = f
        return nr, ni, f

    sr, si, f = lax.fori_loop(0, tt, scan_t, (stre_scr[...], stim_scr[...], stf_scr[...]), unroll=2)
    stre_scr[...] = sr
    stim_scr[...] = si
    stf_scr[...] = f

    y = (jnp.dot(xr_scr[...].astype(BF16), cre_ref[...], preferred_element_type=F32)
         - jnp.dot(xi_scr[...].astype(BF16), cim_ref[...], preferred_element_type=F32))
    y = _gelu_tanh(y + d_ref[...] * u_tm)
    glu = jnp.dot(y.astype(BF16), wglu_ref[...], preferred_element_type=F32) + bglu_ref[...]
    y = y * _sigmoid(glu)
    slout_scr[0] = y[:, 0:LANES]
    slout_scr[1] = y[:, LANES:2 * LANES]

    lane = lax.broadcasted_iota(jnp.int32, (tt, LANES), 1)
    for b in range(nb):
        yc_ref[b, :, 0:LANES] = slout_scr[0, pl.ds(b, tt, stride=nb), :]
        yc_ref[b, :, LANES:2 * LANES] = slout_scr[1, pl.ds(b, tt, stride=nb), :]
        fb = slout_scr[2, pl.ds(b, tt, stride=nb), :]
        hi = fb.astype(BF16).astype(F32)
        r1 = fb - hi
        mid = r1.astype(BF16).astype(F32)
        lo = r1 - mid
        fa = jnp.where(lane < 8, hi,
                       jnp.where(lane < 16, pltpu.roll(mid, 8, axis=1),
                                 jnp.where(lane < 24, pltpu.roll(lo, 16, axis=1),
                                           jnp.where(lane == 24, 1.0, 0.0)))).astype(BF16)
        augq = jnp.dot(fa, eq_ref[...], preferred_element_type=F32)
        augk = jnp.dot(fa, ek_ref[...], preferred_element_type=F32)
        r0 = b * tt
        for h in range(N_HEADS):
            c = (h // 2) * LANES
            zq = z_scr[r0:r0 + tt, Q_OFF + c:Q_OFF + c + LANES]
            zk = z_scr[r0:r0 + tt, K_OFF + c:K_OFF + c + LANES]
            zv = z_scr[r0:r0 + tt, V_OFF + c:V_OFF + c + LANES]
            if h % 2:
                zq = pltpu.roll(zq, HEAD_DIM, axis=1)
                zk = pltpu.roll(zk, HEAD_DIM, axis=1)
                zv = pltpu.roll(zv, HEAD_DIM, axis=1)
            q_ref[b, h] = jnp.where(lane < AUG, zq, augq[:, h * LANES:(h + 1) * LANES]).astype(BF16)
            k_ref[b, h] = jnp.where(lane < AUG, zk, augk[:, h * LANES:(h + 1) * LANES]).astype(BF16)
            v_ref[b, h] = jnp.where(lane < AUG, zv, jnp.where(lane == AUG, 1.0, 0.0)).astype(BF16)


def _const_spec(shape):
    nd = len(shape)
    return pl.BlockSpec(shape, lambda *_: (0,) * nd)


def _mix_in(x, p, tt):
    nb, seq, _ = x.shape
    nt = seq // tt
    rows = nb * tt
    pitch = tt + 4
    consts = [p["w1"], p["b1"], p["pw"], p["ps"], p["are"], p["aim"], p["bbr"], p["bbi"],
              p["cre"], p["cim"], p["d"], p["wglu"], p["bglu"], p["eq"], p["ek"]]
    in_specs = [pl.BlockSpec((nb, tt, D_MODEL), lambda i: (0, i, 0))]
    in_specs += [_const_spec(c.shape) for c in consts]
    out_shape = [
        jax.ShapeDtypeStruct((nb, seq, D_A), F32),
        jax.ShapeDtypeStruct((nb, seq, D_C), F32),
        jax.ShapeDtypeStruct((nb, N_HEADS, seq, LANES), BF16),
        jax.ShapeDtypeStruct((nb, N_HEADS, seq, LANES), BF16),
        jax.ShapeDtypeStruct((nb, N_HEADS, seq, LANES), BF16),
    ]
    out_specs = [
        pl.BlockSpec((nb, tt, D_A), lambda i: (0, i, 0)),
        pl.BlockSpec((nb, tt, D_C), lambda i: (0, i, 0)),
        pl.BlockSpec((nb, N_HEADS, tt, LANES), lambda i: (0, 0, i, 0)),
        pl.BlockSpec((nb, N_HEADS, tt, LANES), lambda i: (0, 0, i, 0)),
        pl.BlockSpec((nb, N_HEADS, tt, LANES), lambda i: (0, 0, i, 0)),
    ]
    scratch = [
        pltpu.VMEM((rows, C1), F32),
        pltpu.VMEM((nb, tt + HALO, D_A), F32),
        pltpu.VMEM((tt + HALO, D_A), F32),
        pltpu.VMEM((tt + HALO, D_A), F32),
        pltpu.VMEM((tt + HALO, D_A), F32),
        pltpu.VMEM((3, nb * pitch, LANES), F32),
        pltpu.VMEM((3, rows, LANES), F32),
        pltpu.VMEM((rows, D_C), F32),
        pltpu.VMEM((rows, N_STATE), F32),
        pltpu.VMEM((rows, N_STATE), F32),
        pltpu.VMEM((nb, N_STATE), F32),
        pltpu.VMEM((nb, N_STATE), F32),
        pltpu.VMEM((nb, LANES), F32),
    ]
    return pl.pallas_call(
        functools.partial(_mix_in_body, tt=tt, nb=nb),
        name="mix_in",
        grid=(nt,),
        in_specs=in_specs,
        out_specs=out_specs,
        out_shape=out_shape,
        scratch_shapes=scratch,
        compiler_params=pltpu.CompilerParams(
            dimension_semantics=("arbitrary",), vmem_limit_bytes=VMEM_LIMIT),
    )(x, *consts)


def _attn_body(q_ref, k_ref, v_ref, o_ref, m_scr, acc_scr, *, tq):
    i = pl.program_id(2)
    dims = (((1,), (1,)), ((), ()))
    row = lax.broadcasted_iota(jnp.int32, (tq, tq), 0)
    col = lax.broadcasted_iota(jnp.int32, (tq, tq), 1)
    outs = []
    for hh in range(2):
        q = q_ref[0, hh]
        m_scr[...] = jnp.full_like(m_scr, NEG_BIG)
        acc_scr[...] = jnp.zeros_like(acc_scr)

        def block(j, masked):
            ks = pl.ds(pl.multiple_of(j * tq, tq), tq)
            s = lax.dot_general(q, k_ref[0, hh, ks, :], dims, preferred_element_type=F32)
            if masked:
                s = jnp.where(col <= row, s, NEG_BIG)
            m_old = m_scr[...]
            m_new = jnp.maximum(m_old, jnp.max(s, axis=-1, keepdims=True))
            p = jnp.exp(s - m_new)
            alpha = jnp.exp(m_old - m_new)
            acc_scr[...] = alpha * acc_scr[...] + jnp.dot(
                p.astype(BF16), v_ref[0, hh, ks, :], preferred_element_type=F32)
            m_scr[...] = m_new

        def body(j, carry):
            block(j, False)
            return carry

        lax.fori_loop(0, i, body, 0)
        block(i, True)
        acc = acc_scr[...]
        outs.append(acc[:, 0:HEAD_DIM] / acc[:, HEAD_DIM:HEAD_DIM + 1])
    o_ref[0] = jnp.concatenate(outs, axis=-1)


def _attention(q, k, v, tq):
    nb, nh, seq, _ = q.shape
    nq = seq // tq
    return pl.pallas_call(
        functools.partial(_attn_body, tq=tq),
        name="fox_attention",
        grid=(nb, nh // 2, nq),
        in_specs=[
            pl.BlockSpec((1, 2, tq, LANES), lambda b, h, i: (b, h, i, 0)),
            pl.BlockSpec((1, 2, seq, LANES), lambda b, h, i: (b, h, 0, 0)),
            pl.BlockSpec((1, 2, seq, LANES), lambda b, h, i: (b, h, 0, 0)),
        ],
        out_specs=pl.BlockSpec((1, tq, LANES), lambda b, h, i: (b, i, h)),
        out_shape=jax.ShapeDtypeStruct((nb, seq, D_B), F32),
        scratch_shapes=[pltpu.VMEM((tq, 1), F32), pltpu.VMEM((tq, LANES), F32)],
        compiler_params=pltpu.CompilerParams(
            dimension_semantics=("parallel", "parallel", "arbitrary"),
            vmem_limit_bytes=VMEM_LIMIT),
    )(q, k, v)


def _merge_out_body(x_ref, ya_ref, yb_ref, yc_ref, w3_ref, b3_ref, wa_ref, wb_ref, wc_ref, wo_ref,
                    g_ref, beta_ref, o_ref, *, alpha):
    x = x_ref[...]
    xb = x.astype(BF16)

    def proj(c0, c1):
        return jnp.dot(xb, w3_ref[:, c0:c1], preferred_element_type=F32) + b3_ref[:, c0:c1]

    ya = (ya_ref[...] * _silu(proj(0, 256))).astype(BF16)
    yc = (yc_ref[...] * _silu(proj(256, 512))).astype(BF16)
    yb = (yb_ref[...] * _silu(proj(512, 1024))).astype(BF16)
    merged = _sigmoid(proj(1024, 2048)) * jnp.dot(ya, wa_ref[...], preferred_element_type=F32)
    merged += _sigmoid(proj(2048, 3072)) * jnp.dot(yb, wb_ref[...], preferred_element_type=F32)
    merged += _sigmoid(proj(3072, 4096)) * jnp.dot(yc, wc_ref[...], preferred_element_type=F32)
    out = jnp.dot(merged.astype(BF16), wo_ref[...], preferred_element_type=F32)
    y = alpha * x + out
    mu = jnp.mean(y, axis=-1, keepdims=True)
    yc0 = y - mu
    var = jnp.mean(yc0 * yc0, axis=-1, keepdims=True)
    o_ref[...] = yc0 * lax.rsqrt(var + LN_EPS) * g_ref[...] + beta_ref[...]


def _merge_out(x2, ya2, yb2, yc2, p, tm, alpha):
    n = x2.shape[0]
    consts = [p["w3"], p["b3"], p["wa"], p["wb"], p["wc"], p["wo"], p["lng"], p["lnb"]]
    row_spec = lambda w: pl.BlockSpec((tm, w), lambda i: (i, 0))
    return pl.pallas_call(
        functools.partial(_merge_out_body, alpha=alpha),
        name="merge_out",
        grid=(n // tm,),
        in_specs=[row_spec(D_MODEL), row_spec(D_A), row_spec(D_B), row_spec(D_C)]
        + [_const_spec(c.shape) for c in consts],
        out_specs=row_spec(D_MODEL),
        out_shape=jax.ShapeDtypeStruct((n, D_MODEL), F32),
        compiler_params=pltpu.CompilerParams(
            dimension_semantics=("parallel",), vmem_limit_bytes=VMEM_LIMIT),
    )(x2, ya2, yb2, yc2, *consts)


def _aug_selectors():
    eq = np.zeros((LANES, N_HEADS * LANES), np.float32)
    ek = np.zeros((LANES, N_HEADS * LANES), np.float32)
    for h in range(N_HEADS):
        base = h * LANES + AUG
        for part in range(3):
            eq[part * 8 + h, base + part] = 1.0
            ek[part * 8 + h, base + 3 + part] = -1.0
            eq[24, base + 3 + part] = 1.0
            ek[24, base + part] = 1.0
    return jnp.asarray(eq, BF16), jnp.asarray(ek, BF16)


def _block_diag(blocks):
    g, r, c = blocks.shape
    eye = jnp.eye(g, dtype=blocks.dtype)
    return jnp.einsum("grc,gk->grkc", blocks, eye).reshape(g * r, g * c)


def _layer_params(l, nb, w_in, b_in, pool_w, pool_scale, abr, abi, bbr, bbi, c_re, c_im, ssm_d,
                  w_glu, b_glu, w_up_a, w_up_b, w_up_c, w_out, ln_g, ln_b, eq, ek):
    w = w_in[l]
    b = b_in[l]
    offs = np.cumsum([0, D_A, D_A, D_B, D_B, D_B, N_HEADS, D_B, D_C, D_C, 3 * D_MODEL])
    za, ga, zq, zk, zv, zf, gb, zc, gc, zg = [slice(int(offs[i]), int(offs[i + 1])) for i in range(10)]
    scale = 1.0 / math.sqrt(HEAD_DIM)
    fpad = LANES - N_HEADS
    w1 = jnp.concatenate([w[:, za], w[:, zc], jnp.pad(w[:, zf], ((0, 0), (0, fpad))),
                          w[:, zq] * scale, w[:, zk], w[:, zv]], axis=1)
    b1 = jnp.concatenate([b[za], b[zc], jnp.pad(b[zf], (0, fpad)), b[zq] * scale, b[zk], b[zv]])
    w3 = jnp.concatenate([w[:, ga], w[:, gc], w[:, gb], w[:, zg]], axis=1)
    b3 = jnp.concatenate([b[ga], b[gc], b[gb], b[zg]])
    row = lambda a: a.reshape(1, -1).astype(F32)
    return {
        "w1": w1.astype(BF16), "b1": row(b1),
        "pw": _block_diag(pool_w[l]).astype(BF16), "ps": row(pool_scale[l]),
        "are": jnp.broadcast_to(abr[l].reshape(1, N_STATE), (nb, N_STATE)),
        "aim": jnp.broadcast_to(abi[l].reshape(1, N_STATE), (nb, N_STATE)),
        "bbr": _block_diag(bbr[l]).astype(BF16),
        "bbi": _block_diag(bbi[l]).astype(BF16),
        "cre": _block_diag(jnp.swapaxes(c_re[l], 1, 2)).astype(BF16),
        "cim": _block_diag(jnp.swapaxes(c_im[l], 1, 2)).astype(BF16),
        "d": row(ssm_d[l]), "wglu": w_glu[l].astype(BF16), "bglu": row(b_glu[l]),
        "eq": eq, "ek": ek,
        "w3": w3.astype(BF16), "b3": row(b3),
        "wa": w_up_a[l].astype(BF16), "wb": w_up_b[l].astype(BF16), "wc": w_up_c[l].astype(BF16),
        "wo": w_out[l].astype(BF16), "lng": row(ln_g[l]), "lnb": row(ln_b[l]),
    }


def _trunk(x, w_in, b_in, pool_w, pool_scale, ssm_a_re, ssm_a_im, ssm_log_dt, ssm_b_re, ssm_b_im,
           ssm_c_re, ssm_c_im, ssm_d, w_glu, b_glu, w_up_a, w_up_b, w_up_c, w_out, ln_g, ln_b,
           *, tt, tq, tm, alpha=None):
    nb, seq, _ = x.shape
    depth = w_in.shape[0]
    if alpha is None:
        alpha = (2.0 * depth) ** 0.25
    abr, abi, bbr, bbi = _ssm_prep(ssm_a_re, ssm_a_im, ssm_log_dt, ssm_b_re, ssm_b_im)
    eq, ek = _aug_selectors()
    h = x
    for l in range(depth):
        p = _layer_params(l, nb, w_in, b_in, pool_w, pool_scale, abr, abi, bbr, bbi, ssm_c_re,
                          ssm_c_im, ssm_d, w_glu, b_glu, w_up_a, w_up_b, w_up_c, w_out, ln_g, ln_b,
                          eq, ek)
        ya, yc, q, k, v = _mix_in(h, p, tt)
        yb = _attention(q, k, v, tq)
        n = nb * seq
        h = _merge_out(h.reshape(n, D_MODEL), ya.reshape(n, D_A), yb.reshape(n, D_B),
                       yc.reshape(n, D_C), p, tm, alpha).reshape(nb, seq, D_MODEL)
    return h


def kernel(x, w_in, b_in, pool_w, pool_scale, ssm_a_re, ssm_a_im, ssm_log_dt, ssm_b_re, ssm_b_im,
           ssm_c_re, ssm_c_im, ssm_d, w_glu, b_glu, w_up_a, w_up_b, w_up_c, w_out, ln_g, ln_b):
    return _trunk(x, w_in, b_in, pool_w, pool_scale, ssm_a_re, ssm_a_im, ssm_log_dt, ssm_b_re,
                  ssm_b_im, ssm_c_re, ssm_c_im, ssm_d, w_glu, b_glu, w_up_a, w_up_b, w_up_c, w_out,
                  ln_g, ln_b, tt=64, tq=256, tm=256)
```

```python
import functools
import math

import jax
import jax.numpy as jnp
import numpy as np
from jax import lax
from jax.experimental import pallas as pl
from jax.experimental.pallas import tpu as pltpu

F32 = jnp.float32
BF16 = jnp.bfloat16

D_MODEL = 1024
N_HEADS = 8
HEAD_DIM = 64
D_A = 256
D_B = N_HEADS * HEAD_DIM
D_C = 256
POOL_WINDOWS = (2, 4, 8, 16)
POOL_GROUP = 64
SSM_GROUPS = 16
SSM_GROUP = 16
SSM_STATE = 64
N_STATE = SSM_GROUPS * SSM_STATE
LN_EPS = 1e-5
LANES = 128
SUBLANES = 8
NEG_BIG = -1e30

ZA_OFF, ZC_OFF, ZF_OFF, Q_OFF, K_OFF, V_OFF = 0, 256, 512, 640, 1152, 1664
C1 = 2176
C3 = 4096
AUG = HEAD_DIM
HALO = 32

VMEM_LIMIT = 56 * 1024 * 1024


def _sigmoid(x):
    return 0.5 * (jnp.tanh(0.5 * x) + 1.0)


def _silu(x):
    return x * _sigmoid(x)


def _gelu_tanh(x):
    return 0.5 * x * (1.0 + jnp.tanh(math.sqrt(2.0 / math.pi) * (x + 0.044715 * (x * x * x))))


def _log_sigmoid(x):
    return jnp.minimum(x, 0.0) - jnp.log(1.0 + jnp.exp(-jnp.abs(x)))


def _ssm_prep_body(are_ref, aim_ref, ldt_ref, bre_ref, bim_ref, abr_ref, abi_ref, bbr_ref, bbi_ref):
    are = are_ref[...]
    aim = aim_ref[...]
    dt = jnp.exp(ldt_ref[...])
    mag = jnp.exp(are * dt)
    ang = aim * dt
    abr = mag * jnp.cos(ang)
    abi = mag * jnp.sin(ang)
    den = are * are + aim * aim
    nr = abr - 1.0
    cre = (nr * are + abi * aim) / den
    cim = (abi * are - nr * aim) / den
    bre = bre_ref[...]
    bim = bim_ref[...]
    abr_ref[...] = abr
    abi_ref[...] = abi
    bbr_ref[...] = cre * bre - cim * bim
    bbi_ref[...] = cre * bim + cim * bre


def _ssm_prep(a_re, a_im, log_dt, b_re, b_im):
    L = a_re.shape[0]
    rows = L * SSM_GROUPS
    wide = SSM_GROUP * SSM_STATE
    tile = lambda a: jnp.tile(a.reshape(rows, SSM_STATE), (1, SSM_GROUP))
    ldt = jnp.broadcast_to(log_dt.reshape(rows, 1), (rows, wide))
    bt = lambda b: jnp.swapaxes(b, 2, 3).reshape(rows, wide)
    outs = pl.pallas_call(
        _ssm_prep_body,
        name="ssm_prep",
        out_shape=[jax.ShapeDtypeStruct((rows, wide), F32)] * 4,
    )(tile(a_re), tile(a_im), ldt, bt(b_re), bt(b_im))
    abr, abi, bbr, bbi = outs
    abr = abr[:, :SSM_STATE].reshape(L, SSM_GROUPS, SSM_STATE)
    abi = abi[:, :SSM_STATE].reshape(L, SSM_GROUPS, SSM_STATE)
    bbr = bbr.reshape(L, SSM_GROUPS, SSM_GROUP, SSM_STATE)
    bbi = bbi.reshape(L, SSM_GROUPS, SSM_GROUP, SSM_STATE)
    return abr, abi, bbr, bbi


def _mix_in_body(x_ref, w1_ref, b1_ref, pw_ref, ps_ref, are_ref, aim_ref, bbr_ref, bbi_ref,
                 cre_ref, cim_ref, d_ref, wglu_ref, bglu_ref, eq_ref, ek_ref,
                 ya_ref, yc_ref, q_ref, k_ref, v_ref,
                 z_scr, ext_scr, lv2_scr, lv4_scr, lv8_scr, slin_scr, slout_scr, utm_scr,
                 xr_scr, xi_scr, stre_scr, stim_scr, stf_scr, *, tt, nb):
    rows = nb * tt
    pitch = tt + 4
    step = pl.program_id(0)

    @pl.when(step == 0)
    def _():
        ext_scr[:, 0:HALO, :] = jnp.zeros((nb, HALO, D_A), F32)
        stre_scr[...] = jnp.zeros_like(stre_scr)
        stim_scr[...] = jnp.zeros_like(stim_scr)
        stf_scr[...] = jnp.zeros_like(stf_scr)

    xb = x_ref[...].reshape(rows, D_MODEL).astype(BF16)
    for c0 in range(0, C1, 256):
        c1 = min(c0 + 256, C1)
        z_scr[:, c0:c1] = (jnp.dot(xb, w1_ref[:, c0:c1], preferred_element_type=F32)
                           + b1_ref[:, c0:c1])

    lane_a = lax.broadcasted_iota(jnp.int32, (tt, D_A), 1)
    row_a = lax.broadcasted_iota(jnp.int32, (tt, D_A), 0)
    win = jnp.where(lane_a < 64, 2, jnp.where(lane_a < 128, 4, jnp.where(lane_a < 192, 8, 16)))
    cnt = jnp.minimum(step * tt + row_a + 1, win).astype(F32)
    ext_len = tt + HALO

    def pool_one(b, carry):
        u = z_scr[pl.ds(pl.multiple_of(b * tt, tt), tt), ZA_OFF:ZA_OFF + D_A]
        ext_scr[b, HALO:ext_len, :] = u
        lv2_scr[8:ext_len, :] = ext_scr[b, 8:ext_len, :] + ext_scr[b, 7:ext_len - 1, :]
        lv4_scr[16:ext_len, :] = lv2_scr[16:ext_len, :] + lv2_scr[14:ext_len - 2, :]
        lv8_scr[24:ext_len, :] = lv4_scr[24:ext_len, :] + lv4_scr[20:ext_len - 4, :]
        s16 = lv8_scr[HALO:ext_len, :] + lv8_scr[HALO - 8:ext_len - 8, :]
        wsum = jnp.where(lane_a < 64, lv2_scr[HALO:ext_len, :],
                         jnp.where(lane_a < 128, lv4_scr[HALO:ext_len, :],
                                   jnp.where(lane_a < 192, lv8_scr[HALO:ext_len, :], s16)))
        pooled = wsum / cnt - u
        ya = jnp.dot(pooled.astype(BF16), pw_ref[...], preferred_element_type=F32) * ps_ref[...]
        ya_ref[b] = ya
        ext_scr[b, 0:HALO, :] = ext_scr[b, tt:ext_len, :]
        return carry

    lax.fori_loop(0, nb, pool_one, 0)

    for b in range(nb):
        r0 = b * tt
        slin_scr[0, b * pitch:b * pitch + tt, :] = z_scr[r0:r0 + tt, ZC_OFF:ZC_OFF + LANES]
        slin_scr[1, b * pitch:b * pitch + tt, :] = z_scr[r0:r0 + tt, ZC_OFF + LANES:ZC_OFF + 2 * LANES]
        slin_scr[2, b * pitch:b * pitch + tt, :] = _log_sigmoid(z_scr[r0:r0 + tt, ZF_OFF:ZF_OFF + LANES])

    def gather_t(t, carry):
        r = pl.ds(pl.multiple_of(t * nb, nb), nb)
        utm_scr[r, 0:LANES] = slin_scr[0, pl.ds(t, nb, stride=pitch), :]
        utm_scr[r, LANES:2 * LANES] = slin_scr[1, pl.ds(t, nb, stride=pitch), :]
        return carry

    lax.fori_loop(0, tt, gather_t, 0, unroll=4)

    u_tm = utm_scr[...]
    ub = u_tm.astype(BF16)
    xr_scr[...] = jnp.dot(ub, bbr_ref[...], preferred_element_type=F32)
    xi_scr[...] = jnp.dot(ub, bbi_ref[...], preferred_element_type=F32)

    ar = are_ref[...]
    ai = aim_ref[...]

    def scan_t(t, carry):
        sr, si, f = carry
        r = pl.ds(pl.multiple_of(t * nb, nb), nb)
        nr = ar * sr - ai * si + xr_scr[r, :]
        ni = ar * si + ai * sr + xi_scr[r, :]
        xr_scr[r, :] = nr
        xi_scr[r, :] = ni
        f = f + slin_scr[2, pl.ds(t, nb, stride=pitch), :]
        slout_scr[2, r, :] = f
        return nr, ni, f

    sr, si, f = lax.fori_loop(0, tt, scan_t, (stre_scr[...], stim_scr[...], stf_scr[...]), unroll=2)
    stre_scr[...] = sr
    stim_scr[...] = si
    stf_scr[...] = f

    y = (jnp.dot(xr_scr[...].astype(BF16), cre_ref[...], preferred_element_type=F32)
         - jnp.dot(xi_scr[...].astype(BF16), cim_ref[...], preferred_element_type=F32))
    y = _gelu_tanh(y + d_ref[...] * u_tm)
    glu = jnp.dot(y.astype(BF16), wglu_ref[...], preferred_element_type=F32) + bglu_ref[...]
    y = y * _sigmoid(glu)
    slout_scr[0] = y[:, 0:LANES]
    slout_scr[1] = y[:, LANES:2 * LANES]

    lane = lax.broadcasted_iota(jnp.int32, (tt, LANES), 1)
    for b in range(nb):
        yc_ref[b, :, 0:LANES] = slout_scr[0, pl.ds(b, tt, stride=nb), :]
        yc_ref[b, :, LANES:2 * LANES] = slout_scr[1, pl.ds(b, tt, stride=nb), :]
        fb = slout_scr[2, pl.ds(b, tt, stride=nb), :]
        hi = fb.astype(BF16).astype(F32)
        r1 = fb - hi
        mid = r1.astype(BF16).astype(F32)
        lo = r1 - mid
        fa = jnp.where(lane < 8, hi,
                       jnp.where(lane < 16, pltpu.roll(mid, 8, axis=1),
                                 jnp.where(lane < 24, pltpu.roll(lo, 16, axis=1),
                                           jnp.where(lane == 24, 1.0, 0.0)))).astype(BF16)
        augq = jnp.dot(fa, eq_ref[...], preferred_element_type=F32)
        augk = jnp.dot(fa, ek_ref[...], preferred_element_type=F32)
        r0 = b * tt
        for h in range(N_HEADS):
            c = (h // 2) * LANES
            zq = z_scr[r0:r0 + tt, Q_OFF + c:Q_OFF + c + LANES]
            zk = z_scr[r0:r0 + tt, K_OFF + c:K_OFF + c + LANES]
            zv = z_scr[r0:r0 + tt, V_OFF + c:V_OFF + c + LANES]
            if h % 2:
                zq = pltpu.roll(zq, HEAD_DIM, axis=1)
                zk = pltpu.roll(zk, HEAD_DIM, axis=1)
                zv = pltpu.roll(zv, HEAD_DIM, axis=1)
            q_ref[b, h] = jnp.where(lane < AUG, zq, augq[:, h * LANES:(h + 1) * LANES]).astype(BF16)
            k_ref[b, h] = jnp.where(lane < AUG, zk, augk[:, h * LANES:(h + 1) * LANES]).astype(BF16)
            v_ref[b, h] = jnp.where(lane < AUG, zv, jnp.where(lane == AUG, 1.0, 0.0)).astype(BF16)


def _const_spec(shape):
    nd = len(shape)
    return pl.BlockSpec(shape, lambda *_: (0,) * nd)


def _mix_in(x, p, tt):
    nb, seq, _ = x.shape
    nt = seq // tt
    rows = nb * tt
    pitch = tt + 4
    consts = [p["w1"], p["b1"], p["pw"], p["ps"], p["are"], p["aim"], p["bbr"], p["bbi"],
              p["cre"], p["cim"], p["d"], p["wglu"], p["bglu"], p["eq"], p["ek"]]
    in_specs = [pl.BlockSpec((nb, tt, D_MODEL), lambda i: (0, i, 0))]
    in_specs += [_const_spec(c.shape) for c in consts]
    out_shape = [
        jax.ShapeDtypeStruct((nb, seq, D_A), F32),
        jax.ShapeDtypeStruct((nb, seq, D_C), F32),
        jax.ShapeDtypeStruct((nb, N_HEADS, seq, LANES), BF16),
        jax.ShapeDtypeStruct((nb, N_HEADS, seq, LANES), BF16),
        jax.ShapeDtypeStruct((nb, N_HEADS, seq, LANES), BF16),
    ]
    out_specs = [
        pl.BlockSpec((nb, tt, D_A), lambda i: (0, i, 0)),
        pl.BlockSpec((nb, tt, D_C), lambda i: (0, i, 0)),
        pl.BlockSpec((nb, N_HEADS, tt, LANES), lambda i: (0, 0, i, 0)),
        pl.BlockSpec((nb, N_HEADS, tt, LANES), lambda i: (0, 0, i, 0)),
        pl.BlockSpec((nb, N_HEADS, tt, LANES), lambda i: (0, 0, i, 0)),
    ]
    scratch = [
        pltpu.VMEM((rows, C1), F32),
        pltpu.VMEM((nb, tt + HALO, D_A), F32),
        pltpu.VMEM((tt + HALO, D_A), F32),
        pltpu.VMEM((tt + HALO, D_A), F32),
        pltpu.VMEM((tt + HALO, D_A), F32),
        pltpu.VMEM((3, nb * pitch, LANES), F32),
        pltpu.VMEM((3, rows, LANES), F32),
        pltpu.VMEM((rows, D_C), F32),
        pltpu.VMEM((rows, N_STATE), F32),
        pltpu.VMEM((rows, N_STATE), F32),
        pltpu.VMEM((nb, N_STATE), F32),
        pltpu.VMEM((nb, N_STATE), F32),
        pltpu.VMEM((nb, LANES), F32),
    ]
    return pl.pallas_call(
        functools.partial(_mix_in_body, tt=tt, nb=nb),
        name="mix_in",
        grid=(nt,),
        in_specs=in_specs,
        out_specs=out_specs,
        out_shape=out_shape,
        scratch_shapes=scratch,
        compiler_params=pltpu.CompilerParams(
            dimension_semantics=("arbitrary",), vmem_limit_bytes=VMEM_LIMIT),
    )(x, *consts)


def _attn_body(q_ref, k_ref, v_ref, o_ref, vt_scr, m_scr, acc_scr, *, t, hp):
    i = pl.program_id(2)
    nblk = v_ref.shape[2] // t
    dims = (((1,), (1,)), ((), ()))

    @pl.when(i == 0)
    def _():
        for hh in range(hp):
            def transpose_v(c, carry):
                rows = pl.ds(pl.multiple_of(c * t, t), t)
                vt_scr[hh, c] = v_ref[0, hh, rows, :].astype(F32).T.astype(BF16)
                return carry

            lax.fori_loop(0, nblk, transpose_v, 0)

    for hh in range(hp):
        m_scr[hh] = jnp.full((1, t), NEG_BIG, F32)
        acc_scr[hh] = jnp.zeros((LANES, t), F32)

    key = lax.broadcasted_iota(jnp.int32, (t, t), 0)
    qry = lax.broadcasted_iota(jnp.int32, (t, t), 1)

    def block(j, masked):
        rows = pl.ds(pl.multiple_of(j * t, t), t)
        for hh in range(hp):
            st = lax.dot_general(k_ref[0, hh, rows, :], q_ref[0, hh], dims,
                                 preferred_element_type=F32)
            if masked:
                st = jnp.where(key <= qry, st, NEG_BIG)
            m_old = m_scr[hh]
            m_new = jnp.maximum(m_old, jnp.max(st, axis=0, keepdims=True))
            p = jnp.exp(st - m_new).astype(BF16)
            alpha = jnp.exp(m_old - m_new)
            acc_scr[hh] = alpha * acc_scr[hh] + jnp.dot(vt_scr[hh, j], p,
                                                        preferred_element_type=F32)
            m_scr[hh] = m_new

    def body(j, carry):
        block(j, False)
        return carry

    lax.fori_loop(0, i, body, 0)
    block(i, True)
    outs = []
    for hh in range(hp):
        acc = acc_scr[hh].T
        outs.append(acc[:, 0:HEAD_DIM] / acc[:, HEAD_DIM:HEAD_DIM + 1])
    o_ref[0] = jnp.concatenate(outs, axis=-1)


def _attention(q, k, v, t, hp):
    nb, nh, seq, _ = q.shape
    nq = seq // t
    return pl.pallas_call(
        functools.partial(_attn_body, t=t, hp=hp),
        name="fox_attention",
        grid=(nb, nh // hp, nq),
        in_specs=[
            pl.BlockSpec((1, hp, t, LANES), lambda b, h, i: (b, h, i, 0)),
            pl.BlockSpec((1, hp, seq, LANES), lambda b, h, i: (b, h, 0, 0)),
            pl.BlockSpec((1, hp, seq, LANES), lambda b, h, i: (b, h, 0, 0)),
        ],
        out_specs=pl.BlockSpec((1, t, hp * HEAD_DIM), lambda b, h, i: (b, i, h)),
        out_shape=jax.ShapeDtypeStruct((nb, seq, D_B), F32),
        scratch_shapes=[
            pltpu.VMEM((hp, seq // t, LANES, t), BF16),
            pltpu.VMEM((hp, 1, t), F32),
            pltpu.VMEM((hp, LANES, t), F32),
        ],
        compiler_params=pltpu.CompilerParams(
            dimension_semantics=("parallel", "parallel", "arbitrary"),
            vmem_limit_bytes=VMEM_LIMIT),
    )(q, k, v)


def _merge_out_body(x_ref, ya_ref, yb_ref, yc_ref, w3_ref, b3_ref, wa_ref, wb_ref, wc_ref, wo_ref,
                    g_ref, beta_ref, o_ref, *, alpha):
    x = x_ref[...]
    xb = x.astype(BF16)

    def proj(c0, c1):
        return jnp.dot(xb, w3_ref[:, c0:c1], preferred_element_type=F32) + b3_ref[:, c0:c1]

    ya = (ya_ref[...] * _silu(proj(0, 256))).astype(BF16)
    yc = (yc_ref[...] * _silu(proj(256, 512))).astype(BF16)
    yb = (yb_ref[...] * _silu(proj(512, 1024))).astype(BF16)
    merged = _sigmoid(proj(1024, 2048)) * jnp.dot(ya, wa_ref[...], preferred_element_type=F32)
    merged += _sigmoid(proj(2048, 3072)) * jnp.dot(yb, wb_ref[...], preferred_element_type=F32)
    merged += _sigmoid(proj(3072, 4096)) * jnp.dot(yc, wc_ref[...], preferred_element_type=F32)
    out = jnp.dot(merged.astype(BF16), wo_ref[...], preferred_element_type=F32)
    y = alpha * x + out
    mu = jnp.mean(y, axis=-1, keepdims=True)
    yc0 = y - mu
    var = jnp.mean(yc0 * yc0, axis=-1, keepdims=True)
    o_ref[...] = yc0 * lax.rsqrt(var + LN_EPS) * g_ref[...] + beta_ref[...]


def _merge_out(x2, ya2, yb2, yc2, p, tm, alpha):
    n = x2.shape[0]
    consts = [p["w3"], p["b3"], p["wa"], p["wb"], p["wc"], p["wo"], p["lng"], p["lnb"]]
    row_spec = lambda w: pl.BlockSpec((tm, w), lambda i: (i, 0))
    return pl.pallas_call(
        functools.partial(_merge_out_body, alpha=alpha),
        name="merge_out",
        grid=(n // tm,),
        in_specs=[row_spec(D_MODEL), row_spec(D_A), row_spec(D_B), row_spec(D_C)]
        + [_const_spec(c.shape) for c in consts],
        out_specs=row_spec(D_MODEL),
        out_shape=jax.ShapeDtypeStruct((n, D_MODEL), F32),
        compiler_params=pltpu.CompilerParams(
            dimension_semantics=("parallel",), vmem_limit_bytes=VMEM_LIMIT),
    )(x2, ya2, yb2, yc2, *consts)


def _aug_selectors():
    eq = np.zeros((LANES, N_HEADS * LANES), np.float32)
    ek = np.zeros((LANES, N_HEADS * LANES), np.float32)
    for h in range(N_HEADS):
        base = h * LANES + AUG
        for part in range(3):
            eq[part * 8 + h, base + part] = 1.0
            ek[part * 8 + h, base + 3 + part] = -1.0
            eq[24, base + 3 + part] = 1.0
            ek[24, base + part] = 1.0
    return jnp.asarray(eq, BF16), jnp.asarray(ek, BF16)


def _block_diag(blocks):
    g, r, c = blocks.shape
    eye = jnp.eye(g, dtype=blocks.dtype)
    return jnp.einsum("grc,gk->grkc", blocks, eye).reshape(g * r, g * c)


def _layer_params(l, nb, w_in, b_in, pool_w, pool_scale, abr, abi, bbr, bbi, c_re, c_im, ssm_d,
                  w_glu, b_glu, w_up_a, w_up_b, w_up_c, w_out, ln_g, ln_b, eq, ek):
    w = w_in[l]
    b = b_in[l]
    offs = np.cumsum([0, D_A, D_A, D_B, D_B, D_B, N_HEADS, D_B, D_C, D_C, 3 * D_MODEL])
    za, ga, zq, zk, zv, zf, gb, zc, gc, zg = [slice(int(offs[i]), int(offs[i + 1])) for i in range(10)]
    scale = 1.0 / math.sqrt(HEAD_DIM)
    fpad = LANES - N_HEADS
    w1 = jnp.concatenate([w[:, za], w[:, zc], jnp.pad(w[:, zf], ((0, 0), (0, fpad))),
                          w[:, zq] * scale, w[:, zk], w[:, zv]], axis=1)
    b1 = jnp.concatenate([b[za], b[zc], jnp.pad(b[zf], (0, fpad)), b[zq] * scale, b[zk], b[zv]])
    w3 = jnp.concatenate([w[:, ga], w[:, gc], w[:, gb], w[:, zg]], axis=1)
    b3 = jnp.concatenate([b[ga], b[gc], b[gb], b[zg]])
    row = lambda a: a.reshape(1, -1).astype(F32)
    return {
        "w1": w1.astype(BF16), "b1": row(b1),
        "pw": _block_diag(pool_w[l]).astype(BF16), "ps": row(pool_scale[l]),
        "are": jnp.broadcast_to(abr[l].reshape(1, N_STATE), (nb, N_STATE)),
        "aim": jnp.broadcast_to(abi[l].reshape(1, N_STATE), (nb, N_STATE)),
        "bbr": _block_diag(bbr[l]).astype(BF16),
        "bbi": _block_diag(bbi[l]).astype(BF16),
        "cre": _block_diag(jnp.swapaxes(c_re[l], 1, 2)).astype(BF16),
        "cim": _block_diag(jnp.swapaxes(c_im[l], 1, 2)).astype(BF16),
        "d": row(ssm_d[l]), "wglu": w_glu[l].astype(BF16), "bglu": row(b_glu[l]),
        "eq": eq, "ek": ek,
        "w3": w3.astype(BF16), "b3": row(b3),
        "wa": w_up_a[l].astype(BF16), "wb": w_up_b[l].astype(BF16), "wc": w_up_c[l].astype(BF16),
        "wo": w_out[l].astype(BF16), "lng": row(ln_g[l]), "lnb": row(ln_b[l]),
    }


def _trunk(x, w_in, b_in, pool_w, pool_scale, ssm_a_re, ssm_a_im, ssm_log_dt, ssm_b_re, ssm_b_im,
           ssm_c_re, ssm_c_im, ssm_d, w_glu, b_glu, w_up_a, w_up_b, w_up_c, w_out, ln_g, ln_b,
           *, tt, tq, tm, hp=4, alpha=None):
    nb, seq, _ = x.shape
    depth = w_in.shape[0]
    if alpha is None:
        alpha = (2.0 * depth) ** 0.25
    abr, abi, bbr, bbi = _ssm_prep(ssm_a_re, ssm_a_im, ssm_log_dt, ssm_b_re, ssm_b_im)
    eq, ek = _aug_selectors()
    h = x
    for l in range(depth):
        p = _layer_params(l, nb, w_in, b_in, pool_w, pool_scale, abr, abi, bbr, bbi, ssm_c_re,
                          ssm_c_im, ssm_d, w_glu, b_glu, w_up_a, w_up_b, w_up_c, w_out, ln_g, ln_b,
                          eq, ek)
        ya, yc, q, k, v = _mix_in(h, p, tt)
        yb = _attention(q, k, v, tq, hp)
        n = nb * seq
        h = _merge_out(h.reshape(n, D_MODEL), ya.reshape(n, D_A), yb.reshape(n, D_B),
                       yc.reshape(n, D_C), p, tm, alpha).reshape(nb, seq, D_MODEL)
    return h


def kernel(x, w_in, b_in, pool_w, pool_scale, ssm_a_re, ssm_a_im, ssm_log_dt, ssm_b_re, ssm_b_im,
           ssm_c_re, ssm_c_im, ssm_d, w_glu, b_glu, w_up_a, w_up_b, w_up_c, w_out, ln_g, ln_b):
    return _trunk(x, w_in, b_in, pool_w, pool_scale, ssm_a_re, ssm_a_im, ssm_log_dt, ssm_b_re,
                  ssm_b_im, ssm_c_re, ssm_c_im, ssm_d, w_glu, b_glu, w_up_a, w_up_b, w_up_c, w_out,
                  ln_g, ln_b, tt=64, tq=512, tm=256)
```

```python
import functools
import math

import jax
import jax.numpy as jnp
import numpy as np
from jax import lax
from jax.experimental import pallas as pl
from jax.experimental.pallas import tpu as pltpu

F32 = jnp.float32
BF16 = jnp.bfloat16

D_MODEL = 1024
N_HEADS = 8
HEAD_DIM = 64
D_A = 256
D_B = N_HEADS * HEAD_DIM
D_C = 256
POOL_WINDOWS = (2, 4, 8, 16)
POOL_GROUP = 64
SSM_GROUPS = 16
SSM_GROUP = 16
SSM_STATE = 64
N_STATE = SSM_GROUPS * SSM_STATE
LN_EPS = 1e-5
LANES = 128
SUBLANES = 8
NEG_BIG = -1e30
LOG2E = math.log2(math.e)
V_ROWS = 80

ZA_OFF, ZC_OFF, ZF_OFF, Q_OFF, K_OFF, V_OFF = 0, 256, 512, 640, 1152, 1664
C1 = 2176
C3 = 4096
AUG = HEAD_DIM
HALO = 32

VMEM_LIMIT = 56 * 1024 * 1024


def _sigmoid(x):
    return 0.5 * (jnp.tanh(0.5 * x) + 1.0)


def _silu(x):
    return x * _sigmoid(x)


def _gelu_tanh(x):
    return 0.5 * x * (1.0 + jnp.tanh(math.sqrt(2.0 / math.pi) * (x + 0.044715 * (x * x * x))))


def _log_sigmoid(x):
    return jnp.minimum(x, 0.0) - jnp.log(1.0 + jnp.exp(-jnp.abs(x)))


def _ssm_prep_body(are_ref, aim_ref, ldt_ref, bre_ref, bim_ref, abr_ref, abi_ref, bbr_ref, bbi_ref):
    are = are_ref[...]
    aim = aim_ref[...]
    dt = jnp.exp(ldt_ref[...])
    mag = jnp.exp(are * dt)
    ang = aim * dt
    abr = mag * jnp.cos(ang)
    abi = mag * jnp.sin(ang)
    den = are * are + aim * aim
    nr = abr - 1.0
    cre = (nr * are + abi * aim) / den
    cim = (abi * are - nr * aim) / den
    bre = bre_ref[...]
    bim = bim_ref[...]
    abr_ref[...] = abr
    abi_ref[...] = abi
    bbr_ref[...] = cre * bre - cim * bim
    bbi_ref[...] = cre * bim + cim * bre


def _ssm_prep(a_re, a_im, log_dt, b_re, b_im):
    L = a_re.shape[0]
    rows = L * SSM_GROUPS
    wide = SSM_GROUP * SSM_STATE
    tile = lambda a: jnp.tile(a.reshape(rows, SSM_STATE), (1, SSM_GROUP))
    ldt = jnp.broadcast_to(log_dt.reshape(rows, 1), (rows, wide))
    bt = lambda b: jnp.swapaxes(b, 2, 3).reshape(rows, wide)
    outs = pl.pallas_call(
        _ssm_prep_body,
        name="ssm_prep",
        out_shape=[jax.ShapeDtypeStruct((rows, wide), F32)] * 4,
    )(tile(a_re), tile(a_im), ldt, bt(b_re), bt(b_im))
    abr, abi, bbr, bbi = outs
    abr = abr[:, :SSM_STATE].reshape(L, SSM_GROUPS, SSM_STATE)
    abi = abi[:, :SSM_STATE].reshape(L, SSM_GROUPS, SSM_STATE)
    bbr = bbr.reshape(L, SSM_GROUPS, SSM_GROUP, SSM_STATE)
    bbi = bbi.reshape(L, SSM_GROUPS, SSM_GROUP, SSM_STATE)
    return abr, abi, bbr, bbi


def _mix_in_body(x_ref, w1_ref, b1_ref, pw_ref, ps_ref, are_ref, aim_ref, bbr_ref, bbi_ref,
                 cre_ref, cim_ref, d_ref, wglu_ref, bglu_ref, eq_ref, ek_ref,
                 ya_ref, yc_ref, q_ref, k_ref, v_ref,
                 z_scr, ext_scr, lv2_scr, lv4_scr, lv8_scr, slin_scr, slout_scr, utm_scr,
                 xr_scr, xi_scr, stre_scr, stim_scr, stf_scr, *, tt, nb):
    rows = nb * tt
    pitch = tt + 4
    step = pl.program_id(0)

    @pl.when(step == 0)
    def _():
        ext_scr[:, 0:HALO, :] = jnp.zeros((nb, HALO, D_A), F32)
        stre_scr[...] = jnp.zeros_like(stre_scr)
        stim_scr[...] = jnp.zeros_like(stim_scr)
        stf_scr[...] = jnp.zeros_like(stf_scr)

    xb = x_ref[...].reshape(rows, D_MODEL).astype(BF16)
    for c0 in range(0, C1, 256):
        c1 = min(c0 + 256, C1)
        z_scr[:, c0:c1] = (jnp.dot(xb, w1_ref[:, c0:c1], preferred_element_type=F32)
                           + b1_ref[:, c0:c1])

    lane_a = lax.broadcasted_iota(jnp.int32, (tt, D_A), 1)
    row_a = lax.broadcasted_iota(jnp.int32, (tt, D_A), 0)
    win = jnp.where(lane_a < 64, 2, jnp.where(lane_a < 128, 4, jnp.where(lane_a < 192, 8, 16)))
    cnt = jnp.minimum(step * tt + row_a + 1, win).astype(F32)
    ext_len = tt + HALO

    def pool_one(b, carry):
        u = z_scr[pl.ds(pl.multiple_of(b * tt, tt), tt), ZA_OFF:ZA_OFF + D_A]
        ext_scr[b, HALO:ext_len, :] = u
        lv2_scr[8:ext_len, :] = ext_scr[b, 8:ext_len, :] + ext_scr[b, 7:ext_len - 1, :]
        lv4_scr[16:ext_len, :] = lv2_scr[16:ext_len, :] + lv2_scr[14:ext_len - 2, :]
        lv8_scr[24:ext_len, :] = lv4_scr[24:ext_len, :] + lv4_scr[20:ext_len - 4, :]
        s16 = lv8_scr[HALO:ext_len, :] + lv8_scr[HALO - 8:ext_len - 8, :]
        wsum = jnp.where(lane_a < 64, lv2_scr[HALO:ext_len, :],
                         jnp.where(lane_a < 128, lv4_scr[HALO:ext_len, :],
                                   jnp.where(lane_a < 192, lv8_scr[HALO:ext_len, :], s16)))
        pooled = wsum / cnt - u
        ya = jnp.dot(pooled.astype(BF16), pw_ref[...], preferred_element_type=F32) * ps_ref[...]
        ya_ref[b] = ya
        ext_scr[b, 0:HALO, :] = ext_scr[b, tt:ext_len, :]
        return carry

    lax.fori_loop(0, nb, pool_one, 0)

    for b in range(nb):
        r0 = b * tt
        slin_scr[0, b * pitch:b * pitch + tt, :] = z_scr[r0:r0 + tt, ZC_OFF:ZC_OFF + LANES]
        slin_scr[1, b * pitch:b * pitch + tt, :] = z_scr[r0:r0 + tt, ZC_OFF + LANES:ZC_OFF + 2 * LANES]
        slin_scr[2, b * pitch:b * pitch + tt, :] = _log_sigmoid(z_scr[r0:r0 + tt, ZF_OFF:ZF_OFF + LANES])

    def gather_t(t, carry):
        r = pl.ds(pl.multiple_of(t * nb, nb), nb)
        utm_scr[r, 0:LANES] = slin_scr[0, pl.ds(t, nb, stride=pitch), :]
        utm_scr[r, LANES:2 * LANES] = slin_scr[1, pl.ds(t, nb, stride=pitch), :]
        return carry

    lax.fori_loop(0, tt, gather_t, 0, unroll=4)

    u_tm = utm_scr[...]
    ub = u_tm.astype(BF16)
    xr_scr[...] = jnp.dot(ub, bbr_ref[...], preferred_element_type=F32)
    xi_scr[...] = jnp.dot(ub, bbi_ref[...], preferred_element_type=F32)

    ar = are_ref[...]
    ai = aim_ref[...]

    def scan_t(t, carry):
        sr, si, f = carry
        r = pl.ds(pl.multiple_of(t * nb, nb), nb)
        nr = ar * sr - ai * si + xr_scr[r, :]
        ni = ar * si + ai * sr + xi_scr[r, :]
        xr_scr[r, :] = nr
        xi_scr[r, :] = ni
        f = f + slin_scr[2, pl.ds(t, nb, stride=pitch), :]
        slout_scr[2, r, :] = f
        return nr, ni, f

    sr, si, f = lax.fori_loop(0, tt, scan_t, (stre_scr[...], stim_scr[...], stf_scr[...]), unroll=2)
    stre_scr[...] = sr
    stim_scr[...] = si
    stf_scr[...] = f

    y = (jnp.dot(xr_scr[...].astype(BF16), cre_ref[...], preferred_element_type=F32)
         - jnp.dot(xi_scr[...].astype(BF16), cim_ref[...], preferred_element_type=F32))
    y = _gelu_tanh(y + d_ref[...] * u_tm)
    glu = jnp.dot(y.astype(BF16), wglu_ref[...], preferred_element_type=F32) + bglu_ref[...]
    y = y * _sigmoid(glu)
    slout_scr[0] = y[:, 0:LANES]
    slout_scr[1] = y[:, LANES:2 * LANES]

    lane = lax.broadcasted_iota(jnp.int32, (tt, LANES), 1)
    for b in range(nb):
        yc_ref[b, :, 0:LANES] = slout_scr[0, pl.ds(b, tt, stride=nb), :]
        yc_ref[b, :, LANES:2 * LANES] = slout_scr[1, pl.ds(b, tt, stride=nb), :]
        fb = slout_scr[2, pl.ds(b, tt, stride=nb), :] * LOG2E
        hi = fb.astype(BF16).astype(F32)
        r1 = fb - hi
        mid = r1.astype(BF16).astype(F32)
        lo = r1 - mid
        fa = jnp.where(lane < 8, hi,
                       jnp.where(lane < 16, pltpu.roll(mid, 8, axis=1),
                                 jnp.where(lane < 24, pltpu.roll(lo, 16, axis=1),
                                           jnp.where(lane == 24, 1.0, 0.0)))).astype(BF16)
        augq = jnp.dot(fa, eq_ref[...], preferred_element_type=F32)
        augk = jnp.dot(fa, ek_ref[...], preferred_element_type=F32)
        r0 = b * tt
        for h in range(N_HEADS):
            c = (h // 2) * LANES
            zq = z_scr[r0:r0 + tt, Q_OFF + c:Q_OFF + c + LANES]
            zk = z_scr[r0:r0 + tt, K_OFF + c:K_OFF + c + LANES]
            zv = z_scr[r0:r0 + tt, V_OFF + c:V_OFF + c + LANES]
            if h % 2:
                zq = pltpu.roll(zq, HEAD_DIM, axis=1)
                zk = pltpu.roll(zk, HEAD_DIM, axis=1)
                zv = pltpu.roll(zv, HEAD_DIM, axis=1)
            q_ref[b, h] = jnp.where(lane < AUG, zq, augq[:, h * LANES:(h + 1) * LANES]).astype(BF16)
            k_ref[b, h] = jnp.where(lane < AUG, zk, augk[:, h * LANES:(h + 1) * LANES]).astype(BF16)
            v_ref[b, h] = jnp.where(lane < AUG, zv, jnp.where(lane == AUG, 1.0, 0.0)).astype(BF16)


def _const_spec(shape):
    nd = len(shape)
    return pl.BlockSpec(shape, lambda *_: (0,) * nd)


def _mix_in(x, p, tt):
    nb, seq, _ = x.shape
    nt = seq // tt
    rows = nb * tt
    pitch = tt + 4
    consts = [p["w1"], p["b1"], p["pw"], p["ps"], p["are"], p["aim"], p["bbr"], p["bbi"],
              p["cre"], p["cim"], p["d"], p["wglu"], p["bglu"], p["eq"], p["ek"]]
    in_specs = [pl.BlockSpec((nb, tt, D_MODEL), lambda i: (0, i, 0))]
    in_specs += [_const_spec(c.shape) for c in consts]
    out_shape = [
        jax.ShapeDtypeStruct((nb, seq, D_A), F32),
        jax.ShapeDtypeStruct((nb, seq, D_C), F32),
        jax.ShapeDtypeStruct((nb, N_HEADS, seq, LANES), BF16),
        jax.ShapeDtypeStruct((nb, N_HEADS, seq, LANES), BF16),
        jax.ShapeDtypeStruct((nb, N_HEADS, seq, LANES), BF16),
    ]
    out_specs = [
        pl.BlockSpec((nb, tt, D_A), lambda i: (0, i, 0)),
        pl.BlockSpec((nb, tt, D_C), lambda i: (0, i, 0)),
        pl.BlockSpec((nb, N_HEADS, tt, LANES), lambda i: (0, 0, i, 0)),
        pl.BlockSpec((nb, N_HEADS, tt, LANES), lambda i: (0, 0, i, 0)),
        pl.BlockSpec((nb, N_HEADS, tt, LANES), lambda i: (0, 0, i, 0)),
    ]
    scratch = [
        pltpu.VMEM((rows, C1), F32),
        pltpu.VMEM((nb, tt + HALO, D_A), F32),
        pltpu.VMEM((tt + HALO, D_A), F32),
        pltpu.VMEM((tt + HALO, D_A), F32),
        pltpu.VMEM((tt + HALO, D_A), F32),
        pltpu.VMEM((3, nb * pitch, LANES), F32),
        pltpu.VMEM((3, rows, LANES), F32),
        pltpu.VMEM((rows, D_C), F32),
        pltpu.VMEM((rows, N_STATE), F32),
        pltpu.VMEM((rows, N_STATE), F32),
        pltpu.VMEM((nb, N_STATE), F32),
        pltpu.VMEM((nb, N_STATE), F32),
        pltpu.VMEM((nb, LANES), F32),
    ]
    return pl.pallas_call(
        functools.partial(_mix_in_body, tt=tt, nb=nb),
        name="mix_in",
        grid=(nt,),
        in_specs=in_specs,
        out_specs=out_specs,
        out_shape=out_shape,
        scratch_shapes=scratch,
        compiler_params=pltpu.CompilerParams(
            dimension_semantics=("arbitrary",), vmem_limit_bytes=VMEM_LIMIT),
    )(x, *consts)


def _attn_body(q_ref, k_ref, v_ref, o_ref, vt_scr, m_scr, acc_scr, st_scr, *, t, hp):
    i = pl.program_id(2)
    nblk = v_ref.shape[2] // t
    dims = (((1,), (1,)), ((), ()))

    @pl.when(i == 0)
    def _():
        for hh in range(hp):
            def transpose_v(c, carry):
                rows = pl.ds(pl.multiple_of(c * t, t), t)
                vt_scr[hh, c] = v_ref[0, hh, rows, :].astype(F32).T[0:V_ROWS].astype(BF16)
                return carry

            lax.fori_loop(0, nblk, transpose_v, 0)

    for hh in range(hp):
        m_scr[hh] = jnp.full((1, t), NEG_BIG, F32)
        acc_scr[hh] = jnp.zeros((LANES, t), F32)

    key = lax.broadcasted_iota(jnp.int32, (t, t), 0)
    qry = lax.broadcasted_iota(jnp.int32, (t, t), 1)

    def scores(hh, j):
        rows = pl.ds(pl.multiple_of(j * t, t), t)
        return lax.dot_general(k_ref[0, hh, rows, :], q_ref[0, hh], dims,
                               preferred_element_type=F32)

    st_scr[...] = scores(0, 0)

    def block(j, masked):
        st_next = st_scr[...]
        for hh in range(hp):
            st = st_next
            if hh + 1 < hp:
                st_next = scores(hh + 1, j)
            elif not masked:
                st_scr[...] = scores(0, j + 1)
            if masked:
                st = jnp.where(key <= qry, st, NEG_BIG)
            m_old = m_scr[hh]
            m_new = jnp.maximum(m_old, jnp.max(st, axis=0, keepdims=True))
            p = jnp.exp2(st - m_new).astype(BF16)
            alpha = jnp.exp2(m_old - m_new)
            acc_scr[hh, 0:V_ROWS, :] = alpha * acc_scr[hh, 0:V_ROWS, :] + jnp.dot(
                vt_scr[hh, j], p, preferred_element_type=F32)
            m_scr[hh] = m_new

    def body(j, carry):
        block(j, False)
        return carry

    lax.fori_loop(0, i, body, 0)
    block(i, True)
    outs = []
    for hh in range(hp):
        acc = acc_scr[hh].T
        outs.append(acc[:, 0:HEAD_DIM] / acc[:, HEAD_DIM:HEAD_DIM + 1])
    o_ref[0] = jnp.concatenate(outs, axis=-1)


def _attention(q, k, v, t, hp):
    nb, nh, seq, _ = q.shape
    nq = seq // t
    return pl.pallas_call(
        functools.partial(_attn_body, t=t, hp=hp),
        name="fox_attention",
        grid=(nb, nh // hp, nq),
        in_specs=[
            pl.BlockSpec((1, hp, t, LANES), lambda b, h, i: (b, h, i, 0)),
            pl.BlockSpec((1, hp, seq, LANES), lambda b, h, i: (b, h, 0, 0)),
            pl.BlockSpec((1, hp, seq, LANES), lambda b, h, i: (b, h, 0, 0)),
        ],
        out_specs=pl.BlockSpec((1, t, hp * HEAD_DIM), lambda b, h, i: (b, i, h)),
        out_shape=jax.ShapeDtypeStruct((nb, seq, D_B), F32),
        scratch_shapes=[
            pltpu.VMEM((hp, seq // t, V_ROWS, t), BF16),
            pltpu.VMEM((hp, 1, t), F32),
            pltpu.VMEM((hp, LANES, t), F32),
            pltpu.VMEM((t, t), F32),
        ],
        compiler_params=pltpu.CompilerParams(
            dimension_semantics=("parallel", "parallel", "arbitrary"),
            vmem_limit_bytes=VMEM_LIMIT),
    )(q, k, v)


def _merge_out_body(x_ref, ya_ref, yb_ref, yc_ref, w3_ref, b3_ref, wa_ref, wb_ref, wc_ref, wo_ref,
                    g_ref, beta_ref, o_ref, *, alpha):
    x = x_ref[...]
    xb = x.astype(BF16)

    def proj(c0, c1):
        return jnp.dot(xb, w3_ref[:, c0:c1], preferred_element_type=F32) + b3_ref[:, c0:c1]

    ya = (ya_ref[...] * _silu(proj(0, 256))).astype(BF16)
    yc = (yc_ref[...] * _silu(proj(256, 512))).astype(BF16)
    yb = (yb_ref[...] * _silu(proj(512, 1024))).astype(BF16)
    merged = _sigmoid(proj(1024, 2048)) * jnp.dot(ya, wa_ref[...], preferred_element_type=F32)
    merged += _sigmoid(proj(2048, 3072)) * jnp.dot(yb, wb_ref[...], preferred_element_type=F32)
    merged += _sigmoid(proj(3072, 4096)) * jnp.dot(yc, wc_ref[...], preferred_element_type=F32)
    out = jnp.dot(merged.astype(BF16), wo_ref[...], preferred_element_type=F32)
    y = alpha * x + out
    mu = jnp.mean(y, axis=-1, keepdims=True)
    yc0 = y - mu
    var = jnp.mean(yc0 * yc0, axis=-1, keepdims=True)
    o_ref[...] = yc0 * lax.rsqrt(var + LN_EPS) * g_ref[...] + beta_ref[...]


def _merge_out(x2, ya2, yb2, yc2, p, tm, alpha):
    n = x2.shape[0]
    consts = [p["w3"], p["b3"], p["wa"], p["wb"], p["wc"], p["wo"], p["lng"], p["lnb"]]
    row_spec = lambda w: pl.BlockSpec((tm, w), lambda i: (i, 0))
    return pl.pallas_call(
        functools.partial(_merge_out_body, alpha=alpha),
        name="merge_out",
        grid=(n // tm,),
        in_specs=[row_spec(D_MODEL), row_spec(D_A), row_spec(D_B), row_spec(D_C)]
        + [_const_spec(c.shape) for c in consts],
        out_specs=row_spec(D_MODEL),
        out_shape=jax.ShapeDtypeStruct((n, D_MODEL), F32),
        compiler_params=pltpu.CompilerParams(
            dimension_semantics=("parallel",), vmem_limit_bytes=VMEM_LIMIT),
    )(x2, ya2, yb2, yc2, *consts)


def _aug_selectors():
    eq = np.zeros((LANES, N_HEADS * LANES), np.float32)
    ek = np.zeros((LANES, N_HEADS * LANES), np.float32)
    for h in range(N_HEADS):
        base = h * LANES + AUG
        for part in range(3):
            eq[part * 8 + h, base + part] = 1.0
            ek[part * 8 + h, base + 3 + part] = -1.0
            eq[24, base + 3 + part] = 1.0
            ek[24, base + part] = 1.0
    return jnp.asarray(eq, BF16), jnp.asarray(ek, BF16)


def _block_diag(blocks):
    g, r, c = blocks.shape
    eye = jnp.eye(g, dtype=blocks.dtype)
    return jnp.einsum("grc,gk->grkc", blocks, eye).reshape(g * r, g * c)


def _layer_params(l, nb, w_in, b_in, pool_w, pool_scale, abr, abi, bbr, bbi, c_re, c_im, ssm_d,
                  w_glu, b_glu, w_up_a, w_up_b, w_up_c, w_out, ln_g, ln_b, eq, ek):
    w = w_in[l]
    b = b_in[l]
    offs = np.cumsum([0, D_A, D_A, D_B, D_B, D_B, N_HEADS, D_B, D_C, D_C, 3 * D_MODEL])
    za, ga, zq, zk, zv, zf, gb, zc, gc, zg = [slice(int(offs[i]), int(offs[i + 1])) for i in range(10)]
    scale = LOG2E / math.sqrt(HEAD_DIM)
    fpad = LANES - N_HEADS
    w1 = jnp.concatenate([w[:, za], w[:, zc], jnp.pad(w[:, zf], ((0, 0), (0, fpad))),
                          w[:, zq] * scale, w[:, zk], w[:, zv]], axis=1)
    b1 = jnp.concatenate([b[za], b[zc], jnp.pad(b[zf], (0, fpad)), b[zq] * scale, b[zk], b[zv]])
    w3 = jnp.concatenate([w[:, ga], w[:, gc], w[:, gb], w[:, zg]], axis=1)
    b3 = jnp.concatenate([b[ga], b[gc], b[gb], b[zg]])
    row = lambda a: a.reshape(1, -1).astype(F32)
    return {
        "w1": w1.astype(BF16), "b1": row(b1),
        "pw": _block_diag(pool_w[l]).astype(BF16), "ps": row(pool_scale[l]),
        "are": jnp.broadcast_to(abr[l].reshape(1, N_STATE), (nb, N_STATE)),
        "aim": jnp.broadcast_to(abi[l].reshape(1, N_STATE), (nb, N_STATE)),
        "bbr": _block_diag(bbr[l]).astype(BF16),
        "bbi": _block_diag(bbi[l]).astype(BF16),
        "cre": _block_diag(jnp.swapaxes(c_re[l], 1, 2)).astype(BF16),
        "cim": _block_diag(jnp.swapaxes(c_im[l], 1, 2)).astype(BF16),
        "d": row(ssm_d[l]), "wglu": w_glu[l].astype(BF16), "bglu": row(b_glu[l]),
        "eq": eq, "ek": ek,
        "w3": w3.astype(BF16), "b3": row(b3),
        "wa": w_up_a[l].astype(BF16), "wb": w_up_b[l].astype(BF16), "wc": w_up_c[l].astype(BF16),
        "wo": w_out[l].astype(BF16), "lng": row(ln_g[l]), "lnb": row(ln_b[l]),
    }


def _trunk(x, w_in, b_in, pool_w, pool_scale, ssm_a_re, ssm_a_im, ssm_log_dt, ssm_b_re, ssm_b_im,
           ssm_c_re, ssm_c_im, ssm_d, w_glu, b_glu, w_up_a, w_up_b, w_up_c, w_out, ln_g, ln_b,
           *, tt, tq, tm, hp=4, alpha=None):
    nb, seq, _ = x.shape
    depth = w_in.shape[0]
    if alpha is None:
        alpha = (2.0 * depth) ** 0.25
    abr, abi, bbr, bbi = _ssm_prep(ssm_a_re, ssm_a_im, ssm_log_dt, ssm_b_re, ssm_b_im)
    eq, ek = _aug_selectors()
    h = x
    for l in range(depth):
        p = _layer_params(l, nb, w_in, b_in, pool_w, pool_scale, abr, abi, bbr, bbi, ssm_c_re,
                          ssm_c_im, ssm_d, w_glu, b_glu, w_up_a, w_up_b, w_up_c, w_out, ln_g, ln_b,
                          eq, ek)
        ya, yc, q, k, v = _mix_in(h, p, tt)
        yb = _attention(q, k, v, tq, hp)
        n = nb * seq
        h = _merge_out(h.reshape(n, D_MODEL), ya.reshape(n, D_A), yb.reshape(n, D_B),
                       yc.reshape(n, D_C), p, tm, alpha).reshape(nb, seq, D_MODEL)
    return h


def kernel(x, w_in, b_in, pool_w, pool_scale, ssm_a_re, ssm_a_im, ssm_log_dt, ssm_b_re, ssm_b_im,
           ssm_c_re, ssm_c_im, ssm_d, w_glu, b_glu, w_up_a, w_up_b, w_up_c, w_out, ln_g, ln_b):
    return _trunk(x, w_in, b_in, pool_w, pool_scale, ssm_a_re, ssm_a_im, ssm_log_dt, ssm_b_re,
                  ssm_b_im, ssm_c_re, ssm_c_im, ssm_d, w_glu, b_glu, w_up_a, w_up_b, w_up_c, w_out,
                  ln_g, ln_b, tt=64, tq=512, tm=256)
```

```python
import functools
import math

import jax
import jax.numpy as jnp
import numpy as np
from jax import lax
from jax.experimental import pallas as pl
from jax.experimental.pallas import tpu as pltpu

F32 = jnp.float32
BF16 = jnp.bfloat16

D_MODEL = 1024
N_HEADS = 8
HEAD_DIM = 64
D_A = 256
D_B = N_HEADS * HEAD_DIM
D_C = 256
POOL_WINDOWS = (2, 4, 8, 16)
POOL_GROUP = 64
SSM_GROUPS = 16
SSM_GROUP = 16
SSM_STATE = 64
N_STATE = SSM_GROUPS * SSM_STATE
LN_EPS = 1e-5
LANES = 128
SUBLANES = 8
NEG_BIG = -1e30
LOG2E = math.log2(math.e)
SCORE_LOOKAHEAD = 1
V_ROWS = 80

ZA_OFF, ZC_OFF, ZF_OFF, Q_OFF, K_OFF, V_OFF = 0, 256, 512, 640, 1152, 1664
C1 = 2176
C3 = 4096
AUG = HEAD_DIM
HALO = 32

VMEM_LIMIT = 56 * 1024 * 1024


def _sigmoid(x):
    return 0.5 * (jnp.tanh(0.5 * x) + 1.0)


def _silu(x):
    return x * _sigmoid(x)


def _gelu_tanh(x):
    return 0.5 * x * (1.0 + jnp.tanh(math.sqrt(2.0 / math.pi) * (x + 0.044715 * (x * x * x))))


def _log_sigmoid(x):
    return jnp.minimum(x, 0.0) - jnp.log(1.0 + jnp.exp(-jnp.abs(x)))


def _ssm_prep_body(are_ref, aim_ref, ldt_ref, bre_ref, bim_ref, abr_ref, abi_ref, bbr_ref, bbi_ref):
    are = are_ref[...]
    aim = aim_ref[...]
    dt = jnp.exp(ldt_ref[...])
    mag = jnp.exp(are * dt)
    ang = aim * dt
    abr = mag * jnp.cos(ang)
    abi = mag * jnp.sin(ang)
    den = are * are + aim * aim
    nr = abr - 1.0
    cre = (nr * are + abi * aim) / den
    cim = (abi * are - nr * aim) / den
    bre = bre_ref[...]
    bim = bim_ref[...]
    abr_ref[...] = abr
    abi_ref[...] = abi
    bbr_ref[...] = cre * bre - cim * bim
    bbi_ref[...] = cre * bim + cim * bre


def _ssm_prep(a_re, a_im, log_dt, b_re, b_im):
    L = a_re.shape[0]
    rows = L * SSM_GROUPS
    wide = SSM_GROUP * SSM_STATE
    tile = lambda a: jnp.tile(a.reshape(rows, SSM_STATE), (1, SSM_GROUP))
    ldt = jnp.broadcast_to(log_dt.reshape(rows, 1), (rows, wide))
    bt = lambda b: jnp.swapaxes(b, 2, 3).reshape(rows, wide)
    outs = pl.pallas_call(
        _ssm_prep_body,
        name="ssm_prep",
        out_shape=[jax.ShapeDtypeStruct((rows, wide), F32)] * 4,
    )(tile(a_re), tile(a_im), ldt, bt(b_re), bt(b_im))
    abr, abi, bbr, bbi = outs
    abr = abr[:, :SSM_STATE].reshape(L, SSM_GROUPS, SSM_STATE)
    abi = abi[:, :SSM_STATE].reshape(L, SSM_GROUPS, SSM_STATE)
    bbr = bbr.reshape(L, SSM_GROUPS, SSM_GROUP, SSM_STATE)
    bbi = bbi.reshape(L, SSM_GROUPS, SSM_GROUP, SSM_STATE)
    return abr, abi, bbr, bbi


def _mix_in_body(x_ref, w1_ref, b1_ref, pw_ref, ps_ref, are_ref, aim_ref, bbr_ref, bbi_ref,
                 cre_ref, cim_ref, d_ref, wglu_ref, bglu_ref, eq_ref, ek_ref,
                 ya_ref, yc_ref, q_ref, k_ref, v_ref,
                 z_scr, ext_scr, lv2_scr, lv4_scr, lv8_scr, slin_scr, slout_scr, utm_scr,
                 xr_scr, xi_scr, stre_scr, stim_scr, stf_scr, *, tt, nb):
    rows = nb * tt
    pitch = tt + 4
    step = pl.program_id(0)

    @pl.when(step == 0)
    def _():
        ext_scr[:, 0:HALO, :] = jnp.zeros((nb, HALO, D_A), F32)
        stre_scr[...] = jnp.zeros_like(stre_scr)
        stim_scr[...] = jnp.zeros_like(stim_scr)
        stf_scr[...] = jnp.zeros_like(stf_scr)

    xb = x_ref[...].reshape(rows, D_MODEL).astype(BF16)
    for c0 in range(0, C1, 256):
        c1 = min(c0 + 256, C1)
        z_scr[:, c0:c1] = (jnp.dot(xb, w1_ref[:, c0:c1], preferred_element_type=F32)
                           + b1_ref[:, c0:c1])

    lane_a = lax.broadcasted_iota(jnp.int32, (tt, D_A), 1)
    row_a = lax.broadcasted_iota(jnp.int32, (tt, D_A), 0)
    win = jnp.where(lane_a < 64, 2, jnp.where(lane_a < 128, 4, jnp.where(lane_a < 192, 8, 16)))
    cnt = jnp.minimum(step * tt + row_a + 1, win).astype(F32)
    ext_len = tt + HALO

    def pool_one(b, carry):
        u = z_scr[pl.ds(pl.multiple_of(b * tt, tt), tt), ZA_OFF:ZA_OFF + D_A]
        ext_scr[b, HALO:ext_len, :] = u
        lv2_scr[8:ext_len, :] = ext_scr[b, 8:ext_len, :] + ext_scr[b, 7:ext_len - 1, :]
        lv4_scr[16:ext_len, :] = lv2_scr[16:ext_len, :] + lv2_scr[14:ext_len - 2, :]
        lv8_scr[24:ext_len, :] = lv4_scr[24:ext_len, :] + lv4_scr[20:ext_len - 4, :]
        s16 = lv8_scr[HALO:ext_len, :] + lv8_scr[HALO - 8:ext_len - 8, :]
        wsum = jnp.where(lane_a < 64, lv2_scr[HALO:ext_len, :],
                         jnp.where(lane_a < 128, lv4_scr[HALO:ext_len, :],
                                   jnp.where(lane_a < 192, lv8_scr[HALO:ext_len, :], s16)))
        pooled = wsum / cnt - u
        ya = jnp.dot(pooled.astype(BF16), pw_ref[...], preferred_element_type=F32) * ps_ref[...]
        ya_ref[b] = ya
        ext_scr[b, 0:HALO, :] = ext_scr[b, tt:ext_len, :]
        return carry

    lax.fori_loop(0, nb, pool_one, 0)

    for b in range(nb):
        r0 = b * tt
        slin_scr[0, b * pitch:b * pitch + tt, :] = z_scr[r0:r0 + tt, ZC_OFF:ZC_OFF + LANES]
        slin_scr[1, b * pitch:b * pitch + tt, :] = z_scr[r0:r0 + tt, ZC_OFF + LANES:ZC_OFF + 2 * LANES]
        slin_scr[2, b * pitch:b * pitch + tt, :] = _log_sigmoid(z_scr[r0:r0 + tt, ZF_OFF:ZF_OFF + LANES])

    def gather_t(t, carry):
        r = pl.ds(pl.multiple_of(t * nb, nb), nb)
        utm_scr[r, 0:LANES] = slin_scr[0, pl.ds(t, nb, stride=pitch), :]
        utm_scr[r, LANES:2 * LANES] = slin_scr[1, pl.ds(t, nb, stride=pitch), :]
        return carry

    lax.fori_loop(0, tt, gather_t, 0, unroll=4)

    u_tm = utm_scr[...]
    ub = u_tm.astype(BF16)
    xr_scr[...] = jnp.dot(ub, bbr_ref[...], preferred_element_type=F32)
    xi_scr[...] = jnp.dot(ub, bbi_ref[...], preferred_element_type=F32)

    ar = are_ref[...]
    ai = aim_ref[...]

    def scan_t(t, carry):
        sr, si, f = carry
        r = pl.ds(pl.multiple_of(t * nb, nb), nb)
        nr = ar * sr - ai * si + xr_scr[r, :]
        ni = ar * si + ai * sr + xi_scr[r, :]
        xr_scr[r, :] = nr
        xi_scr[r, :] = ni
        f = f + slin_scr[2, pl.ds(t, nb, stride=pitch), :]
        slout_scr[2, r, :] = f
        return nr, ni, f

    sr, si, f = lax.fori_loop(0, tt, scan_t, (stre_scr[...], stim_scr[...], stf_scr[...]), unroll=2)
    stre_scr[...] = sr
    stim_scr[...] = si
    stf_scr[...] = f

    y = (jnp.dot(xr_scr[...].astype(BF16), cre_ref[...], preferred_element_type=F32)
         - jnp.dot(xi_scr[...].astype(BF16), cim_ref[...], preferred_element_type=F32))
    y = _gelu_tanh(y + d_ref[...] * u_tm)
    glu = jnp.dot(y.astype(BF16), wglu_ref[...], preferred_element_type=F32) + bglu_ref[...]
    y = y * _sigmoid(glu)
    slout_scr[0] = y[:, 0:LANES]
    slout_scr[1] = y[:, LANES:2 * LANES]

    lane = lax.broadcasted_iota(jnp.int32, (tt, LANES), 1)
    for b in range(nb):
        yc_ref[b, :, 0:LANES] = slout_scr[0, pl.ds(b, tt, stride=nb), :]
        yc_ref[b, :, LANES:2 * LANES] = slout_scr[1, pl.ds(b, tt, stride=nb), :]
        fb = slout_scr[2, pl.ds(b, tt, stride=nb), :] * LOG2E
        hi = fb.astype(BF16).astype(F32)
        r1 = fb - hi
        mid = r1.astype(BF16).astype(F32)
        lo = r1 - mid
        fa = jnp.where(lane < 8, hi,
                       jnp.where(lane < 16, pltpu.roll(mid, 8, axis=1),
                                 jnp.where(lane < 24, pltpu.roll(lo, 16, axis=1),
                                           jnp.where(lane == 24, 1.0, 0.0)))).astype(BF16)
        augq = jnp.dot(fa, eq_ref[...], preferred_element_type=F32)
        augk = jnp.dot(fa, ek_ref[...], preferred_element_type=F32)
        r0 = b * tt
        for h in range(N_HEADS):
            c = (h // 2) * LANES
            zq = z_scr[r0:r0 + tt, Q_OFF + c:Q_OFF + c + LANES]
            zk = z_scr[r0:r0 + tt, K_OFF + c:K_OFF + c + LANES]
            zv = z_scr[r0:r0 + tt, V_OFF + c:V_OFF + c + LANES]
            if h % 2:
                zq = pltpu.roll(zq, HEAD_DIM, axis=1)
                zk = pltpu.roll(zk, HEAD_DIM, axis=1)
                zv = pltpu.roll(zv, HEAD_DIM, axis=1)
            q_ref[b, h] = jnp.where(lane < AUG, zq, augq[:, h * LANES:(h + 1) * LANES]).astype(BF16)
            k_ref[b, h] = jnp.where(lane < AUG, zk, augk[:, h * LANES:(h + 1) * LANES]).astype(BF16)
            v_ref[b, h] = jnp.where(lane < AUG, zv, jnp.where(lane == AUG, 1.0, 0.0)).astype(BF16)


def _const_spec(shape):
    nd = len(shape)
    return pl.BlockSpec(shape, lambda *_: (0,) * nd, pipeline_mode=pl.Buffered(1))


def _mix_in(x, p, tt):
    nb, seq, _ = x.shape
    nt = seq // tt
    rows = nb * tt
    pitch = tt + 4
    consts = [p["w1"], p["b1"], p["pw"], p["ps"], p["are"], p["aim"], p["bbr"], p["bbi"],
              p["cre"], p["cim"], p["d"], p["wglu"], p["bglu"], p["eq"], p["ek"]]
    in_specs = [pl.BlockSpec((nb, tt, D_MODEL), lambda i: (0, i, 0))]
    in_specs += [_const_spec(c.shape) for c in consts]
    out_shape = [
        jax.ShapeDtypeStruct((nb, seq, D_A), F32),
        jax.ShapeDtypeStruct((nb, seq, D_C), F32),
        jax.ShapeDtypeStruct((nb, N_HEADS, seq, LANES), BF16),
        jax.ShapeDtypeStruct((nb, N_HEADS, seq, LANES), BF16),
        jax.ShapeDtypeStruct((nb, N_HEADS, seq, LANES), BF16),
    ]
    out_specs = [
        pl.BlockSpec((nb, tt, D_A), lambda i: (0, i, 0)),
        pl.BlockSpec((nb, tt, D_C), lambda i: (0, i, 0)),
        pl.BlockSpec((nb, N_HEADS, tt, LANES), lambda i: (0, 0, i, 0)),
        pl.BlockSpec((nb, N_HEADS, tt, LANES), lambda i: (0, 0, i, 0)),
        pl.BlockSpec((nb, N_HEADS, tt, LANES), lambda i: (0, 0, i, 0)),
    ]
    scratch = [
        pltpu.VMEM((rows, C1), F32),
        pltpu.VMEM((nb, tt + HALO, D_A), F32),
        pltpu.VMEM((tt + HALO, D_A), F32),
        pltpu.VMEM((tt + HALO, D_A), F32),
        pltpu.VMEM((tt + HALO, D_A), F32),
        pltpu.VMEM((3, nb * pitch, LANES), F32),
        pltpu.VMEM((3, rows, LANES), F32),
        pltpu.VMEM((rows, D_C), F32),
        pltpu.VMEM((rows, N_STATE), F32),
        pltpu.VMEM((rows, N_STATE), F32),
        pltpu.VMEM((nb, N_STATE), F32),
        pltpu.VMEM((nb, N_STATE), F32),
        pltpu.VMEM((nb, LANES), F32),
    ]
    return pl.pallas_call(
        functools.partial(_mix_in_body, tt=tt, nb=nb),
        name="mix_in",
        grid=(nt,),
        in_specs=in_specs,
        out_specs=out_specs,
        out_shape=out_shape,
        scratch_shapes=scratch,
        compiler_params=pltpu.CompilerParams(
            dimension_semantics=("arbitrary",), vmem_limit_bytes=VMEM_LIMIT),
    )(x, *consts)


def _attn_body(q_ref, k_ref, v_ref, o_ref, vt_scr, m_scr, acc_scr, st_scr, *, t, hp):
    i = pl.program_id(2)
    nblk = v_ref.shape[2] // t
    dims = (((1,), (1,)), ((), ()))

    @pl.when(i == 0)
    def _():
        for hh in range(hp):
            def transpose_v(c, carry):
                rows = pl.ds(pl.multiple_of(c * t, t), t)
                vt_scr[hh, c] = v_ref[0, hh, rows, :].astype(F32).T[0:V_ROWS].astype(BF16)
                return carry

            lax.fori_loop(0, nblk, transpose_v, 0)

    for hh in range(hp):
        m_scr[hh] = jnp.full((1, t), NEG_BIG, F32)
        acc_scr[hh] = jnp.zeros((LANES, t), F32)

    key = lax.broadcasted_iota(jnp.int32, (t, t), 0)
    qry = lax.broadcasted_iota(jnp.int32, (t, t), 1)

    def scores(hh, j):
        rows = pl.ds(pl.multiple_of(j * t, t), t)
        return lax.dot_general(k_ref[0, hh, rows, :], q_ref[0, hh], dims,
                               preferred_element_type=F32)

    for hh in range(SCORE_LOOKAHEAD):
        st_scr[hh] = scores(hh, 0)

    def read_scores(hh, masked):
        st = st_scr[hh]
        return jnp.where(key <= qry, st, NEG_BIG) if masked else st

    def block(j, masked):
        for hh in range(hp):
            ahead = hh + SCORE_LOOKAHEAD
            if ahead < hp:
                st_scr[ahead] = scores(ahead, j)
            elif not masked:
                st_scr[ahead - hp] = scores(ahead - hp, j + 1)
            m_old = m_scr[hh]
            m_new = jnp.maximum(m_old, jnp.max(read_scores(hh, masked), axis=0, keepdims=True))
            p = jnp.exp2(read_scores(hh, masked) - m_new).astype(BF16)
            alpha = jnp.exp2(m_old - m_new)
            acc_scr[hh, 0:V_ROWS, :] = alpha * acc_scr[hh, 0:V_ROWS, :] + jnp.dot(
                vt_scr[hh, j], p, preferred_element_type=F32)
            m_scr[hh] = m_new

    def body(j, carry):
        block(j, False)
        return carry

    lax.fori_loop(0, i, body, 0)
    block(i, True)
    outs = []
    for hh in range(hp):
        acc = acc_scr[hh].T
        outs.append(acc[:, 0:HEAD_DIM] / acc[:, HEAD_DIM:HEAD_DIM + 1])
    o_ref[0] = jnp.concatenate(outs, axis=-1)


def _attention(q, k, v, t, hp):
    nb, nh, seq, _ = q.shape
    nq = seq // t
    return pl.pallas_call(
        functools.partial(_attn_body, t=t, hp=hp),
        name="fox_attention",
        grid=(nb, nh // hp, nq),
        in_specs=[
            pl.BlockSpec((1, hp, t, LANES), lambda b, h, i: (b, h, i, 0)),
            pl.BlockSpec((1, hp, seq, LANES), lambda b, h, i: (b, h, 0, 0),
                         pipeline_mode=pl.Buffered(1)),
            pl.BlockSpec((1, hp, seq, LANES), lambda b, h, i: (b, h, 0, 0),
                         pipeline_mode=pl.Buffered(1)),
        ],
        out_specs=pl.BlockSpec((1, t, hp * HEAD_DIM), lambda b, h, i: (b, i, h)),
        out_shape=jax.ShapeDtypeStruct((nb, seq, D_B), F32),
        scratch_shapes=[
            pltpu.VMEM((hp, seq // t, V_ROWS, t), BF16),
            pltpu.VMEM((hp, 1, t), F32),
            pltpu.VMEM((hp, LANES, t), F32),
            pltpu.VMEM((hp, t, t), F32),
        ],
        compiler_params=pltpu.CompilerParams(
            dimension_semantics=("parallel", "parallel", "arbitrary"),
            vmem_limit_bytes=VMEM_LIMIT),
    )(q, k, v)


def _merge_out_body(x_ref, ya_ref, yb_ref, yc_ref, w3_ref, b3_ref, wa_ref, wb_ref, wc_ref, wo_ref,
                    g_ref, beta_ref, o_ref, *, alpha):
    x = x_ref[...]
    xb = x.astype(BF16)

    def proj(c0, c1):
        return jnp.dot(xb, w3_ref[:, c0:c1], preferred_element_type=F32) + b3_ref[:, c0:c1]

    ya = (ya_ref[...] * _silu(proj(0, 256))).astype(BF16)
    yc = (yc_ref[...] * _silu(proj(256, 512))).astype(BF16)
    yb = (yb_ref[...] * _silu(proj(512, 1024))).astype(BF16)
    merged = _sigmoid(proj(1024, 2048)) * jnp.dot(ya, wa_ref[...], preferred_element_type=F32)
    merged += _sigmoid(proj(2048, 3072)) * jnp.dot(yb, wb_ref[...], preferred_element_type=F32)
    merged += _sigmoid(proj(3072, 4096)) * jnp.dot(yc, wc_ref[...], preferred_element_type=F32)
    out = jnp.dot(merged.astype(BF16), wo_ref[...], preferred_element_type=F32)
    y = alpha * x + out
    mu = jnp.mean(y, axis=-1, keepdims=True)
    yc0 = y - mu
    var = jnp.mean(yc0 * yc0, axis=-1, keepdims=True)
    o_ref[...] = yc0 * lax.rsqrt(var + LN_EPS) * g_ref[...] + beta_ref[...]


def _merge_out(x2, ya2, yb2, yc2, p, tm, alpha):
    n = x2.shape[0]
    consts = [p["w3"], p["b3"], p["wa"], p["wb"], p["wc"], p["wo"], p["lng"], p["lnb"]]
    row_spec = lambda w: pl.BlockSpec((tm, w), lambda i: (i, 0))
    return pl.pallas_call(
        functools.partial(_merge_out_body, alpha=alpha),
        name="merge_out",
        grid=(n // tm,),
        in_specs=[row_spec(D_MODEL), row_spec(D_A), row_spec(D_B), row_spec(D_C)]
        + [_const_spec(c.shape) for c in consts],
        out_specs=row_spec(D_MODEL),
        out_shape=jax.ShapeDtypeStruct((n, D_MODEL), F32),
        compiler_params=pltpu.CompilerParams(
            dimension_semantics=("parallel",), vmem_limit_bytes=VMEM_LIMIT),
    )(x2, ya2, yb2, yc2, *consts)


def _aug_selectors():
    eq = np.zeros((LANES, N_HEADS * LANES), np.float32)
    ek = np.zeros((LANES, N_HEADS * LANES), np.float32)
    for h in range(N_HEADS):
        base = h * LANES + AUG
        for part in range(3):
            eq[part * 8 + h, base + part] = 1.0
            ek[part * 8 + h, base + 3 + part] = -1.0
            eq[24, base + 3 + part] = 1.0
            ek[24, base + part] = 1.0
    return jnp.asarray(eq, BF16), jnp.asarray(ek, BF16)


def _block_diag(blocks):
    g, r, c = blocks.shape
    eye = jnp.eye(g, dtype=blocks.dtype)
    return jnp.einsum("grc,gk->grkc", blocks, eye).reshape(g * r, g * c)


def _layer_params(l, nb, w_in, b_in, pool_w, pool_scale, abr, abi, bbr, bbi, c_re, c_im, ssm_d,
                  w_glu, b_glu, w_up_a, w_up_b, w_up_c, w_out, ln_g, ln_b, eq, ek):
    w = w_in[l]
    b = b_in[l]
    offs = np.cumsum([0, D_A, D_A, D_B, D_B, D_B, N_HEADS, D_B, D_C, D_C, 3 * D_MODEL])
    za, ga, zq, zk, zv, zf, gb, zc, gc, zg = [slice(int(offs[i]), int(offs[i + 1])) for i in range(10)]
    scale = LOG2E / math.sqrt(HEAD_DIM)
    fpad = LANES - N_HEADS
    w1 = jnp.concatenate([w[:, za], w[:, zc], jnp.pad(w[:, zf], ((0, 0), (0, fpad))),
                          w[:, zq] * scale, w[:, zk], w[:, zv]], axis=1)
    b1 = jnp.concatenate([b[za], b[zc], jnp.pad(b[zf], (0, fpad)), b[zq] * scale, b[zk], b[zv]])
    w3 = jnp.concatenate([w[:, ga], w[:, gc], w[:, gb], w[:, zg]], axis=1)
    b3 = jnp.concatenate([b[ga], b[gc], b[gb], b[zg]])
    row = lambda a: a.reshape(1, -1).astype(F32)
    return {
        "w1": w1.astype(BF16), "b1": row(b1),
        "pw": _block_diag(pool_w[l]).astype(BF16), "ps": row(pool_scale[l]),
        "are": jnp.broadcast_to(abr[l].reshape(1, N_STATE), (nb, N_STATE)),
        "aim": jnp.broadcast_to(abi[l].reshape(1, N_STATE), (nb, N_STATE)),
        "bbr": _block_diag(bbr[l]).astype(BF16),
        "bbi": _block_diag(bbi[l]).astype(BF16),
        "cre": _block_diag(jnp.swapaxes(c_re[l], 1, 2)).astype(BF16),
        "cim": _block_diag(jnp.swapaxes(c_im[l], 1, 2)).astype(BF16),
        "d": row(ssm_d[l]), "wglu": w_glu[l].astype(BF16), "bglu": row(b_glu[l]),
        "eq": eq, "ek": ek,
        "w3": w3.astype(BF16), "b3": row(b3),
        "wa": w_up_a[l].astype(BF16), "wb": w_up_b[l].astype(BF16), "wc": w_up_c[l].astype(BF16),
        "wo": w_out[l].astype(BF16), "lng": row(ln_g[l]), "lnb": row(ln_b[l]),
    }


def _trunk(x, w_in, b_in, pool_w, pool_scale, ssm_a_re, ssm_a_im, ssm_log_dt, ssm_b_re, ssm_b_im,
           ssm_c_re, ssm_c_im, ssm_d, w_glu, b_glu, w_up_a, w_up_b, w_up_c, w_out, ln_g, ln_b,
           *, tt, tq, tm, hp=8, alpha=None):
    nb, seq, _ = x.shape
    depth = w_in.shape[0]
    if alpha is None:
        alpha = (2.0 * depth) ** 0.25
    abr, abi, bbr, bbi = _ssm_prep(ssm_a_re, ssm_a_im, ssm_log_dt, ssm_b_re, ssm_b_im)
    eq, ek = _aug_selectors()
    h = x
    for l in range(depth):
        p = _layer_params(l, nb, w_in, b_in, pool_w, pool_scale, abr, abi, bbr, bbi, ssm_c_re,
                          ssm_c_im, ssm_d, w_glu, b_glu, w_up_a, w_up_b, w_up_c, w_out, ln_g, ln_b,
                          eq, ek)
        ya, yc, q, k, v = _mix_in(h, p, tt)
        yb = _attention(q, k, v, tq, hp)
        n = nb * seq
        h = _merge_out(h.reshape(n, D_MODEL), ya.reshape(n, D_A), yb.reshape(n, D_B),
                       yc.reshape(n, D_C), p, tm, alpha).reshape(nb, seq, D_MODEL)
    return h


def kernel(x, w_in, b_in, pool_w, pool_scale, ssm_a_re, ssm_a_im, ssm_log_dt, ssm_b_re, ssm_b_im,
           ssm_c_re, ssm_c_im, ssm_d, w_glu, b_glu, w_up_a, w_up_b, w_up_c, w_out, ln_g, ln_b):
    return _trunk(x, w_in, b_in, pool_w, pool_scale, ssm_a_re, ssm_a_im, ssm_log_dt, ssm_b_re,
                  ssm_b_im, ssm_c_re, ssm_c_im, ssm_d, w_glu, b_glu, w_up_a, w_up_b, w_up_c, w_out,
                  ln_g, ln_b, tt=64, tq=512, tm=512)
```

```python
import functools
import math

import jax
import jax.numpy as jnp
import numpy as np
from jax import lax
from jax.experimental import pallas as pl
from jax.experimental.pallas import tpu as pltpu

F32 = jnp.float32
BF16 = jnp.bfloat16

D_MODEL = 1024
N_HEADS = 8
HEAD_DIM = 64
D_A = 256
D_B = N_HEADS * HEAD_DIM
D_C = 256
POOL_WINDOWS = (2, 4, 8, 16)
POOL_GROUP = 64
SSM_GROUPS = 16
SSM_GROUP = 16
SSM_STATE = 64
N_STATE = SSM_GROUPS * SSM_STATE
LN_EPS = 1e-5
LANES = 128
SUBLANES = 8
NEG_BIG = -1e30
LOG2E = math.log2(math.e)
SCORE_LOOKAHEAD = 1
V_ROWS = 80

ZA_OFF, ZC_OFF, ZF_OFF, Q_OFF, K_OFF, V_OFF = 0, 256, 512, 640, 1152, 1664
C1 = 2176
C3 = 4096
AUG = HEAD_DIM
POOL_HISTORY = 16

VMEM_LIMIT = 56 * 1024 * 1024


def _sigmoid(x):
    return 0.5 * (jnp.tanh(0.5 * x) + 1.0)


def _silu(x):
    return x * _sigmoid(x)


def _gelu_tanh(x):
    return 0.5 * x * (1.0 + jnp.tanh(math.sqrt(2.0 / math.pi) * (x + 0.044715 * (x * x * x))))


def _log_sigmoid(x):
    return jnp.minimum(x, 0.0) - jnp.log(1.0 + jnp.exp(-jnp.abs(x)))


def _ssm_prep_body(are_ref, aim_ref, ldt_ref, bre_ref, bim_ref, abr_ref, abi_ref, bbr_ref, bbi_ref):
    are = are_ref[...]
    aim = aim_ref[...]
    dt = jnp.exp(ldt_ref[...])
    mag = jnp.exp(are * dt)
    ang = aim * dt
    abr = mag * jnp.cos(ang)
    abi = mag * jnp.sin(ang)
    den = are * are + aim * aim
    nr = abr - 1.0
    cre = (nr * are + abi * aim) / den
    cim = (abi * are - nr * aim) / den
    bre = bre_ref[...]
    bim = bim_ref[...]
    abr_ref[...] = abr
    abi_ref[...] = abi
    bbr_ref[...] = cre * bre - cim * bim
    bbi_ref[...] = cre * bim + cim * bre


def _ssm_prep(a_re, a_im, log_dt, b_re, b_im):
    L = a_re.shape[0]
    rows = L * SSM_GROUPS
    wide = SSM_GROUP * SSM_STATE
    tile = lambda a: jnp.tile(a.reshape(rows, SSM_STATE), (1, SSM_GROUP))
    ldt = jnp.broadcast_to(log_dt.reshape(rows, 1), (rows, wide))
    bt = lambda b: jnp.swapaxes(b, 2, 3).reshape(rows, wide)
    outs = pl.pallas_call(
        _ssm_prep_body,
        name="ssm_prep",
        out_shape=[jax.ShapeDtypeStruct((rows, wide), F32)] * 4,
    )(tile(a_re), tile(a_im), ldt, bt(b_re), bt(b_im))
    abr, abi, bbr, bbi = outs
    abr = abr[:, :SSM_STATE].reshape(L, SSM_GROUPS, SSM_STATE)
    abi = abi[:, :SSM_STATE].reshape(L, SSM_GROUPS, SSM_STATE)
    bbr = bbr.reshape(L, SSM_GROUPS, SSM_GROUP, SSM_STATE)
    bbi = bbi.reshape(L, SSM_GROUPS, SSM_GROUP, SSM_STATE)
    return abr, abi, bbr, bbi


def _mix_in_body(x_ref, w1_ref, b1_ref, pw_ref, ps_ref, are_ref, aim_ref, bbr_ref, bbi_ref,
                 cre_ref, cim_ref, d_ref, wglu_ref, bglu_ref, eq_ref, ek_ref,
                 ya_ref, yc_ref, q_ref, k_ref, v_ref,
                 z_scr, ext_scr, lv2_scr, lv4_scr, lv8_scr, slin_scr, slout_scr, utm_scr,
                 xr_scr, xi_scr, stre_scr, stim_scr, stf_scr, *, tt, nb):
    rows = nb * tt
    pitch = tt + 4
    hrows = POOL_HISTORY * nb
    step = pl.program_id(0)

    @pl.when(step == 0)
    def _():
        ext_scr[0:hrows, :] = jnp.zeros((hrows, D_A), F32)
        stre_scr[...] = jnp.zeros_like(stre_scr)
        stim_scr[...] = jnp.zeros_like(stim_scr)
        stf_scr[...] = jnp.zeros_like(stf_scr)

    xb = x_ref[...].reshape(rows, D_MODEL).astype(BF16)
    for c0 in range(0, C1, 256):
        c1 = min(c0 + 256, C1)
        z_scr[:, c0:c1] = (jnp.dot(xb, w1_ref[:, c0:c1], preferred_element_type=F32)
                           + b1_ref[:, c0:c1])

    for b in range(nb):
        r0 = b * tt
        dst = slice(b * pitch, b * pitch + tt)
        slin_scr[0, dst, :] = z_scr[r0:r0 + tt, ZC_OFF:ZC_OFF + LANES]
        slin_scr[1, dst, :] = z_scr[r0:r0 + tt, ZC_OFF + LANES:ZC_OFF + 2 * LANES]
        slin_scr[2, dst, :] = _log_sigmoid(z_scr[r0:r0 + tt, ZF_OFF:ZF_OFF + LANES])
        slin_scr[3, dst, :] = z_scr[r0:r0 + tt, ZA_OFF:ZA_OFF + LANES]
        slin_scr[4, dst, :] = z_scr[r0:r0 + tt, ZA_OFF + LANES:ZA_OFF + 2 * LANES]

    def gather_t(t, carry):
        r = pl.ds(pl.multiple_of(t * nb, nb), nb)
        e = pl.ds(pl.multiple_of(hrows + t * nb, nb), nb)
        utm_scr[r, 0:LANES] = slin_scr[0, pl.ds(t, nb, stride=pitch), :]
        utm_scr[r, LANES:2 * LANES] = slin_scr[1, pl.ds(t, nb, stride=pitch), :]
        ext_scr[e, 0:LANES] = slin_scr[3, pl.ds(t, nb, stride=pitch), :]
        ext_scr[e, LANES:2 * LANES] = slin_scr[4, pl.ds(t, nb, stride=pitch), :]
        return carry

    lax.fori_loop(0, tt, gather_t, 0, unroll=4)

    ext_len = rows + hrows
    lv2_scr[nb:ext_len, :] = ext_scr[nb:ext_len, :] + ext_scr[0:ext_len - nb, :]
    lv4_scr[3 * nb:ext_len, :] = lv2_scr[3 * nb:ext_len, :] + lv2_scr[nb:ext_len - 2 * nb, :]
    lv8_scr[7 * nb:ext_len, :] = lv4_scr[7 * nb:ext_len, :] + lv4_scr[3 * nb:ext_len - 4 * nb, :]
    s16 = lv8_scr[hrows:ext_len, :] + lv8_scr[hrows - 8 * nb:ext_len - 8 * nb, :]
    lane_a = lax.broadcasted_iota(jnp.int32, (rows, D_A), 1)
    time_a = step * tt + lax.broadcasted_iota(jnp.int32, (tt, nb, D_A), 0).reshape(rows, D_A)
    win = jnp.where(lane_a < 64, 2, jnp.where(lane_a < 128, 4, jnp.where(lane_a < 192, 8, 16)))
    cnt = jnp.minimum(time_a + 1, win).astype(F32)
    wsum = jnp.where(lane_a < 64, lv2_scr[hrows:ext_len, :],
                     jnp.where(lane_a < 128, lv4_scr[hrows:ext_len, :],
                               jnp.where(lane_a < 192, lv8_scr[hrows:ext_len, :], s16)))
    pooled = wsum / cnt - ext_scr[hrows:ext_len, :]
    ya = jnp.dot(pooled.astype(BF16), pw_ref[...], preferred_element_type=F32) * ps_ref[...]
    slout_scr[3] = ya[:, 0:LANES]
    slout_scr[4] = ya[:, LANES:2 * LANES]
    ext_scr[0:hrows, :] = ext_scr[rows:ext_len, :]


    u_tm = utm_scr[...]
    ub = u_tm.astype(BF16)
    xr_scr[...] = jnp.dot(ub, bbr_ref[...], preferred_element_type=F32)
    xi_scr[...] = jnp.dot(ub, bbi_ref[...], preferred_element_type=F32)

    ar = are_ref[...]
    ai = aim_ref[...]

    def scan_t(t, carry):
        sr, si, f = carry
        r = pl.ds(pl.multiple_of(t * nb, nb), nb)
        nr = ar * sr - ai * si + xr_scr[r, :]
        ni = ar * si + ai * sr + xi_scr[r, :]
        xr_scr[r, :] = nr
        xi_scr[r, :] = ni
        f = f + slin_scr[2, pl.ds(t, nb, stride=pitch), :]
        slout_scr[2, r, :] = f
        return nr, ni, f

    sr, si, f = lax.fori_loop(0, tt, scan_t, (stre_scr[...], stim_scr[...], stf_scr[...]), unroll=2)
    stre_scr[...] = sr
    stim_scr[...] = si
    stf_scr[...] = f

    y = (jnp.dot(xr_scr[...].astype(BF16), cre_ref[...], preferred_element_type=F32)
         - jnp.dot(xi_scr[...].astype(BF16), cim_ref[...], preferred_element_type=F32))
    y = _gelu_tanh(y + d_ref[...] * u_tm)
    glu = jnp.dot(y.astype(BF16), wglu_ref[...], preferred_element_type=F32) + bglu_ref[...]
    y = y * _sigmoid(glu)
    slout_scr[0] = y[:, 0:LANES]
    slout_scr[1] = y[:, LANES:2 * LANES]

    lane = lax.broadcasted_iota(jnp.int32, (tt, LANES), 1)
    for b in range(nb):
        ya_ref[b, :, 0:LANES] = slout_scr[3, pl.ds(b, tt, stride=nb), :]
        ya_ref[b, :, LANES:2 * LANES] = slout_scr[4, pl.ds(b, tt, stride=nb), :]
        yc_ref[b, :, 0:LANES] = slout_scr[0, pl.ds(b, tt, stride=nb), :]
        yc_ref[b, :, LANES:2 * LANES] = slout_scr[1, pl.ds(b, tt, stride=nb), :]
        fb = slout_scr[2, pl.ds(b, tt, stride=nb), :] * LOG2E
        hi = fb.astype(BF16).astype(F32)
        r1 = fb - hi
        mid = r1.astype(BF16).astype(F32)
        lo = r1 - mid
        fa = jnp.where(lane < 8, hi,
                       jnp.where(lane < 16, pltpu.roll(mid, 8, axis=1),
                                 jnp.where(lane < 24, pltpu.roll(lo, 16, axis=1),
                                           jnp.where(lane == 24, 1.0, 0.0)))).astype(BF16)
        augq = jnp.dot(fa, eq_ref[...], preferred_element_type=F32)
        augk = jnp.dot(fa, ek_ref[...], preferred_element_type=F32)
        r0 = b * tt
        for h in range(N_HEADS):
            c = (h // 2) * LANES
            zq = z_scr[r0:r0 + tt, Q_OFF + c:Q_OFF + c + LANES]
            zk = z_scr[r0:r0 + tt, K_OFF + c:K_OFF + c + LANES]
            zv = z_scr[r0:r0 + tt, V_OFF + c:V_OFF + c + LANES]
            if h % 2:
                zq = pltpu.roll(zq, HEAD_DIM, axis=1)
                zk = pltpu.roll(zk, HEAD_DIM, axis=1)
                zv = pltpu.roll(zv, HEAD_DIM, axis=1)
            q_ref[b, h] = jnp.where(lane < AUG, zq, augq[:, h * LANES:(h + 1) * LANES]).astype(BF16)
            k_ref[b, h] = jnp.where(lane < AUG, zk, augk[:, h * LANES:(h + 1) * LANES]).astype(BF16)
            v_ref[b, h] = jnp.where(lane < AUG, zv, jnp.where(lane == AUG, 1.0, 0.0)).astype(BF16)


def _const_spec(shape):
    nd = len(shape)
    return pl.BlockSpec(shape, lambda *_: (0,) * nd, pipeline_mode=pl.Buffered(1))


def _mix_in(x, p, tt):
    nb, seq, _ = x.shape
    nt = seq // tt
    rows = nb * tt
    pitch = tt + 4
    hrows = POOL_HISTORY * nb
    consts = [p["w1"], p["b1"], p["pw"], p["ps"], p["are"], p["aim"], p["bbr"], p["bbi"],
              p["cre"], p["cim"], p["d"], p["wglu"], p["bglu"], p["eq"], p["ek"]]
    in_specs = [pl.BlockSpec((nb, tt, D_MODEL), lambda i: (0, i, 0))]
    in_specs += [_const_spec(c.shape) for c in consts]
    out_shape = [
        jax.ShapeDtypeStruct((nb, seq, D_A), F32),
        jax.ShapeDtypeStruct((nb, seq, D_C), F32),
        jax.ShapeDtypeStruct((nb, N_HEADS, seq, LANES), BF16),
        jax.ShapeDtypeStruct((nb, N_HEADS, seq, LANES), BF16),
        jax.ShapeDtypeStruct((nb, N_HEADS, seq, LANES), BF16),
    ]
    out_specs = [
        pl.BlockSpec((nb, tt, D_A), lambda i: (0, i, 0)),
        pl.BlockSpec((nb, tt, D_C), lambda i: (0, i, 0)),
        pl.BlockSpec((nb, N_HEADS, tt, LANES), lambda i: (0, 0, i, 0)),
        pl.BlockSpec((nb, N_HEADS, tt, LANES), lambda i: (0, 0, i, 0)),
        pl.BlockSpec((nb, N_HEADS, tt, LANES), lambda i: (0, 0, i, 0)),
    ]
    scratch = [
        pltpu.VMEM((rows, C1), F32),
        pltpu.VMEM((rows + hrows, D_A), F32),
        pltpu.VMEM((rows + hrows, D_A), F32),
        pltpu.VMEM((rows + hrows, D_A), F32),
        pltpu.VMEM((rows + hrows, D_A), F32),
        pltpu.VMEM((5, nb * pitch, LANES), F32),
        pltpu.VMEM((5, rows, LANES), F32),
        pltpu.VMEM((rows, D_C), F32),
        pltpu.VMEM((rows, N_STATE), F32),
        pltpu.VMEM((rows, N_STATE), F32),
        pltpu.VMEM((nb, N_STATE), F32),
        pltpu.VMEM((nb, N_STATE), F32),
        pltpu.VMEM((nb, LANES), F32),
    ]
    return pl.pallas_call(
        functools.partial(_mix_in_body, tt=tt, nb=nb),
        name="mix_in",
        grid=(nt,),
        in_specs=in_specs,
        out_specs=out_specs,
        out_shape=out_shape,
        scratch_shapes=scratch,
        compiler_params=pltpu.CompilerParams(
            dimension_semantics=("arbitrary",), vmem_limit_bytes=VMEM_LIMIT),
    )(x, *consts)


def _attn_body(q_ref, k_ref, v_ref, o_ref, vt_scr, m_scr, acc_scr, st_scr, *, t, hp):
    i = pl.program_id(2)
    dims = (((1,), (1,)), ((), ()))

    for hh in range(hp):
        m_scr[hh] = jnp.full((1, t), NEG_BIG, F32)
        acc_scr[hh] = jnp.zeros((LANES, t), F32)

    key = lax.broadcasted_iota(jnp.int32, (t, t), 0)
    qry = lax.broadcasted_iota(jnp.int32, (t, t), 1)

    def scores(hh, j):
        rows = pl.ds(pl.multiple_of(j * t, t), t)
        return lax.dot_general(k_ref[0, hh, rows, :], q_ref[0, hh], dims,
                               preferred_element_type=F32)

    for hh in range(SCORE_LOOKAHEAD):
        st_scr[hh] = scores(hh, 0)

    def read_scores(hh, masked):
        st = st_scr[hh]
        return jnp.where(key <= qry, st, NEG_BIG) if masked else st

    def block(j, masked):
        for hh in range(hp):
            ahead = hh + SCORE_LOOKAHEAD
            if ahead < hp:
                st_scr[ahead] = scores(ahead, j)
            elif not masked:
                st_scr[ahead - hp] = scores(ahead - hp, j + 1)
            m_old = m_scr[hh]
            m_new = jnp.maximum(m_old, jnp.max(read_scores(hh, masked), axis=0, keepdims=True))
            p = jnp.exp2(read_scores(hh, masked) - m_new).astype(BF16)
            alpha = jnp.exp2(m_old - m_new)
            acc_scr[hh, 0:V_ROWS, :] = alpha * acc_scr[hh, 0:V_ROWS, :] + jnp.dot(
                vt_scr[hh, j], p, preferred_element_type=F32)
            m_scr[hh] = m_new

    def body(j, carry):
        block(j, False)
        return carry

    lax.fori_loop(0, i, body, 0)
    for hh in range(hp):
        vt_scr[hh, i] = v_ref[0, hh].astype(F32).T[0:V_ROWS].astype(BF16)
    block(i, True)
    outs = []
    for hh in range(hp):
        acc = acc_scr[hh].T
        outs.append(acc[:, 0:HEAD_DIM] / acc[:, HEAD_DIM:HEAD_DIM + 1])
    o_ref[0] = jnp.concatenate(outs, axis=-1)


def _attention(q, k, v, t, hp):
    nb, nh, seq, _ = q.shape
    nq = seq // t
    return pl.pallas_call(
        functools.partial(_attn_body, t=t, hp=hp),
        name="fox_attention",
        grid=(nb, nh // hp, nq),
        in_specs=[
            pl.BlockSpec((1, hp, t, LANES), lambda b, h, i: (b, h, i, 0)),
            pl.BlockSpec((1, hp, seq, LANES), lambda b, h, i: (b, h, 0, 0)),
            pl.BlockSpec((1, hp, t, LANES), lambda b, h, i: (b, h, i, 0)),
        ],
        out_specs=pl.BlockSpec((1, t, hp * HEAD_DIM), lambda b, h, i: (b, i, h)),
        out_shape=jax.ShapeDtypeStruct((nb, seq, D_B), F32),
        scratch_shapes=[
            pltpu.VMEM((hp, seq // t, V_ROWS, t), BF16),
            pltpu.VMEM((hp, 1, t), F32),
            pltpu.VMEM((hp, LANES, t), F32),
            pltpu.VMEM((hp, t, t), F32),
        ],
        compiler_params=pltpu.CompilerParams(
            dimension_semantics=("parallel", "parallel", "arbitrary"),
            vmem_limit_bytes=VMEM_LIMIT),
    )(q, k, v)


def _merge_out_body(x_ref, ya_ref, yb_ref, yc_ref, w3_ref, b3_ref, wa_ref, wb_ref, wc_ref, wo_ref,
                    g_ref, beta_ref, o_ref, *, alpha):
    x = x_ref[...]
    xb = x.astype(BF16)

    def proj(c0, c1):
        return jnp.dot(xb, w3_ref[:, c0:c1], preferred_element_type=F32) + b3_ref[:, c0:c1]

    ya = (ya_ref[...] * _silu(proj(0, 256))).astype(BF16)
    yc = (yc_ref[...] * _silu(proj(256, 512))).astype(BF16)
    yb = (yb_ref[...] * _silu(proj(512, 1024))).astype(BF16)
    merged = _sigmoid(proj(1024, 2048)) * jnp.dot(ya, wa_ref[...], preferred_element_type=F32)
    merged += _sigmoid(proj(2048, 3072)) * jnp.dot(yb, wb_ref[...], preferred_element_type=F32)
    merged += _sigmoid(proj(3072, 4096)) * jnp.dot(yc, wc_ref[...], preferred_element_type=F32)
    out = jnp.dot(merged.astype(BF16), wo_ref[...], preferred_element_type=F32)
    y = alpha * x + out
    mu = jnp.mean(y, axis=-1, keepdims=True)
    yc0 = y - mu
    var = jnp.mean(yc0 * yc0, axis=-1, keepdims=True)
    o_ref[...] = yc0 * lax.rsqrt(var + LN_EPS) * g_ref[...] + beta_ref[...]


def _merge_out(x2, ya2, yb2, yc2, p, tm, alpha):
    n = x2.shape[0]
    consts = [p["w3"], p["b3"], p["wa"], p["wb"], p["wc"], p["wo"], p["lng"], p["lnb"]]
    row_spec = lambda w: pl.BlockSpec((tm, w), lambda i: (i, 0))
    return pl.pallas_call(
        functools.partial(_merge_out_body, alpha=alpha),
        name="merge_out",
        grid=(n // tm,),
        in_specs=[row_spec(D_MODEL), row_spec(D_A), row_spec(D_B), row_spec(D_C)]
        + [_const_spec(c.shape) for c in consts],
        out_specs=row_spec(D_MODEL),
        out_shape=jax.ShapeDtypeStruct((n, D_MODEL), F32),
        compiler_params=pltpu.CompilerParams(
            dimension_semantics=("parallel",), vmem_limit_bytes=VMEM_LIMIT),
    )(x2, ya2, yb2, yc2, *consts)


def _aug_selectors():
    eq = np.zeros((LANES, N_HEADS * LANES), np.float32)
    ek = np.zeros((LANES, N_HEADS * LANES), np.float32)
    for h in range(N_HEADS):
        base = h * LANES + AUG
        for part in range(3):
            eq[part * 8 + h, base + part] = 1.0
            ek[part * 8 + h, base + 3 + part] = -1.0
            eq[24, base + 3 + part] = 1.0
            ek[24, base + part] = 1.0
    return jnp.asarray(eq, BF16), jnp.asarray(ek, BF16)


def _block_diag(blocks):
    g, r, c = blocks.shape
    eye = jnp.eye(g, dtype=blocks.dtype)
    return jnp.einsum("grc,gk->grkc", blocks, eye).reshape(g * r, g * c)


def _layer_params(l, nb, w_in, b_in, pool_w, pool_scale, abr, abi, bbr, bbi, c_re, c_im, ssm_d,
                  w_glu, b_glu, w_up_a, w_up_b, w_up_c, w_out, ln_g, ln_b, eq, ek):
    w = w_in[l]
    b = b_in[l]
    offs = np.cumsum([0, D_A, D_A, D_B, D_B, D_B, N_HEADS, D_B, D_C, D_C, 3 * D_MODEL])
    za, ga, zq, zk, zv, zf, gb, zc, gc, zg = [slice(int(offs[i]), int(offs[i + 1])) for i in range(10)]
    scale = LOG2E / math.sqrt(HEAD_DIM)
    fpad = LANES - N_HEADS
    w1 = jnp.concatenate([w[:, za], w[:, zc], jnp.pad(w[:, zf], ((0, 0), (0, fpad))),
                          w[:, zq] * scale, w[:, zk], w[:, zv]], axis=1)
    b1 = jnp.concatenate([b[za], b[zc], jnp.pad(b[zf], (0, fpad)), b[zq] * scale, b[zk], b[zv]])
    w3 = jnp.concatenate([w[:, ga], w[:, gc], w[:, gb], w[:, zg]], axis=1)
    b3 = jnp.concatenate([b[ga], b[gc], b[gb], b[zg]])
    row = lambda a: a.reshape(1, -1).astype(F32)
    return {
        "w1": w1.astype(BF16), "b1": row(b1),
        "pw": _block_diag(pool_w[l]).astype(BF16), "ps": row(pool_scale[l]),
        "are": jnp.broadcast_to(abr[l].reshape(1, N_STATE), (nb, N_STATE)),
        "aim": jnp.broadcast_to(abi[l].reshape(1, N_STATE), (nb, N_STATE)),
        "bbr": _block_diag(bbr[l]).astype(BF16),
        "bbi": _block_diag(bbi[l]).astype(BF16),
        "cre": _block_diag(jnp.swapaxes(c_re[l], 1, 2)).astype(BF16),
        "cim": _block_diag(jnp.swapaxes(c_im[l], 1, 2)).astype(BF16),
        "d": row(ssm_d[l]), "wglu": w_glu[l].astype(BF16), "bglu": row(b_glu[l]),
        "eq": eq, "ek": ek,
        "w3": w3.astype(BF16), "b3": row(b3),
        "wa": w_up_a[l].astype(BF16), "wb": w_up_b[l].astype(BF16), "wc": w_up_c[l].astype(BF16),
        "wo": w_out[l].astype(BF16), "lng": row(ln_g[l]), "lnb": row(ln_b[l]),
    }


def _trunk(x, w_in, b_in, pool_w, pool_scale, ssm_a_re, ssm_a_im, ssm_log_dt, ssm_b_re, ssm_b_im,
           ssm_c_re, ssm_c_im, ssm_d, w_glu, b_glu, w_up_a, w_up_b, w_up_c, w_out, ln_g, ln_b,
           *, tt, tq, tm, hp=8, alpha=None):
    nb, seq, _ = x.shape
    depth = w_in.shape[0]
    if alpha is None:
        alpha = (2.0 * depth) ** 0.25
    abr, abi, bbr, bbi = _ssm_prep(ssm_a_re, ssm_a_im, ssm_log_dt, ssm_b_re, ssm_b_im)
    eq, ek = _aug_selectors()
    h = x
    for l in range(depth):
        p = _layer_params(l, nb, w_in, b_in, pool_w, pool_scale, abr, abi, bbr, bbi, ssm_c_re,
                          ssm_c_im, ssm_d, w_glu, b_glu, w_up_a, w_up_b, w_up_c, w_out, ln_g, ln_b,
                          eq, ek)
        ya, yc, q, k, v = _mix_in(h, p, tt)
        yb = _attention(q, k, v, tq, hp)
        n = nb * seq
        h = _merge_out(h.reshape(n, D_MODEL), ya.reshape(n, D_A), yb.reshape(n, D_B),
                       yc.reshape(n, D_C), p, tm, alpha).reshape(nb, seq, D_MODEL)
    return h


def kernel(x, w_in, b_in, pool_w, pool_scale, ssm_a_re, ssm_a_im, ssm_log_dt, ssm_b_re, ssm_b_im,
           ssm_c_re, ssm_c_im, ssm_d, w_glu, b_glu, w_up_a, w_up_b, w_up_c, w_out, ln_g, ln_b):
    return _trunk(x, w_in, b_in, pool_w, pool_scale, ssm_a_re, ssm_a_im, ssm_log_dt, ssm_b_re,
                  ssm_b_im, ssm_c_re, ssm_c_im, ssm_d, w_glu, b_glu, w_up_a, w_up_b, w_up_c, w_out,
                  ln_g, ln_b, tt=64, tq=512, tm=512)
```

```python
import functools
import math

import jax
import jax.numpy as jnp
import numpy as np
from jax import lax
from jax.experimental import pallas as pl
from jax.experimental.pallas import tpu as pltpu

F32 = jnp.float32
BF16 = jnp.bfloat16

D_MODEL = 1024
N_HEADS = 8
HEAD_DIM = 64
D_A = 256
D_B = N_HEADS * HEAD_DIM
D_C = 256
POOL_WINDOWS = (2, 4, 8, 16)
POOL_GROUP = 64
SSM_GROUPS = 16
SSM_GROUP = 16
SSM_STATE = 64
N_STATE = SSM_GROUPS * SSM_STATE
LN_EPS = 1e-5
LANES = 128
SUBLANES = 8
NEG_BIG = -1e30
LOG2E = math.log2(math.e)
SCORE_LOOKAHEAD = 1
V_ROWS = 80

ZA_OFF, ZC_OFF, ZF_OFF, Q_OFF, K_OFF, V_OFF = 0, 256, 512, 640, 1152, 1664
C1 = 2176
C3 = 4096
AUG = HEAD_DIM
POOL_HISTORY = 16

VMEM_LIMIT = 56 * 1024 * 1024


def _sigmoid(x):
    return 0.5 * (jnp.tanh(0.5 * x) + 1.0)


def _silu(x):
    return x * _sigmoid(x)


def _gelu_tanh(x):
    return 0.5 * x * (1.0 + jnp.tanh(math.sqrt(2.0 / math.pi) * (x + 0.044715 * (x * x * x))))


def _log_sigmoid(x):
    return jnp.minimum(x, 0.0) - jnp.log(1.0 + jnp.exp(-jnp.abs(x)))


def _ssm_prep_body(are_ref, aim_ref, ldt_ref, bre_ref, bim_ref, abr_ref, abi_ref, bbr_ref, bbi_ref):
    are = are_ref[...]
    aim = aim_ref[...]
    dt = jnp.exp(ldt_ref[...])
    mag = jnp.exp(are * dt)
    ang = aim * dt
    abr = mag * jnp.cos(ang)
    abi = mag * jnp.sin(ang)
    den = are * are + aim * aim
    nr = abr - 1.0
    cre = (nr * are + abi * aim) / den
    cim = (abi * are - nr * aim) / den
    bre = bre_ref[...]
    bim = bim_ref[...]
    abr_ref[...] = abr
    abi_ref[...] = abi
    bbr_ref[...] = cre * bre - cim * bim
    bbi_ref[...] = cre * bim + cim * bre


def _ssm_prep(a_re, a_im, log_dt, b_re, b_im):
    L = a_re.shape[0]
    rows = L * SSM_GROUPS
    wide = SSM_GROUP * SSM_STATE
    tile = lambda a: jnp.tile(a.reshape(rows, SSM_STATE), (1, SSM_GROUP))
    ldt = jnp.broadcast_to(log_dt.reshape(rows, 1), (rows, wide))
    bt = lambda b: jnp.swapaxes(b, 2, 3).reshape(rows, wide)
    outs = pl.pallas_call(
        _ssm_prep_body,
        name="ssm_prep",
        out_shape=[jax.ShapeDtypeStruct((rows, wide), F32)] * 4,
    )(tile(a_re), tile(a_im), ldt, bt(b_re), bt(b_im))
    abr, abi, bbr, bbi = outs
    abr = abr[:, :SSM_STATE].reshape(L, SSM_GROUPS, SSM_STATE)
    abi = abi[:, :SSM_STATE].reshape(L, SSM_GROUPS, SSM_STATE)
    bbr = bbr.reshape(L, SSM_GROUPS, SSM_GROUP, SSM_STATE)
    bbi = bbi.reshape(L, SSM_GROUPS, SSM_GROUP, SSM_STATE)
    return abr, abi, bbr, bbi


def _mix_in_body(x_ref, w1_ref, b1_ref, pw_ref, ps_ref, are_ref, aim_ref, bbr_ref, bbi_ref,
                 cre_ref, cim_ref, d_ref, wglu_ref, bglu_ref, sel_ref,
                 ya_ref, yc_ref, q_ref, k_ref, v_ref,
                 xb_scr, z_scr, zc_scr, ext_scr, lv2_scr, lv4_scr, lv8_scr, slin_scr, slout_scr, utm_scr,
                 xr_scr, xi_scr, stre_scr, stim_scr, stf_scr, *, tt, nb):
    rows = nb * tt
    pitch = tt + 4
    hrows = POOL_HISTORY * nb
    step = pl.program_id(0)

    @pl.when(step == 0)
    def _():
        ext_scr[0:hrows, :] = jnp.zeros((hrows, D_A), F32)
        stre_scr[...] = jnp.zeros_like(stre_scr)
        stim_scr[...] = jnp.zeros_like(stim_scr)
        stf_scr[...] = jnp.zeros_like(stf_scr)

    xb_scr[...] = x_ref[...].reshape(rows, D_MODEL).astype(BF16)
    for c0 in range(0, Q_OFF, 256):
        c1 = min(c0 + 256, Q_OFF)
        z_scr[:, c0:c1] = (jnp.dot(xb_scr[...], w1_ref[:, c0:c1], preferred_element_type=F32)
                           + b1_ref[:, c0:c1])

    for b in range(nb):
        r0 = b * tt
        dst = slice(b * pitch, b * pitch + tt)
        slin_scr[0, dst, :] = z_scr[r0:r0 + tt, ZC_OFF:ZC_OFF + LANES]
        slin_scr[1, dst, :] = z_scr[r0:r0 + tt, ZC_OFF + LANES:ZC_OFF + 2 * LANES]
        slin_scr[2, dst, :] = _log_sigmoid(z_scr[r0:r0 + tt, ZF_OFF:ZF_OFF + LANES])
        slin_scr[3, dst, :] = z_scr[r0:r0 + tt, ZA_OFF:ZA_OFF + LANES]
        slin_scr[4, dst, :] = z_scr[r0:r0 + tt, ZA_OFF + LANES:ZA_OFF + 2 * LANES]

    def gather_t(t, carry):
        r = pl.ds(pl.multiple_of(t * nb, nb), nb)
        e = pl.ds(pl.multiple_of(hrows + t * nb, nb), nb)
        utm_scr[r, 0:LANES] = slin_scr[0, pl.ds(t, nb, stride=pitch), :]
        utm_scr[r, LANES:2 * LANES] = slin_scr[1, pl.ds(t, nb, stride=pitch), :]
        ext_scr[e, 0:LANES] = slin_scr[3, pl.ds(t, nb, stride=pitch), :]
        ext_scr[e, LANES:2 * LANES] = slin_scr[4, pl.ds(t, nb, stride=pitch), :]
        return carry

    lax.fori_loop(0, tt, gather_t, 0, unroll=4)

    ext_len = rows + hrows
    lv2_scr[nb:ext_len, :] = ext_scr[nb:ext_len, :] + ext_scr[0:ext_len - nb, :]
    lv4_scr[3 * nb:ext_len, :] = lv2_scr[3 * nb:ext_len, :] + lv2_scr[nb:ext_len - 2 * nb, :]
    lv8_scr[7 * nb:ext_len, :] = lv4_scr[7 * nb:ext_len, :] + lv4_scr[3 * nb:ext_len - 4 * nb, :]
    s16 = lv8_scr[hrows:ext_len, :] + lv8_scr[hrows - 8 * nb:ext_len - 8 * nb, :]
    lane_a = lax.broadcasted_iota(jnp.int32, (rows, D_A), 1)
    time_a = step * tt + lax.broadcasted_iota(jnp.int32, (tt, nb, D_A), 0).reshape(rows, D_A)
    win = jnp.where(lane_a < 64, 2, jnp.where(lane_a < 128, 4, jnp.where(lane_a < 192, 8, 16)))
    cnt = jnp.minimum(time_a + 1, win).astype(F32)
    wsum = jnp.where(lane_a < 64, lv2_scr[hrows:ext_len, :],
                     jnp.where(lane_a < 128, lv4_scr[hrows:ext_len, :],
                               jnp.where(lane_a < 192, lv8_scr[hrows:ext_len, :], s16)))
    pooled = wsum / cnt - ext_scr[hrows:ext_len, :]
    ya = jnp.dot(pooled.astype(BF16), pw_ref[...], preferred_element_type=F32) * ps_ref[...]
    slout_scr[3] = ya[:, 0:LANES]
    slout_scr[4] = ya[:, LANES:2 * LANES]
    ext_scr[0:hrows, :] = ext_scr[rows:ext_len, :]


    u_tm = utm_scr[...]
    ub = u_tm.astype(BF16)
    xr_scr[...] = jnp.dot(ub, bbr_ref[...], preferred_element_type=F32)
    xi_scr[...] = jnp.dot(ub, bbi_ref[...], preferred_element_type=F32)

    ar = are_ref[...]
    ai = aim_ref[...]

    def scan_t(t, carry):
        sr, si, f = carry
        r = pl.ds(pl.multiple_of(t * nb, nb), nb)
        nr = ar * sr - ai * si + xr_scr[r, :]
        ni = ar * si + ai * sr + xi_scr[r, :]
        xr_scr[r, :] = nr
        xi_scr[r, :] = ni
        f = f + slin_scr[2, pl.ds(t, nb, stride=pitch), :]
        slout_scr[2, r, :] = f
        return nr, ni, f

    sr, si, f = lax.fori_loop(0, tt, scan_t, (stre_scr[...], stim_scr[...], stf_scr[...]), unroll=2)
    stre_scr[...] = sr
    stim_scr[...] = si
    stf_scr[...] = f

    y = (jnp.dot(xr_scr[...].astype(BF16), cre_ref[...], preferred_element_type=F32)
         - jnp.dot(xi_scr[...].astype(BF16), cim_ref[...], preferred_element_type=F32))
    y = _gelu_tanh(y + d_ref[...] * u_tm)
    glu = jnp.dot(y.astype(BF16), wglu_ref[...], preferred_element_type=F32) + bglu_ref[...]
    y = y * _sigmoid(glu)
    slout_scr[0] = y[:, 0:LANES]
    slout_scr[1] = y[:, LANES:2 * LANES]

    lane_r = lax.broadcasted_iota(jnp.int32, (rows, LANES), 1)
    f_tm = slout_scr[2] * LOG2E
    hi = f_tm.astype(BF16).astype(F32)
    r1 = f_tm - hi
    mid = r1.astype(BF16).astype(F32)
    lo = r1 - mid
    fa = jnp.where(lane_r < 8, hi,
                   jnp.where(lane_r < 16, pltpu.roll(mid, 8, axis=1),
                             jnp.where(lane_r < 24, pltpu.roll(lo, 16, axis=1), 0.0))).astype(BF16)
    slout_scr[2] = jnp.dot(fa, sel_ref[...], preferred_element_type=F32)

    for b in range(nb):
        ya_ref[b, :, 0:LANES] = slout_scr[3, pl.ds(b, tt, stride=nb), :]
        ya_ref[b, :, LANES:2 * LANES] = slout_scr[4, pl.ds(b, tt, stride=nb), :]
        yc_ref[b, :, 0:LANES] = slout_scr[0, pl.ds(b, tt, stride=nb), :]
        yc_ref[b, :, LANES:2 * LANES] = slout_scr[1, pl.ds(b, tt, stride=nb), :]

    lane = lax.broadcasted_iota(jnp.int32, (tt, LANES), 1)
    in_lo = lane < AUG + 3
    in_hi = (lane >= AUG + 3) & (lane < AUG + 6)
    ones_lo = jnp.where((lane >= AUG) & in_lo, 1.0, 0.0)
    ones_hi = jnp.where(in_hi, 1.0, 0.0)
    ones_v = jnp.where(lane == AUG, 1.0, 0.0)
    out_refs = (q_ref, k_ref, v_ref)
    for chunk in range(6):
        kind, half = divmod(chunk, 2)
        c0 = Q_OFF + chunk * 256
        zc_scr[chunk % 2] = (jnp.dot(xb_scr[...], w1_ref[:, c0:c0 + 256],
                                     preferred_element_type=F32) + b1_ref[:, c0:c0 + 256])
        for b in range(nb):
            if kind < 2:
                fg = slout_scr[2, pl.ds(b, tt, stride=nb), :]
            for hl in range(4):
                h = 4 * half + hl
                c = (hl // 2) * LANES
                z = zc_scr[chunk % 2, b * tt:(b + 1) * tt, c:c + LANES]
                if hl % 2:
                    z = pltpu.roll(z, HEAD_DIM, axis=1)
                if kind == 0:
                    aug = jnp.where(in_lo, pltpu.roll(fg, AUG - 8 * h, axis=1), ones_hi)
                elif kind == 1:
                    aug = jnp.where(in_hi, pltpu.roll(fg, AUG - 8 * h, axis=1), ones_lo)
                else:
                    aug = ones_v
                out_refs[kind][b, h] = jnp.where(lane < AUG, z, aug).astype(BF16)


def _const_spec(shape):
    nd = len(shape)
    return pl.BlockSpec(shape, lambda *_: (0,) * nd, pipeline_mode=pl.Buffered(1))


def _mix_in(x, p, tt):
    nb, seq, _ = x.shape
    nt = seq // tt
    rows = nb * tt
    pitch = tt + 4
    hrows = POOL_HISTORY * nb
    consts = [p["w1"], p["b1"], p["pw"], p["ps"], p["are"], p["aim"], p["bbr"], p["bbi"],
              p["cre"], p["cim"], p["d"], p["wglu"], p["bglu"], p["sel"]]
    in_specs = [pl.BlockSpec((nb, tt, D_MODEL), lambda i: (0, i, 0))]
    in_specs += [_const_spec(c.shape) for c in consts]
    out_shape = [
        jax.ShapeDtypeStruct((nb, seq, D_A), F32),
        jax.ShapeDtypeStruct((nb, seq, D_C), F32),
        jax.ShapeDtypeStruct((nb, N_HEADS, seq, LANES), BF16),
        jax.ShapeDtypeStruct((nb, N_HEADS, seq, LANES), BF16),
        jax.ShapeDtypeStruct((nb, N_HEADS, seq, LANES), BF16),
    ]
    out_specs = [
        pl.BlockSpec((nb, tt, D_A), lambda i: (0, i, 0)),
        pl.BlockSpec((nb, tt, D_C), lambda i: (0, i, 0)),
        pl.BlockSpec((nb, N_HEADS, tt, LANES), lambda i: (0, 0, i, 0)),
        pl.BlockSpec((nb, N_HEADS, tt, LANES), lambda i: (0, 0, i, 0)),
        pl.BlockSpec((nb, N_HEADS, tt, LANES), lambda i: (0, 0, i, 0)),
    ]
    scratch = [
        pltpu.VMEM((rows, D_MODEL), BF16),
        pltpu.VMEM((rows, Q_OFF), F32),
        pltpu.VMEM((2, rows, 256), F32),
        pltpu.VMEM((rows + hrows, D_A), F32),
        pltpu.VMEM((rows + hrows, D_A), F32),
        pltpu.VMEM((rows + hrows, D_A), F32),
        pltpu.VMEM((rows + hrows, D_A), F32),
        pltpu.VMEM((5, nb * pitch, LANES), F32),
        pltpu.VMEM((5, rows, LANES), F32),
        pltpu.VMEM((rows, D_C), F32),
        pltpu.VMEM((rows, N_STATE), F32),
        pltpu.VMEM((rows, N_STATE), F32),
        pltpu.VMEM((nb, N_STATE), F32),
        pltpu.VMEM((nb, N_STATE), F32),
        pltpu.VMEM((nb, LANES), F32),
    ]
    return pl.pallas_call(
        functools.partial(_mix_in_body, tt=tt, nb=nb),
        name="mix_in",
        grid=(nt,),
        in_specs=in_specs,
        out_specs=out_specs,
        out_shape=out_shape,
        scratch_shapes=scratch,
        compiler_params=pltpu.CompilerParams(
            dimension_semantics=("arbitrary",), vmem_limit_bytes=VMEM_LIMIT),
    )(x, *consts)


def _attn_body(q_ref, k_ref, v_ref, o_ref, vt_scr, m_scr, acc_scr, st_scr, *, t, hp):
    i = pl.program_id(2)
    dims = (((1,), (1,)), ((), ()))

    for hh in range(hp):
        m_scr[hh] = jnp.full((1, t), NEG_BIG, F32)
        acc_scr[hh] = jnp.zeros((LANES, t), F32)

    key = lax.broadcasted_iota(jnp.int32, (t, t), 0)
    qry = lax.broadcasted_iota(jnp.int32, (t, t), 1)

    def scores(hh, j):
        rows = pl.ds(pl.multiple_of(j * t, t), t)
        return lax.dot_general(k_ref[0, hh, rows, :], q_ref[0, hh], dims,
                               preferred_element_type=F32)

    for hh in range(SCORE_LOOKAHEAD):
        st_scr[hh] = scores(hh, 0)

    def read_scores(hh, masked):
        st = st_scr[hh]
        return jnp.where(key <= qry, st, NEG_BIG) if masked else st

    def block(j, masked):
        for hh in range(hp):
            ahead = hh + SCORE_LOOKAHEAD
            if ahead < hp:
                st_scr[ahead] = scores(ahead, j)
            elif not masked:
                st_scr[ahead - hp] = scores(ahead - hp, j + 1)
            m_old = m_scr[hh]
            m_new = jnp.maximum(m_old, jnp.max(read_scores(hh, masked), axis=0, keepdims=True))
            p = jnp.exp2(read_scores(hh, masked) - m_new).astype(BF16)
            alpha = jnp.exp2(m_old - m_new)
            acc_scr[hh, 0:V_ROWS, :] = alpha * acc_scr[hh, 0:V_ROWS, :] + jnp.dot(
                vt_scr[hh, j], p, preferred_element_type=F32)
            m_scr[hh] = m_new

    def body(j, carry):
        block(j, False)
        return carry

    lax.fori_loop(0, i, body, 0)
    for hh in range(hp):
        vt_scr[hh, i] = v_ref[0, hh].astype(F32).T[0:V_ROWS].astype(BF16)
    block(i, True)
    outs = []
    for hh in range(hp):
        acc = acc_scr[hh].T
        outs.append(acc[:, 0:HEAD_DIM] / acc[:, HEAD_DIM:HEAD_DIM + 1])
    o_ref[0] = jnp.concatenate(outs, axis=-1)


def _attention(q, k, v, t, hp):
    nb, nh, seq, _ = q.shape
    nq = seq // t
    return pl.pallas_call(
        functools.partial(_attn_body, t=t, hp=hp),
        name="fox_attention",
        grid=(nb, nh // hp, nq),
        in_specs=[
            pl.BlockSpec((1, hp, t, LANES), lambda b, h, i: (b, h, i, 0)),
            pl.BlockSpec((1, hp, seq, LANES), lambda b, h, i: (b, h, 0, 0)),
            pl.BlockSpec((1, hp, t, LANES), lambda b, h, i: (b, h, i, 0)),
        ],
        out_specs=pl.BlockSpec((1, t, hp * HEAD_DIM), lambda b, h, i: (b, i, h)),
        out_shape=jax.ShapeDtypeStruct((nb, seq, D_B), F32),
        scratch_shapes=[
            pltpu.VMEM((hp, seq // t, V_ROWS, t), BF16),
            pltpu.VMEM((hp, 1, t), F32),
            pltpu.VMEM((hp, LANES, t), F32),
            pltpu.VMEM((hp, t, t), F32),
        ],
        compiler_params=pltpu.CompilerParams(
            dimension_semantics=("parallel", "parallel", "arbitrary"),
            vmem_limit_bytes=VMEM_LIMIT),
    )(q, k, v)


def _merge_out_body(x_ref, ya_ref, yb_ref, yc_ref, w3_ref, b3_ref, wa_ref, wb_ref, wc_ref, wo_ref,
                    g_ref, beta_ref, o_ref, *, alpha):
    x = x_ref[...]
    xb = x.astype(BF16)

    def proj(c0, c1):
        return jnp.dot(xb, w3_ref[:, c0:c1], preferred_element_type=F32) + b3_ref[:, c0:c1]

    ya = (ya_ref[...] * _silu(proj(0, 256))).astype(BF16)
    yc = (yc_ref[...] * _silu(proj(256, 512))).astype(BF16)
    yb = (yb_ref[...] * _silu(proj(512, 1024))).astype(BF16)
    merged = _sigmoid(proj(1024, 2048)) * jnp.dot(ya, wa_ref[...], preferred_element_type=F32)
    merged += _sigmoid(proj(2048, 3072)) * jnp.dot(yb, wb_ref[...], preferred_element_type=F32)
    merged += _sigmoid(proj(3072, 4096)) * jnp.dot(yc, wc_ref[...], preferred_element_type=F32)
    out = jnp.dot(merged.astype(BF16), wo_ref[...], preferred_element_type=F32)
    y = alpha * x + out
    mu = jnp.mean(y, axis=-1, keepdims=True)
    yc0 = y - mu
    var = jnp.mean(yc0 * yc0, axis=-1, keepdims=True)
    o_ref[...] = yc0 * lax.rsqrt(var + LN_EPS) * g_ref[...] + beta_ref[...]


def _merge_out(x2, ya2, yb2, yc2, p, tm, alpha):
    n = x2.shape[0]
    consts = [p["w3"], p["b3"], p["wa"], p["wb"], p["wc"], p["wo"], p["lng"], p["lnb"]]
    row_spec = lambda w: pl.BlockSpec((tm, w), lambda i: (i, 0))
    return pl.pallas_call(
        functools.partial(_merge_out_body, alpha=alpha),
        name="merge_out",
        grid=(n // tm,),
        in_specs=[row_spec(D_MODEL), row_spec(D_A), row_spec(D_B), row_spec(D_C)]
        + [_const_spec(c.shape) for c in consts],
        out_specs=row_spec(D_MODEL),
        out_shape=jax.ShapeDtypeStruct((n, D_MODEL), F32),
        compiler_params=pltpu.CompilerParams(
            dimension_semantics=("parallel",), vmem_limit_bytes=VMEM_LIMIT),
    )(x2, ya2, yb2, yc2, *consts)


def _aug_selector():
    sel = np.zeros((LANES, LANES), np.float32)
    for h in range(N_HEADS):
        for part in range(3):
            sel[part * 8 + h, 8 * h + part] = 1.0
            sel[part * 8 + h, 8 * h + 3 + part] = -1.0
    return jnp.asarray(sel, BF16)


def _block_diag(blocks):
    g, r, c = blocks.shape
    eye = jnp.eye(g, dtype=blocks.dtype)
    return jnp.einsum("grc,gk->grkc", blocks, eye).reshape(g * r, g * c)


def _layer_params(l, nb, w_in, b_in, pool_w, pool_scale, abr, abi, bbr, bbi, c_re, c_im, ssm_d,
                  w_glu, b_glu, w_up_a, w_up_b, w_up_c, w_out, ln_g, ln_b, sel):
    w = w_in[l]
    b = b_in[l]
    offs = np.cumsum([0, D_A, D_A, D_B, D_B, D_B, N_HEADS, D_B, D_C, D_C, 3 * D_MODEL])
    za, ga, zq, zk, zv, zf, gb, zc, gc, zg = [slice(int(offs[i]), int(offs[i + 1])) for i in range(10)]
    scale = LOG2E / math.sqrt(HEAD_DIM)
    fpad = LANES - N_HEADS
    w1 = jnp.concatenate([w[:, za], w[:, zc], jnp.pad(w[:, zf], ((0, 0), (0, fpad))),
                          w[:, zq] * scale, w[:, zk], w[:, zv]], axis=1)
    b1 = jnp.concatenate([b[za], b[zc], jnp.pad(b[zf], (0, fpad)), b[zq] * scale, b[zk], b[zv]])
    w3 = jnp.concatenate([w[:, ga], w[:, gc], w[:, gb], w[:, zg]], axis=1)
    b3 = jnp.concatenate([b[ga], b[gc], b[gb], b[zg]])
    row = lambda a: a.reshape(1, -1).astype(F32)
    return {
        "w1": w1.astype(BF16), "b1": row(b1),
        "pw": _block_diag(pool_w[l]).astype(BF16), "ps": row(pool_scale[l]),
        "are": jnp.broadcast_to(abr[l].reshape(1, N_STATE), (nb, N_STATE)),
        "aim": jnp.broadcast_to(abi[l].reshape(1, N_STATE), (nb, N_STATE)),
        "bbr": _block_diag(bbr[l]).astype(BF16),
        "bbi": _block_diag(bbi[l]).astype(BF16),
        "cre": _block_diag(jnp.swapaxes(c_re[l], 1, 2)).astype(BF16),
        "cim": _block_diag(jnp.swapaxes(c_im[l], 1, 2)).astype(BF16),
        "d": row(ssm_d[l]), "wglu": w_glu[l].astype(BF16), "bglu": row(b_glu[l]),
        "sel": sel,
        "w3": w3.astype(BF16), "b3": row(b3),
        "wa": w_up_a[l].astype(BF16), "wb": w_up_b[l].astype(BF16), "wc": w_up_c[l].astype(BF16),
        "wo": w_out[l].astype(BF16), "lng": row(ln_g[l]), "lnb": row(ln_b[l]),
    }


def _trunk(x, w_in, b_in, pool_w, pool_scale, ssm_a_re, ssm_a_im, ssm_log_dt, ssm_b_re, ssm_b_im,
           ssm_c_re, ssm_c_im, ssm_d, w_glu, b_glu, w_up_a, w_up_b, w_up_c, w_out, ln_g, ln_b,
           *, tt, tq, tm, hp=8, alpha=None):
    nb, seq, _ = x.shape
    depth = w_in.shape[0]
    if alpha is None:
        alpha = (2.0 * depth) ** 0.25
    abr, abi, bbr, bbi = _ssm_prep(ssm_a_re, ssm_a_im, ssm_log_dt, ssm_b_re, ssm_b_im)
    sel = _aug_selector()
    h = x
    for l in range(depth):
        p = _layer_params(l, nb, w_in, b_in, pool_w, pool_scale, abr, abi, bbr, bbi, ssm_c_re,
                          ssm_c_im, ssm_d, w_glu, b_glu, w_up_a, w_up_b, w_up_c, w_out, ln_g, ln_b,
                          sel)
        ya, yc, q, k, v = _mix_in(h, p, tt)
        yb = _attention(q, k, v, tq, hp)
        n = nb * seq
        h = _merge_out(h.reshape(n, D_MODEL), ya.reshape(n, D_A), yb.reshape(n, D_B),
                       yc.reshape(n, D_C), p, tm, alpha).reshape(nb, seq, D_MODEL)
    return h


def kernel(x, w_in, b_in, pool_w, pool_scale, ssm_a_re, ssm_a_im, ssm_log_dt, ssm_b_re, ssm_b_im,
           ssm_c_re, ssm_c_im, ssm_d, w_glu, b_glu, w_up_a, w_up_b, w_up_c, w_out, ln_g, ln_b):
    return _trunk(x, w_in, b_in, pool_w, pool_scale, ssm_a_re, ssm_a_im, ssm_log_dt, ssm_b_re,
                  ssm_b_im, ssm_c_re, ssm_c_im, ssm_d, w_glu, b_glu, w_up_a, w_up_b, w_up_c, w_out,
                  ln_g, ln_b, tt=64, tq=512, tm=512)
```

```python
import functools
import math

import jax
import jax.numpy as jnp
import numpy as np
from jax import lax
from jax.experimental import pallas as pl
from jax.experimental.pallas import tpu as pltpu

F32 = jnp.float32
BF16 = jnp.bfloat16

D_MODEL = 1024
N_HEADS = 8
HEAD_DIM = 64
D_A = 256
D_B = N_HEADS * HEAD_DIM
D_C = 256
POOL_WINDOWS = (2, 4, 8, 16)
POOL_GROUP = 64
SSM_GROUPS = 16
SSM_GROUP = 16
SSM_STATE = 64
N_STATE = SSM_GROUPS * SSM_STATE
LN_EPS = 1e-5
LANES = 128
SUBLANES = 8
NEG_BIG = -1e30
LOG2E = math.log2(math.e)
SCORE_LOOKAHEAD = 1
V_ROWS = 80

ZA_OFF, ZC_OFF, ZF_OFF, Q_OFF, K_OFF, V_OFF = 0, 256, 512, 640, 1152, 1664
C1 = 2176
C3 = 4096
AUG = HEAD_DIM
POOL_HISTORY = 16

VMEM_LIMIT = 56 * 1024 * 1024


def _sigmoid(x):
    return 0.5 * (jnp.tanh(0.5 * x) + 1.0)


def _silu(x):
    return x * _sigmoid(x)


def _gelu_tanh(x):
    return 0.5 * x * (1.0 + jnp.tanh(math.sqrt(2.0 / math.pi) * (x + 0.044715 * (x * x * x))))


def _log_sigmoid(x):
    return jnp.minimum(x, 0.0) - jnp.log(1.0 + jnp.exp(-jnp.abs(x)))


def _ssm_prep_body(are_ref, aim_ref, ldt_ref, bre_ref, bim_ref, abr_ref, abi_ref, bbr_ref, bbi_ref):
    are = are_ref[...]
    aim = aim_ref[...]
    dt = jnp.exp(ldt_ref[...])
    mag = jnp.exp(are * dt)
    ang = aim * dt
    abr = mag * jnp.cos(ang)
    abi = mag * jnp.sin(ang)
    den = are * are + aim * aim
    nr = abr - 1.0
    cre = (nr * are + abi * aim) / den
    cim = (abi * are - nr * aim) / den
    bre = bre_ref[...]
    bim = bim_ref[...]
    abr_ref[...] = abr
    abi_ref[...] = abi
    bbr_ref[...] = cre * bre - cim * bim
    bbi_ref[...] = cre * bim + cim * bre


def _ssm_prep(a_re, a_im, log_dt, b_re, b_im):
    L = a_re.shape[0]
    rows = L * SSM_GROUPS
    wide = SSM_GROUP * SSM_STATE
    tile = lambda a: jnp.tile(a.reshape(rows, SSM_STATE), (1, SSM_GROUP))
    ldt = jnp.broadcast_to(log_dt.reshape(rows, 1), (rows, wide))
    bt = lambda b: jnp.swapaxes(b, 2, 3).reshape(rows, wide)
    outs = pl.pallas_call(
        _ssm_prep_body,
        name="ssm_prep",
        out_shape=[jax.ShapeDtypeStruct((rows, wide), F32)] * 4,
    )(tile(a_re), tile(a_im), ldt, bt(b_re), bt(b_im))
    abr, abi, bbr, bbi = outs
    abr = abr[:, :SSM_STATE].reshape(L, SSM_GROUPS, SSM_STATE)
    abi = abi[:, :SSM_STATE].reshape(L, SSM_GROUPS, SSM_STATE)
    bbr = bbr.reshape(L, SSM_GROUPS, SSM_GROUP, SSM_STATE)
    bbi = bbi.reshape(L, SSM_GROUPS, SSM_GROUP, SSM_STATE)
    return abr, abi, bbr, bbi


def _mix_in_body(x_ref, w1_ref, b1_ref, pw_ref, ps_ref, are_ref, aim_ref, bbr_ref, bbi_ref,
                 cre_ref, cim_ref, d_ref, wglu_ref, bglu_ref, sel_ref,
                 ya_ref, yc_ref, q_ref, k_ref, v_ref,
                 xb_scr, z_scr, zc_scr, ext_scr, lv2_scr, lv4_scr, lv8_scr, slin_scr, slout_scr, utm_scr,
                 xr_scr, xi_scr, stre_scr, stim_scr, stf_scr, *, tt, nb):
    rows = nb * tt
    pitch = tt + 4
    hrows = POOL_HISTORY * nb
    step = pl.program_id(0)

    @pl.when(step == 0)
    def _():
        ext_scr[0:hrows, :] = jnp.zeros((hrows, D_A), F32)
        stre_scr[...] = jnp.zeros_like(stre_scr)
        stim_scr[...] = jnp.zeros_like(stim_scr)
        stf_scr[...] = jnp.zeros_like(stf_scr)

    xb_scr[...] = x_ref[...].reshape(rows, D_MODEL).astype(BF16)
    for c0 in range(0, Q_OFF, 256):
        c1 = min(c0 + 256, Q_OFF)
        z_scr[:, c0:c1] = (jnp.dot(xb_scr[...], w1_ref[:, c0:c1], preferred_element_type=F32)
                           + b1_ref[:, c0:c1])

    for b in range(nb):
        r0 = b * tt
        dst = slice(b * pitch, b * pitch + tt)
        slin_scr[0, dst, :] = z_scr[r0:r0 + tt, ZC_OFF:ZC_OFF + LANES]
        slin_scr[1, dst, :] = z_scr[r0:r0 + tt, ZC_OFF + LANES:ZC_OFF + 2 * LANES]
        slin_scr[2, dst, :] = _log_sigmoid(z_scr[r0:r0 + tt, ZF_OFF:ZF_OFF + LANES])
        slin_scr[3, dst, :] = z_scr[r0:r0 + tt, ZA_OFF:ZA_OFF + LANES]
        slin_scr[4, dst, :] = z_scr[r0:r0 + tt, ZA_OFF + LANES:ZA_OFF + 2 * LANES]

    def gather_t(t, carry):
        r = pl.ds(pl.multiple_of(t * nb, nb), nb)
        e = pl.ds(pl.multiple_of(hrows + t * nb, nb), nb)
        utm_scr[r, 0:LANES] = slin_scr[0, pl.ds(t, nb, stride=pitch), :]
        utm_scr[r, LANES:2 * LANES] = slin_scr[1, pl.ds(t, nb, stride=pitch), :]
        ext_scr[e, 0:LANES] = slin_scr[3, pl.ds(t, nb, stride=pitch), :]
        ext_scr[e, LANES:2 * LANES] = slin_scr[4, pl.ds(t, nb, stride=pitch), :]
        return carry

    lax.fori_loop(0, tt, gather_t, 0, unroll=4)

    ext_len = rows + hrows
    lv2_scr[nb:ext_len, :] = ext_scr[nb:ext_len, :] + ext_scr[0:ext_len - nb, :]
    lv4_scr[3 * nb:ext_len, :] = lv2_scr[3 * nb:ext_len, :] + lv2_scr[nb:ext_len - 2 * nb, :]
    lv8_scr[7 * nb:ext_len, :] = lv4_scr[7 * nb:ext_len, :] + lv4_scr[3 * nb:ext_len - 4 * nb, :]
    s16 = lv8_scr[hrows:ext_len, :] + lv8_scr[hrows - 8 * nb:ext_len - 8 * nb, :]
    lane_a = lax.broadcasted_iota(jnp.int32, (rows, D_A), 1)
    time_a = step * tt + lax.broadcasted_iota(jnp.int32, (tt, nb, D_A), 0).reshape(rows, D_A)
    win = jnp.where(lane_a < 64, 2, jnp.where(lane_a < 128, 4, jnp.where(lane_a < 192, 8, 16)))
    cnt = jnp.minimum(time_a + 1, win).astype(F32)
    wsum = jnp.where(lane_a < 64, lv2_scr[hrows:ext_len, :],
                     jnp.where(lane_a < 128, lv4_scr[hrows:ext_len, :],
                               jnp.where(lane_a < 192, lv8_scr[hrows:ext_len, :], s16)))
    pooled = wsum / cnt - ext_scr[hrows:ext_len, :]
    ya = jnp.dot(pooled.astype(BF16), pw_ref[...], preferred_element_type=F32) * ps_ref[...]
    slout_scr[3] = ya[:, 0:LANES]
    slout_scr[4] = ya[:, LANES:2 * LANES]
    ext_scr[0:hrows, :] = ext_scr[rows:ext_len, :]


    u_tm = utm_scr[...]
    ub = u_tm.astype(BF16)
    xr_scr[...] = jnp.dot(ub, bbr_ref[...], preferred_element_type=F32)
    xi_scr[...] = jnp.dot(ub, bbi_ref[...], preferred_element_type=F32)

    ar = are_ref[...]
    ai = aim_ref[...]

    def scan_t(t, carry):
        sr, si, f = carry
        r = pl.ds(pl.multiple_of(t * nb, nb), nb)
        nr = ar * sr - ai * si + xr_scr[r, :]
        ni = ar * si + ai * sr + xi_scr[r, :]
        xr_scr[r, :] = nr
        xi_scr[r, :] = ni
        f = f + slin_scr[2, pl.ds(t, nb, stride=pitch), :]
        slout_scr[2, r, :] = f
        return nr, ni, f

    sr, si, f = lax.fori_loop(0, tt, scan_t, (stre_scr[...], stim_scr[...], stf_scr[...]), unroll=2)
    stre_scr[...] = sr
    stim_scr[...] = si
    stf_scr[...] = f

    y = (jnp.dot(xr_scr[...].astype(BF16), cre_ref[...], preferred_element_type=F32)
         - jnp.dot(xi_scr[...].astype(BF16), cim_ref[...], preferred_element_type=F32))
    y = _gelu_tanh(y + d_ref[...] * u_tm)
    glu = jnp.dot(y.astype(BF16), wglu_ref[...], preferred_element_type=F32) + bglu_ref[...]
    y = y * _sigmoid(glu)
    slout_scr[0] = y[:, 0:LANES]
    slout_scr[1] = y[:, LANES:2 * LANES]

    lane_r = lax.broadcasted_iota(jnp.int32, (rows, LANES), 1)
    f_tm = slout_scr[2] * LOG2E
    hi = f_tm.astype(BF16).astype(F32)
    r1 = f_tm - hi
    mid = r1.astype(BF16).astype(F32)
    lo = r1 - mid
    fa = jnp.where(lane_r < 8, hi,
                   jnp.where(lane_r < 16, pltpu.roll(mid, 8, axis=1),
                             jnp.where(lane_r < 24, pltpu.roll(lo, 16, axis=1), 0.0))).astype(BF16)
    slout_scr[2] = jnp.dot(fa, sel_ref[...], preferred_element_type=F32)

    for b in range(nb):
        ya_ref[b, :, 0:LANES] = slout_scr[3, pl.ds(b, tt, stride=nb), :]
        ya_ref[b, :, LANES:2 * LANES] = slout_scr[4, pl.ds(b, tt, stride=nb), :]
        yc_ref[b, :, 0:LANES] = slout_scr[0, pl.ds(b, tt, stride=nb), :]
        yc_ref[b, :, LANES:2 * LANES] = slout_scr[1, pl.ds(b, tt, stride=nb), :]

    lane = lax.broadcasted_iota(jnp.int32, (tt, LANES), 1)
    in_lo = lane < AUG + 3
    in_hi = (lane >= AUG + 3) & (lane < AUG + 6)
    ones_lo = jnp.where((lane >= AUG) & in_lo, 1.0, 0.0)
    ones_hi = jnp.where(in_hi, 1.0, 0.0)
    out_refs = (q_ref, k_ref)
    for chunk in range(6):
        kind, half = divmod(chunk, 2)
        c0 = Q_OFF + chunk * 256
        zc_scr[chunk % 2] = (jnp.dot(xb_scr[...], w1_ref[:, c0:c0 + 256],
                                     preferred_element_type=F32) + b1_ref[:, c0:c0 + 256])
        for b in range(nb):
            if kind == 2:
                for pr in range(2):
                    v_ref[b, 2 * half + pr] = zc_scr[
                        chunk % 2, b * tt:(b + 1) * tt, pr * LANES:(pr + 1) * LANES].astype(BF16)
                continue
            fg = slout_scr[2, pl.ds(b, tt, stride=nb), :]
            for hl in range(4):
                h = 4 * half + hl
                c = (hl // 2) * LANES
                z = zc_scr[chunk % 2, b * tt:(b + 1) * tt, c:c + LANES]
                if hl % 2:
                    z = pltpu.roll(z, HEAD_DIM, axis=1)
                if kind == 0:
                    aug = jnp.where(in_lo, pltpu.roll(fg, AUG - 8 * h, axis=1), ones_hi)
                else:
                    aug = jnp.where(in_hi, pltpu.roll(fg, AUG - 8 * h, axis=1), ones_lo)
                out_refs[kind][b, h] = jnp.where(lane < AUG, z, aug).astype(BF16)


def _const_spec(shape):
    nd = len(shape)
    return pl.BlockSpec(shape, lambda *_: (0,) * nd, pipeline_mode=pl.Buffered(1))


def _layer_spec(stacked, layer):
    nd = stacked.ndim - 1
    return pl.BlockSpec((None,) + stacked.shape[1:], lambda *_: (layer,) + (0,) * nd,
                        pipeline_mode=pl.Buffered(1))


def _mix_in(x, p, sel, layer, tt):
    nb, seq, _ = x.shape
    nt = seq // tt
    rows = nb * tt
    pitch = tt + 4
    hrows = POOL_HISTORY * nb
    consts = [p[name] for name in ("w1", "b1", "pw", "ps", "are", "aim", "bbr", "bbi", "cre", "cim",
                                   "d", "wglu", "bglu")]
    in_specs = [pl.BlockSpec((nb, tt, D_MODEL), lambda i: (0, i, 0))]
    in_specs += [_layer_spec(c, layer) for c in consts] + [_const_spec(sel.shape)]
    out_shape = [
        jax.ShapeDtypeStruct((nb, seq, D_A), F32),
        jax.ShapeDtypeStruct((nb, seq, D_C), F32),
        jax.ShapeDtypeStruct((nb, N_HEADS, seq, LANES), BF16),
        jax.ShapeDtypeStruct((nb, N_HEADS, seq, LANES), BF16),
        jax.ShapeDtypeStruct((nb, N_HEADS // 2, seq, LANES), BF16),
    ]
    out_specs = [
        pl.BlockSpec((nb, tt, D_A), lambda i: (0, i, 0)),
        pl.BlockSpec((nb, tt, D_C), lambda i: (0, i, 0)),
        pl.BlockSpec((nb, N_HEADS, tt, LANES), lambda i: (0, 0, i, 0)),
        pl.BlockSpec((nb, N_HEADS, tt, LANES), lambda i: (0, 0, i, 0)),
        pl.BlockSpec((nb, N_HEADS // 2, tt, LANES), lambda i: (0, 0, i, 0)),
    ]
    scratch = [
        pltpu.VMEM((rows, D_MODEL), BF16),
        pltpu.VMEM((rows, Q_OFF), F32),
        pltpu.VMEM((2, rows, 256), F32),
        pltpu.VMEM((rows + hrows, D_A), F32),
        pltpu.VMEM((rows + hrows, D_A), F32),
        pltpu.VMEM((rows + hrows, D_A), F32),
        pltpu.VMEM((rows + hrows, D_A), F32),
        pltpu.VMEM((5, nb * pitch, LANES), F32),
        pltpu.VMEM((5, rows, LANES), F32),
        pltpu.VMEM((rows, D_C), F32),
        pltpu.VMEM((rows, N_STATE), F32),
        pltpu.VMEM((rows, N_STATE), F32),
        pltpu.VMEM((nb, N_STATE), F32),
        pltpu.VMEM((nb, N_STATE), F32),
        pltpu.VMEM((nb, LANES), F32),
    ]
    return pl.pallas_call(
        functools.partial(_mix_in_body, tt=tt, nb=nb),
        name="mix_in",
        grid=(nt,),
        in_specs=in_specs,
        out_specs=out_specs,
        out_shape=out_shape,
        scratch_shapes=scratch,
        compiler_params=pltpu.CompilerParams(
            dimension_semantics=("arbitrary",), vmem_limit_bytes=VMEM_LIMIT),
    )(x, *consts, sel)


def _attn_body(q_ref, k_ref, v_ref, o_ref, vt_scr, m_scr, acc_scr, st_scr, *, t, hp):
    i = pl.program_id(2)
    dims = (((1,), (1,)), ((), ()))

    for hh in range(hp):
        m_scr[hh] = jnp.full((1, t), NEG_BIG, F32)
        acc_scr[hh] = jnp.zeros((LANES, t), F32)

    key = lax.broadcasted_iota(jnp.int32, (t, t), 0)
    qry = lax.broadcasted_iota(jnp.int32, (t, t), 1)

    def scores(hh, j):
        rows = pl.ds(pl.multiple_of(j * t, t), t)
        return lax.dot_general(k_ref[0, hh, rows, :], q_ref[0, hh], dims,
                               preferred_element_type=F32)

    for hh in range(SCORE_LOOKAHEAD):
        st_scr[hh] = scores(hh, 0)

    def read_scores(hh, masked):
        st = st_scr[hh]
        return jnp.where(key <= qry, st, NEG_BIG) if masked else st

    def block(j, masked):
        for hh in range(hp):
            ahead = hh + SCORE_LOOKAHEAD
            if ahead < hp:
                st_scr[ahead] = scores(ahead, j)
            elif not masked:
                st_scr[ahead - hp] = scores(ahead - hp, j + 1)
            m_old = m_scr[hh]
            m_new = jnp.maximum(m_old, jnp.max(read_scores(hh, masked), axis=0, keepdims=True))
            p = jnp.exp2(read_scores(hh, masked) - m_new).astype(BF16)
            alpha = jnp.exp2(m_old - m_new)
            acc_scr[hh, 0:V_ROWS, :] = alpha * acc_scr[hh, 0:V_ROWS, :] + jnp.dot(
                vt_scr[hh, j], p, preferred_element_type=F32)
            m_scr[hh] = m_new

    def body(j, carry):
        block(j, False)
        return carry

    lax.fori_loop(0, i, body, 0)
    pad_rows = lax.broadcasted_iota(jnp.int32, (V_ROWS - HEAD_DIM, t), 0)
    ones_row = jnp.where(pad_rows == 0, 1.0, 0.0).astype(BF16)
    for pr in range(hp // 2):
        vt = v_ref[0, pr].astype(F32).T.astype(BF16)
        for hh in (2 * pr, 2 * pr + 1):
            vt_scr[hh, i, 0:HEAD_DIM, :] = vt[(hh % 2) * HEAD_DIM:(hh % 2 + 1) * HEAD_DIM]
            vt_scr[hh, i, HEAD_DIM:V_ROWS, :] = ones_row
    block(i, True)
    for pr in range(hp // 2):
        halves = []
        for hh in (2 * pr, 2 * pr + 1):
            inv = 1.0 / acc_scr[hh, HEAD_DIM:HEAD_DIM + 1, :]
            halves.append(acc_scr[hh, 0:HEAD_DIM, :] * inv)
        o_ref[0, :, pr * LANES:(pr + 1) * LANES] = jnp.concatenate(halves, axis=0).T


def _attention(q, k, v, t, hp):
    nb, nh, seq, _ = q.shape
    nq = seq // t
    return pl.pallas_call(
        functools.partial(_attn_body, t=t, hp=hp),
        name="fox_attention",
        grid=(nb, nh // hp, nq),
        in_specs=[
            pl.BlockSpec((1, hp, t, LANES), lambda b, h, i: (b, h, i, 0)),
            pl.BlockSpec((1, hp, seq, LANES), lambda b, h, i: (b, h, 0, 0)),
            pl.BlockSpec((1, hp // 2, t, LANES), lambda b, h, i: (b, h, i, 0)),
        ],
        out_specs=pl.BlockSpec((1, t, hp * HEAD_DIM), lambda b, h, i: (b, i, h)),
        out_shape=jax.ShapeDtypeStruct((nb, seq, D_B), F32),
        scratch_shapes=[
            pltpu.VMEM((hp, seq // t, V_ROWS, t), BF16),
            pltpu.VMEM((hp, 1, t), F32),
            pltpu.VMEM((hp, LANES, t), F32),
            pltpu.VMEM((hp, t, t), F32),
        ],
        compiler_params=pltpu.CompilerParams(
            dimension_semantics=("parallel", "parallel", "arbitrary"),
            vmem_limit_bytes=VMEM_LIMIT),
    )(q, k, v)


def _merge_out_body(x_ref, ya_ref, yb_ref, yc_ref, w3_ref, b3_ref, wa_ref, wb_ref, wc_ref, wo_ref,
                    g_ref, beta_ref, o_ref, *, alpha):
    x = x_ref[...]
    xb = x.astype(BF16)

    def proj(c0, c1):
        return jnp.dot(xb, w3_ref[:, c0:c1], preferred_element_type=F32) + b3_ref[:, c0:c1]

    ya = (ya_ref[...] * _silu(proj(0, 256))).astype(BF16)
    yc = (yc_ref[...] * _silu(proj(256, 512))).astype(BF16)
    yb = (yb_ref[...] * _silu(proj(512, 1024))).astype(BF16)
    merged = _sigmoid(proj(1024, 2048)) * jnp.dot(ya, wa_ref[...], preferred_element_type=F32)
    merged += _sigmoid(proj(2048, 3072)) * jnp.dot(yb, wb_ref[...], preferred_element_type=F32)
    merged += _sigmoid(proj(3072, 4096)) * jnp.dot(yc, wc_ref[...], preferred_element_type=F32)
    out = jnp.dot(merged.astype(BF16), wo_ref[...], preferred_element_type=F32)
    y = alpha * x + out
    mu = jnp.mean(y, axis=-1, keepdims=True)
    yc0 = y - mu
    var = jnp.mean(yc0 * yc0, axis=-1, keepdims=True)
    o_ref[...] = yc0 * lax.rsqrt(var + LN_EPS) * g_ref[...] + beta_ref[...]


def _merge_out(x2, ya2, yb2, yc2, p, layer, tm, alpha):
    n = x2.shape[0]
    consts = [p[name] for name in ("w3", "b3", "wa", "wb", "wc", "wo", "lng", "lnb")]
    row_spec = lambda w: pl.BlockSpec((tm, w), lambda i: (i, 0))
    return pl.pallas_call(
        functools.partial(_merge_out_body, alpha=alpha),
        name="merge_out",
        grid=(n // tm,),
        in_specs=[row_spec(D_MODEL), row_spec(D_A), row_spec(D_B), row_spec(D_C)]
        + [_layer_spec(c, layer) for c in consts],
        out_specs=row_spec(D_MODEL),
        out_shape=jax.ShapeDtypeStruct((n, D_MODEL), F32),
        compiler_params=pltpu.CompilerParams(
            dimension_semantics=("parallel",), vmem_limit_bytes=VMEM_LIMIT),
    )(x2, ya2, yb2, yc2, *consts)


def _aug_selector():
    sel = np.zeros((LANES, LANES), np.float32)
    for h in range(N_HEADS):
        for part in range(3):
            sel[part * 8 + h, 8 * h + part] = 1.0
            sel[part * 8 + h, 8 * h + 3 + part] = -1.0
    return jnp.asarray(sel, BF16)


def _block_diag(blocks):
    nl, g, r, c = blocks.shape
    eye = jnp.eye(g, dtype=blocks.dtype)
    return jnp.einsum("lgrc,gk->lgrkc", blocks, eye).reshape(nl, g * r, g * c)


def _stacked_params(nb, w_in, b_in, pool_w, pool_scale, abr, abi, bbr, bbi, c_re, c_im, ssm_d,
                    w_glu, b_glu, w_up_a, w_up_b, w_up_c, w_out, ln_g, ln_b):
    nl = w_in.shape[0]
    offs = np.cumsum([0, D_A, D_A, D_B, D_B, D_B, N_HEADS, D_B, D_C, D_C, 3 * D_MODEL])
    za, ga, zq, zk, zv, zf, gb, zc, gc, zg = [slice(int(offs[i]), int(offs[i + 1])) for i in range(10)]
    scale = LOG2E / math.sqrt(HEAD_DIM)
    fpad = LANES - N_HEADS
    w, b = w_in, b_in
    w1 = jnp.concatenate([w[..., za], w[..., zc], jnp.pad(w[..., zf], ((0, 0), (0, 0), (0, fpad))),
                          w[..., zq] * scale, w[..., zk], w[..., zv]], axis=-1)
    b1 = jnp.concatenate([b[:, za], b[:, zc], jnp.pad(b[:, zf], ((0, 0), (0, fpad))),
                          b[:, zq] * scale, b[:, zk], b[:, zv]], axis=-1)
    w3 = jnp.concatenate([w[..., ga], w[..., gc], w[..., gb], w[..., zg]], axis=-1)
    b3 = jnp.concatenate([b[:, ga], b[:, gc], b[:, gb], b[:, zg]], axis=-1)
    row = lambda a: a.reshape(nl, 1, -1).astype(F32)
    tile_nb = lambda a: jnp.broadcast_to(a.reshape(nl, 1, N_STATE), (nl, nb, N_STATE))
    return {
        "w1": w1.astype(BF16), "b1": row(b1),
        "pw": _block_diag(pool_w).astype(BF16), "ps": row(pool_scale),
        "are": tile_nb(abr), "aim": tile_nb(abi),
        "bbr": _block_diag(bbr).astype(BF16),
        "bbi": _block_diag(bbi).astype(BF16),
        "cre": _block_diag(jnp.swapaxes(c_re, 2, 3)).astype(BF16),
        "cim": _block_diag(jnp.swapaxes(c_im, 2, 3)).astype(BF16),
        "d": row(ssm_d), "wglu": w_glu.astype(BF16), "bglu": row(b_glu),
        "w3": w3.astype(BF16), "b3": row(b3),
        "wa": w_up_a.astype(BF16), "wb": w_up_b.astype(BF16), "wc": w_up_c.astype(BF16),
        "wo": w_out.astype(BF16), "lng": row(ln_g), "lnb": row(ln_b),
    }


def _trunk(x, w_in, b_in, pool_w, pool_scale, ssm_a_re, ssm_a_im, ssm_log_dt, ssm_b_re, ssm_b_im,
           ssm_c_re, ssm_c_im, ssm_d, w_glu, b_glu, w_up_a, w_up_b, w_up_c, w_out, ln_g, ln_b,
           *, tt, tq, tm, hp=8, alpha=None):
    nb, seq, _ = x.shape
    depth = w_in.shape[0]
    if alpha is None:
        alpha = (2.0 * depth) ** 0.25
    abr, abi, bbr, bbi = _ssm_prep(ssm_a_re, ssm_a_im, ssm_log_dt, ssm_b_re, ssm_b_im)
    p = _stacked_params(nb, w_in, b_in, pool_w, pool_scale, abr, abi, bbr, bbi, ssm_c_re, ssm_c_im,
                        ssm_d, w_glu, b_glu, w_up_a, w_up_b, w_up_c, w_out, ln_g, ln_b)
    sel = _aug_selector()
    h = x
    for l in range(depth):
        ya, yc, q, k, v = _mix_in(h, p, sel, l, tt)
        yb = _attention(q, k, v, tq, hp)
        n = nb * seq
        h = _merge_out(h.reshape(n, D_MODEL), ya.reshape(n, D_A), yb.reshape(n, D_B),
                       yc.reshape(n, D_C), p, l, tm, alpha).reshape(nb, seq, D_MODEL)
    return h


def kernel(x, w_in, b_in, pool_w, pool_scale, ssm_a_re, ssm_a_im, ssm_log_dt, ssm_b_re, ssm_b_im,
           ssm_c_re, ssm_c_im, ssm_d, w_glu, b_glu, w_up_a, w_up_b, w_up_c, w_out, ln_g, ln_b):
    return _trunk(x, w_in, b_in, pool_w, pool_scale, ssm_a_re, ssm_a_im, ssm_log_dt, ssm_b_re,
                  ssm_b_im, ssm_c_re, ssm_c_im, ssm_d, w_glu, b_glu, w_up_a, w_up_b, w_up_c, w_out,
                  ln_g, ln_b, tt=64, tq=512, tm=512)
```

```python
import functools
import math

import jax
import jax.numpy as jnp
import numpy as np
from jax import lax
from jax.experimental import pallas as pl
from jax.experimental.pallas import tpu as pltpu

F32 = jnp.float32
BF16 = jnp.bfloat16

D_MODEL = 1024
N_HEADS = 8
HEAD_DIM = 64
D_A = 256
D_B = N_HEADS * HEAD_DIM
D_C = 256
POOL_WINDOWS = (2, 4, 8, 16)
POOL_GROUP = 64
SSM_GROUPS = 16
SSM_GROUP = 16
SSM_STATE = 64
N_STATE = SSM_GROUPS * SSM_STATE
LN_EPS = 1e-5
LANES = 128
SUBLANES = 8
NEG_BIG = -1e30
LOG2E = math.log2(math.e)
NEXT_BLOCK_LEAD = 1
V_ROWS = 80

ZA_OFF, ZC_OFF, ZF_OFF, Q_OFF, K_OFF, V_OFF = 0, 256, 512, 640, 1152, 1664
C1 = 2176
C3 = 4096
AUG = HEAD_DIM
POOL_HISTORY = 16

VMEM_LIMIT = 56 * 1024 * 1024


def _sigmoid(x):
    return 0.5 * (jnp.tanh(0.5 * x) + 1.0)


def _silu(x):
    return x * _sigmoid(x)


def _gelu_tanh(x):
    return 0.5 * x * (1.0 + jnp.tanh(math.sqrt(2.0 / math.pi) * (x + 0.044715 * (x * x * x))))


def _log_sigmoid(x):
    return jnp.minimum(x, 0.0) - jnp.log(1.0 + jnp.exp(-jnp.abs(x)))


def _ssm_prep_body(are_ref, aim_ref, ldt_ref, bre_ref, bim_ref, abr_ref, abi_ref, bbr_ref, bbi_ref):
    are = are_ref[...]
    aim = aim_ref[...]
    dt = jnp.exp(ldt_ref[...])
    mag = jnp.exp(are * dt)
    ang = aim * dt
    abr = mag * jnp.cos(ang)
    abi = mag * jnp.sin(ang)
    den = are * are + aim * aim
    nr = abr - 1.0
    cre = (nr * are + abi * aim) / den
    cim = (abi * are - nr * aim) / den
    bre = bre_ref[...]
    bim = bim_ref[...]
    abr_ref[...] = abr
    abi_ref[...] = abi
    bbr_ref[...] = cre * bre - cim * bim
    bbi_ref[...] = cre * bim + cim * bre


def _ssm_prep(a_re, a_im, log_dt, b_re, b_im):
    L = a_re.shape[0]
    rows = L * SSM_GROUPS
    wide = SSM_GROUP * SSM_STATE
    tile = lambda a: jnp.tile(a.reshape(rows, SSM_STATE), (1, SSM_GROUP))
    ldt = jnp.broadcast_to(log_dt.reshape(rows, 1), (rows, wide))
    bt = lambda b: jnp.swapaxes(b, 2, 3).reshape(rows, wide)
    outs = pl.pallas_call(
        _ssm_prep_body,
        name="ssm_prep",
        out_shape=[jax.ShapeDtypeStruct((rows, wide), F32)] * 4,
    )(tile(a_re), tile(a_im), ldt, bt(b_re), bt(b_im))
    abr, abi, bbr, bbi = outs
    abr = abr[:, :SSM_STATE].reshape(L, SSM_GROUPS, SSM_STATE)
    abi = abi[:, :SSM_STATE].reshape(L, SSM_GROUPS, SSM_STATE)
    bbr = bbr.reshape(L, SSM_GROUPS, SSM_GROUP, SSM_STATE)
    bbi = bbi.reshape(L, SSM_GROUPS, SSM_GROUP, SSM_STATE)
    return abr, abi, bbr, bbi


def _mix_in_body(x_ref, w1_ref, b1_ref, pw_ref, ps_ref, are_ref, aim_ref, bbr_ref, bbi_ref,
                 cre_ref, cim_ref, d_ref, wglu_ref, bglu_ref, sel_ref,
                 ya_ref, yc_ref, q_ref, k_ref, v_ref,
                 xb_scr, z_scr, zc_scr, ext_scr, lv2_scr, lv4_scr, lv8_scr, slin_scr, slout_scr, utm_scr,
                 xr_scr, xi_scr, stre_scr, stim_scr, stf_scr, *, tt, nb):
    rows = nb * tt
    pitch = tt + 4
    hrows = POOL_HISTORY * nb
    step = pl.program_id(0)

    @pl.when(step == 0)
    def _():
        ext_scr[0:hrows, :] = jnp.zeros((hrows, D_A), F32)
        stre_scr[...] = jnp.zeros_like(stre_scr)
        stim_scr[...] = jnp.zeros_like(stim_scr)
        stf_scr[...] = jnp.zeros_like(stf_scr)

    xb_scr[...] = x_ref[...].reshape(rows, D_MODEL).astype(BF16)
    for c0 in range(0, Q_OFF, 256):
        c1 = min(c0 + 256, Q_OFF)
        z_scr[:, c0:c1] = (jnp.dot(xb_scr[...], w1_ref[:, c0:c1], preferred_element_type=F32)
                           + b1_ref[:, c0:c1])

    for b in range(nb):
        r0 = b * tt
        dst = slice(b * pitch, b * pitch + tt)
        slin_scr[0, dst, :] = z_scr[r0:r0 + tt, ZC_OFF:ZC_OFF + LANES]
        slin_scr[1, dst, :] = z_scr[r0:r0 + tt, ZC_OFF + LANES:ZC_OFF + 2 * LANES]
        slin_scr[2, dst, :] = _log_sigmoid(z_scr[r0:r0 + tt, ZF_OFF:ZF_OFF + LANES])
        slin_scr[3, dst, :] = z_scr[r0:r0 + tt, ZA_OFF:ZA_OFF + LANES]
        slin_scr[4, dst, :] = z_scr[r0:r0 + tt, ZA_OFF + LANES:ZA_OFF + 2 * LANES]

    def gather_t(t, carry):
        r = pl.ds(pl.multiple_of(t * nb, nb), nb)
        e = pl.ds(pl.multiple_of(hrows + t * nb, nb), nb)
        utm_scr[r, 0:LANES] = slin_scr[0, pl.ds(t, nb, stride=pitch), :]
        utm_scr[r, LANES:2 * LANES] = slin_scr[1, pl.ds(t, nb, stride=pitch), :]
        ext_scr[e, 0:LANES] = slin_scr[3, pl.ds(t, nb, stride=pitch), :]
        ext_scr[e, LANES:2 * LANES] = slin_scr[4, pl.ds(t, nb, stride=pitch), :]
        return carry

    lax.fori_loop(0, tt, gather_t, 0, unroll=4)

    ext_len = rows + hrows
    lv2_scr[nb:ext_len, :] = ext_scr[nb:ext_len, :] + ext_scr[0:ext_len - nb, :]
    lv4_scr[3 * nb:ext_len, :] = lv2_scr[3 * nb:ext_len, :] + lv2_scr[nb:ext_len - 2 * nb, :]
    lv8_scr[7 * nb:ext_len, :] = lv4_scr[7 * nb:ext_len, :] + lv4_scr[3 * nb:ext_len - 4 * nb, :]
    s16 = lv8_scr[hrows:ext_len, :] + lv8_scr[hrows - 8 * nb:ext_len - 8 * nb, :]
    lane_a = lax.broadcasted_iota(jnp.int32, (rows, D_A), 1)
    time_a = step * tt + lax.broadcasted_iota(jnp.int32, (tt, nb, D_A), 0).reshape(rows, D_A)
    win = jnp.where(lane_a < 64, 2, jnp.where(lane_a < 128, 4, jnp.where(lane_a < 192, 8, 16)))
    cnt = jnp.minimum(time_a + 1, win).astype(F32)
    wsum = jnp.where(lane_a < 64, lv2_scr[hrows:ext_len, :],
                     jnp.where(lane_a < 128, lv4_scr[hrows:ext_len, :],
                               jnp.where(lane_a < 192, lv8_scr[hrows:ext_len, :], s16)))
    pooled = wsum / cnt - ext_scr[hrows:ext_len, :]
    ya = jnp.dot(pooled.astype(BF16), pw_ref[...], preferred_element_type=F32) * ps_ref[...]
    slout_scr[3] = ya[:, 0:LANES]
    slout_scr[4] = ya[:, LANES:2 * LANES]
    ext_scr[0:hrows, :] = ext_scr[rows:ext_len, :]


    u_tm = utm_scr[...]
    ub = u_tm.astype(BF16)
    xr_scr[...] = jnp.dot(ub, bbr_ref[...], preferred_element_type=F32)
    xi_scr[...] = jnp.dot(ub, bbi_ref[...], preferred_element_type=F32)

    ar = are_ref[...]
    ai = aim_ref[...]

    def scan_t(t, carry):
        sr, si, f = carry
        r = pl.ds(pl.multiple_of(t * nb, nb), nb)
        nr = ar * sr - ai * si + xr_scr[r, :]
        ni = ar * si + ai * sr + xi_scr[r, :]
        xr_scr[r, :] = nr
        xi_scr[r, :] = ni
        f = f + slin_scr[2, pl.ds(t, nb, stride=pitch), :]
        slout_scr[2, r, :] = f
        return nr, ni, f

    sr, si, f = lax.fori_loop(0, tt, scan_t, (stre_scr[...], stim_scr[...], stf_scr[...]), unroll=2)
    stre_scr[...] = sr
    stim_scr[...] = si
    stf_scr[...] = f

    y = (jnp.dot(xr_scr[...].astype(BF16), cre_ref[...], preferred_element_type=F32)
         - jnp.dot(xi_scr[...].astype(BF16), cim_ref[...], preferred_element_type=F32))
    y = _gelu_tanh(y + d_ref[...] * u_tm)
    glu = jnp.dot(y.astype(BF16), wglu_ref[...], preferred_element_type=F32) + bglu_ref[...]
    y = y * _sigmoid(glu)
    slout_scr[0] = y[:, 0:LANES]
    slout_scr[1] = y[:, LANES:2 * LANES]

    lane_r = lax.broadcasted_iota(jnp.int32, (rows, LANES), 1)
    f_tm = slout_scr[2] * LOG2E
    hi = f_tm.astype(BF16).astype(F32)
    r1 = f_tm - hi
    mid = r1.astype(BF16).astype(F32)
    lo = r1 - mid
    fa = jnp.where(lane_r < 8, hi,
                   jnp.where(lane_r < 16, pltpu.roll(mid, 8, axis=1),
                             jnp.where(lane_r < 24, pltpu.roll(lo, 16, axis=1), 0.0))).astype(BF16)
    slout_scr[2] = jnp.dot(fa, sel_ref[...], preferred_element_type=F32)

    for b in range(nb):
        ya_ref[b, :, 0:LANES] = slout_scr[3, pl.ds(b, tt, stride=nb), :]
        ya_ref[b, :, LANES:2 * LANES] = slout_scr[4, pl.ds(b, tt, stride=nb), :]
        yc_ref[b, :, 0:LANES] = slout_scr[0, pl.ds(b, tt, stride=nb), :]
        yc_ref[b, :, LANES:2 * LANES] = slout_scr[1, pl.ds(b, tt, stride=nb), :]

    lane = lax.broadcasted_iota(jnp.int32, (tt, LANES), 1)
    in_lo = lane < AUG + 3
    in_hi = (lane >= AUG + 3) & (lane < AUG + 6)
    ones_lo = jnp.where((lane >= AUG) & in_lo, 1.0, 0.0)
    ones_hi = jnp.where(in_hi, 1.0, 0.0)
    out_refs = (q_ref, k_ref)
    for chunk in range(6):
        kind, half = divmod(chunk, 2)
        c0 = Q_OFF + chunk * 256
        zc_scr[chunk % 2] = (jnp.dot(xb_scr[...], w1_ref[:, c0:c0 + 256],
                                     preferred_element_type=F32) + b1_ref[:, c0:c0 + 256])
        for b in range(nb):
            if kind == 2:
                for pr in range(2):
                    v_ref[b, 2 * half + pr] = zc_scr[
                        chunk % 2, b * tt:(b + 1) * tt, pr * LANES:(pr + 1) * LANES].astype(BF16)
                continue
            fg = slout_scr[2, pl.ds(b, tt, stride=nb), :]
            for hl in range(4):
                h = 4 * half + hl
                c = (hl // 2) * LANES
                z = zc_scr[chunk % 2, b * tt:(b + 1) * tt, c:c + LANES]
                if hl % 2:
                    z = pltpu.roll(z, HEAD_DIM, axis=1)
                if kind == 0:
                    aug = jnp.where(in_lo, pltpu.roll(fg, AUG - 8 * h, axis=1), ones_hi)
                else:
                    aug = jnp.where(in_hi, pltpu.roll(fg, AUG - 8 * h, axis=1), ones_lo)
                out_refs[kind][b, h] = jnp.where(lane < AUG, z, aug).astype(BF16)


def _const_spec(shape):
    nd = len(shape)
    return pl.BlockSpec(shape, lambda *_: (0,) * nd, pipeline_mode=pl.Buffered(1))


def _layer_spec(stacked, layer):
    nd = stacked.ndim - 1
    return pl.BlockSpec((None,) + stacked.shape[1:], lambda *_: (layer,) + (0,) * nd,
                        pipeline_mode=pl.Buffered(1))


def _mix_in(x, p, sel, layer, tt):
    nb, seq, _ = x.shape
    nt = seq // tt
    rows = nb * tt
    pitch = tt + 4
    hrows = POOL_HISTORY * nb
    consts = [p[name] for name in ("w1", "b1", "pw", "ps", "are", "aim", "bbr", "bbi", "cre", "cim",
                                   "d", "wglu", "bglu")]
    in_specs = [pl.BlockSpec((nb, tt, D_MODEL), lambda i: (0, i, 0))]
    in_specs += [_layer_spec(c, layer) for c in consts] + [_const_spec(sel.shape)]
    out_shape = [
        jax.ShapeDtypeStruct((nb, seq, D_A), F32),
        jax.ShapeDtypeStruct((nb, seq, D_C), F32),
        jax.ShapeDtypeStruct((nb, N_HEADS, seq, LANES), BF16),
        jax.ShapeDtypeStruct((nb, N_HEADS, seq, LANES), BF16),
        jax.ShapeDtypeStruct((nb, N_HEADS // 2, seq, LANES), BF16),
    ]
    out_specs = [
        pl.BlockSpec((nb, tt, D_A), lambda i: (0, i, 0)),
        pl.BlockSpec((nb, tt, D_C), lambda i: (0, i, 0)),
        pl.BlockSpec((nb, N_HEADS, tt, LANES), lambda i: (0, 0, i, 0)),
        pl.BlockSpec((nb, N_HEADS, tt, LANES), lambda i: (0, 0, i, 0)),
        pl.BlockSpec((nb, N_HEADS // 2, tt, LANES), lambda i: (0, 0, i, 0)),
    ]
    scratch = [
        pltpu.VMEM((rows, D_MODEL), BF16),
        pltpu.VMEM((rows, Q_OFF), F32),
        pltpu.VMEM((2, rows, 256), F32),
        pltpu.VMEM((rows + hrows, D_A), F32),
        pltpu.VMEM((rows + hrows, D_A), F32),
        pltpu.VMEM((rows + hrows, D_A), F32),
        pltpu.VMEM((rows + hrows, D_A), F32),
        pltpu.VMEM((5, nb * pitch, LANES), F32),
        pltpu.VMEM((5, rows, LANES), F32),
        pltpu.VMEM((rows, D_C), F32),
        pltpu.VMEM((rows, N_STATE), F32),
        pltpu.VMEM((rows, N_STATE), F32),
        pltpu.VMEM((nb, N_STATE), F32),
        pltpu.VMEM((nb, N_STATE), F32),
        pltpu.VMEM((nb, LANES), F32),
    ]
    return pl.pallas_call(
        functools.partial(_mix_in_body, tt=tt, nb=nb),
        name="mix_in",
        grid=(nt,),
        in_specs=in_specs,
        out_specs=out_specs,
        out_shape=out_shape,
        scratch_shapes=scratch,
        compiler_params=pltpu.CompilerParams(
            dimension_semantics=("arbitrary",), vmem_limit_bytes=VMEM_LIMIT),
    )(x, *consts, sel)


def _attn_body(q_ref, k_ref, v_ref, o_ref, vt_scr, m_scr, acc_scr, st_scr, *, t, hp):
    i = pl.program_id(2)
    dims = (((1,), (1,)), ((), ()))

    for hh in range(hp):
        m_scr[hh] = jnp.full((1, t), NEG_BIG, F32)
        acc_scr[hh] = jnp.zeros((LANES, t), F32)

    key = lax.broadcasted_iota(jnp.int32, (t, t), 0)
    qry = lax.broadcasted_iota(jnp.int32, (t, t), 1)

    def scores(hh, j):
        rows = pl.ds(pl.multiple_of(j * t, t), t)
        return lax.dot_general(k_ref[0, hh, rows, :], q_ref[0, hh], dims,
                               preferred_element_type=F32)

    st_scr[0] = scores(0, 0)

    def read_scores(hh, masked):
        st = st_scr[hh]
        return jnp.where(key <= qry, st, NEG_BIG) if masked else st

    def block(j, masked):
        for hh in range(hp):
            if hh + 1 < hp:
                st_scr[hh + 1] = scores(hh + 1, j)
            if not masked and hh == hp - NEXT_BLOCK_LEAD:
                st_scr[0] = scores(0, j + 1)
            m_old = m_scr[hh]
            m_new = jnp.maximum(m_old, jnp.max(read_scores(hh, masked), axis=0, keepdims=True))
            p = jnp.exp2(read_scores(hh, masked) - m_new).astype(BF16)
            alpha = jnp.exp2(m_old - m_new)
            acc_scr[hh, 0:V_ROWS, :] = alpha * acc_scr[hh, 0:V_ROWS, :] + jnp.dot(
                vt_scr[hh, j], p, preferred_element_type=F32)
            m_scr[hh] = m_new

    def body(j, carry):
        block(j, False)
        return carry

    lax.fori_loop(0, i, body, 0)
    pad_rows = lax.broadcasted_iota(jnp.int32, (V_ROWS - HEAD_DIM, t), 0)
    ones_row = jnp.where(pad_rows == 0, 1.0, 0.0).astype(BF16)
    for pr in range(hp // 2):
        vt = v_ref[0, pr].astype(F32).T.astype(BF16)
        for hh in (2 * pr, 2 * pr + 1):
            vt_scr[hh, i, 0:HEAD_DIM, :] = vt[(hh % 2) * HEAD_DIM:(hh % 2 + 1) * HEAD_DIM]
            vt_scr[hh, i, HEAD_DIM:V_ROWS, :] = ones_row
    block(i, True)
    for pr in range(hp // 2):
        halves = []
        for hh in (2 * pr, 2 * pr + 1):
            inv = 1.0 / acc_scr[hh, HEAD_DIM:HEAD_DIM + 1, :]
            halves.append(acc_scr[hh, 0:HEAD_DIM, :] * inv)
        o_ref[0, :, pr * LANES:(pr + 1) * LANES] = jnp.concatenate(halves, axis=0).T


def _attention(q, k, v, t, hp):
    nb, nh, seq, _ = q.shape
    nq = seq // t
    return pl.pallas_call(
        functools.partial(_attn_body, t=t, hp=hp),
        name="fox_attention",
        grid=(nb, nh // hp, nq),
        in_specs=[
            pl.BlockSpec((1, hp, t, LANES), lambda b, h, i: (b, h, i, 0)),
            pl.BlockSpec((1, hp, seq, LANES), lambda b, h, i: (b, h, 0, 0)),
            pl.BlockSpec((1, hp // 2, t, LANES), lambda b, h, i: (b, h, i, 0)),
        ],
        out_specs=pl.BlockSpec((1, t, hp * HEAD_DIM), lambda b, h, i: (b, i, h)),
        out_shape=jax.ShapeDtypeStruct((nb, seq, D_B), F32),
        scratch_shapes=[
            pltpu.VMEM((hp, seq // t, V_ROWS, t), BF16),
            pltpu.VMEM((hp, 1, t), F32),
            pltpu.VMEM((hp, LANES, t), F32),
            pltpu.VMEM((hp, t, t), F32),
        ],
        compiler_params=pltpu.CompilerParams(
            dimension_semantics=("parallel", "parallel", "arbitrary"),
            vmem_limit_bytes=VMEM_LIMIT),
    )(q, k, v)


def _merge_out_body(x_ref, ya_ref, yb_ref, yc_ref, w3_ref, b3_ref, wa_ref, wb_ref, wc_ref, wo_ref,
                    g_ref, beta_ref, o_ref, *, alpha):
    x = x_ref[...]
    xb = x.astype(BF16)

    def proj(c0, c1):
        return jnp.dot(xb, w3_ref[:, c0:c1], preferred_element_type=F32) + b3_ref[:, c0:c1]

    ya = (ya_ref[...] * _silu(proj(0, 256))).astype(BF16)
    yc = (yc_ref[...] * _silu(proj(256, 512))).astype(BF16)
    yb = (yb_ref[...] * _silu(proj(512, 1024))).astype(BF16)
    merged = _sigmoid(proj(1024, 2048)) * jnp.dot(ya, wa_ref[...], preferred_element_type=F32)
    merged += _sigmoid(proj(2048, 3072)) * jnp.dot(yb, wb_ref[...], preferred_element_type=F32)
    merged += _sigmoid(proj(3072, 4096)) * jnp.dot(yc, wc_ref[...], preferred_element_type=F32)
    out = jnp.dot(merged.astype(BF16), wo_ref[...], preferred_element_type=F32)
    y = alpha * x + out
    mu = jnp.mean(y, axis=-1, keepdims=True)
    yc0 = y - mu
    var = jnp.mean(yc0 * yc0, axis=-1, keepdims=True)
    o_ref[...] = yc0 * lax.rsqrt(var + LN_EPS) * g_ref[...] + beta_ref[...]


def _merge_out(x2, ya2, yb2, yc2, p, layer, tm, alpha):
    n = x2.shape[0]
    consts = [p[name] for name in ("w3", "b3", "wa", "wb", "wc", "wo", "lng", "lnb")]
    row_spec = lambda w: pl.BlockSpec((tm, w), lambda i: (i, 0))
    return pl.pallas_call(
        functools.partial(_merge_out_body, alpha=alpha),
        name="merge_out",
        grid=(n // tm,),
        in_specs=[row_spec(D_MODEL), row_spec(D_A), row_spec(D_B), row_spec(D_C)]
        + [_layer_spec(c, layer) for c in consts],
        out_specs=row_spec(D_MODEL),
        out_shape=jax.ShapeDtypeStruct((n, D_MODEL), F32),
        compiler_params=pltpu.CompilerParams(
            dimension_semantics=("parallel",), vmem_limit_bytes=VMEM_LIMIT),
    )(x2, ya2, yb2, yc2, *consts)


def _aug_selector():
    sel = np.zeros((LANES, LANES), np.float32)
    for h in range(N_HEADS):
        for part in range(3):
            sel[part * 8 + h, 8 * h + part] = 1.0
            sel[part * 8 + h, 8 * h + 3 + part] = -1.0
    return jnp.asarray(sel, BF16)


def _block_diag(blocks):
    nl, g, r, c = blocks.shape
    eye = jnp.eye(g, dtype=blocks.dtype)
    return jnp.einsum("lgrc,gk->lgrkc", blocks, eye).reshape(nl, g * r, g * c)


def _stacked_params(nb, w_in, b_in, pool_w, pool_scale, abr, abi, bbr, bbi, c_re, c_im, ssm_d,
                    w_glu, b_glu, w_up_a, w_up_b, w_up_c, w_out, ln_g, ln_b):
    nl = w_in.shape[0]
    offs = np.cumsum([0, D_A, D_A, D_B, D_B, D_B, N_HEADS, D_B, D_C, D_C, 3 * D_MODEL])
    za, ga, zq, zk, zv, zf, gb, zc, gc, zg = [slice(int(offs[i]), int(offs[i + 1])) for i in range(10)]
    scale = LOG2E / math.sqrt(HEAD_DIM)
    fpad = LANES - N_HEADS
    b = b_in
    w = w_in.astype(BF16)
    wq = (w_in[..., zq] * scale).astype(BF16)
    w1 = jnp.concatenate([w[..., za], w[..., zc], jnp.pad(w[..., zf], ((0, 0), (0, 0), (0, fpad))),
                          wq, w[..., zk], w[..., zv]], axis=-1)
    b1 = jnp.concatenate([b[:, za], b[:, zc], jnp.pad(b[:, zf], ((0, 0), (0, fpad))),
                          b[:, zq] * scale, b[:, zk], b[:, zv]], axis=-1)
    w3 = jnp.concatenate([w[..., ga], w[..., gc], w[..., gb], w[..., zg]], axis=-1)
    b3 = jnp.concatenate([b[:, ga], b[:, gc], b[:, gb], b[:, zg]], axis=-1)
    row = lambda a: a.reshape(nl, 1, -1).astype(F32)
    tile_nb = lambda a: jnp.broadcast_to(a.reshape(nl, 1, N_STATE), (nl, nb, N_STATE))
    return {
        "w1": w1, "b1": row(b1),
        "pw": _block_diag(pool_w).astype(BF16), "ps": row(pool_scale),
        "are": tile_nb(abr), "aim": tile_nb(abi),
        "bbr": _block_diag(bbr).astype(BF16),
        "bbi": _block_diag(bbi).astype(BF16),
        "cre": _block_diag(jnp.swapaxes(c_re, 2, 3)).astype(BF16),
        "cim": _block_diag(jnp.swapaxes(c_im, 2, 3)).astype(BF16),
        "d": row(ssm_d), "wglu": w_glu.astype(BF16), "bglu": row(b_glu),
        "w3": w3, "b3": row(b3),
        "wa": w_up_a.astype(BF16), "wb": w_up_b.astype(BF16), "wc": w_up_c.astype(BF16),
        "wo": w_out.astype(BF16), "lng": row(ln_g), "lnb": row(ln_b),
    }


def _trunk(x, w_in, b_in, pool_w, pool_scale, ssm_a_re, ssm_a_im, ssm_log_dt, ssm_b_re, ssm_b_im,
           ssm_c_re, ssm_c_im, ssm_d, w_glu, b_glu, w_up_a, w_up_b, w_up_c, w_out, ln_g, ln_b,
           *, tt, tq, tm, hp=8, alpha=None):
    nb, seq, _ = x.shape
    depth = w_in.shape[0]
    if alpha is None:
        alpha = (2.0 * depth) ** 0.25
    abr, abi, bbr, bbi = _ssm_prep(ssm_a_re, ssm_a_im, ssm_log_dt, ssm_b_re, ssm_b_im)
    p = _stacked_params(nb, w_in, b_in, pool_w, pool_scale, abr, abi, bbr, bbi, ssm_c_re, ssm_c_im,
                        ssm_d, w_glu, b_glu, w_up_a, w_up_b, w_up_c, w_out, ln_g, ln_b)
    sel = _aug_selector()
    h = x
    for l in range(depth):
        ya, yc, q, k, v = _mix_in(h, p, sel, l, tt)
        yb = _attention(q, k, v, tq, hp)
        n = nb * seq
        h = _merge_out(h.reshape(n, D_MODEL), ya.reshape(n, D_A), yb.reshape(n, D_B),
                       yc.reshape(n, D_C), p, l, tm, alpha).reshape(nb, seq, D_MODEL)
    return h


def kernel(x, w_in, b_in, pool_w, pool_scale, ssm_a_re, ssm_a_im, ssm_log_dt, ssm_b_re, ssm_b_im,
           ssm_c_re, ssm_c_im, ssm_d, w_glu, b_glu, w_up_a, w_up_b, w_up_c, w_out, ln_g, ln_b):
    return _trunk(x, w_in, b_in, pool_w, pool_scale, ssm_a_re, ssm_a_im, ssm_log_dt, ssm_b_re,
                  ssm_b_im, ssm_c_re, ssm_c_im, ssm_d, w_glu, b_glu, w_up_a, w_up_b, w_up_c, w_out,
                  ln_g, ln_b, tt=64, tq=512, tm=1024)
```

```python
import functools
import math

import jax
import jax.numpy as jnp
import numpy as np
from jax import lax
from jax.experimental import pallas as pl
from jax.experimental.pallas import tpu as pltpu

F32 = jnp.float32
BF16 = jnp.bfloat16

D_MODEL = 1024
N_HEADS = 8
HEAD_DIM = 64
D_A = 256
D_B = N_HEADS * HEAD_DIM
D_C = 256
POOL_WINDOWS = (2, 4, 8, 16)
POOL_GROUP = 64
SSM_GROUPS = 16
SSM_GROUP = 16
SSM_STATE = 64
N_STATE = SSM_GROUPS * SSM_STATE
LN_EPS = 1e-5
LANES = 128
SUBLANES = 8
NEG_BIG = -1e30
LOG2E = math.log2(math.e)
NEXT_BLOCK_LEAD = 1
V_ROWS = 80

ZA_OFF, ZC_OFF, ZF_OFF, Q_OFF, K_OFF, V_OFF = 0, 256, 512, 640, 1152, 1664
C1 = 2176
C3 = 4096
AUG = HEAD_DIM
POOL_HISTORY = 16

VMEM_LIMIT = 56 * 1024 * 1024


def _sigmoid(x):
    return 0.5 * (jnp.tanh(0.5 * x) + 1.0)


def _silu(x):
    return x * _sigmoid(x)


def _gelu_tanh(x):
    return 0.5 * x * (1.0 + jnp.tanh(math.sqrt(2.0 / math.pi) * (x + 0.044715 * (x * x * x))))


def _log_sigmoid(x):
    return jnp.minimum(x, 0.0) - jnp.log(1.0 + jnp.exp(-jnp.abs(x)))


def _ssm_prep_body(are_ref, aim_ref, ldt_ref, bre_ref, bim_ref, abr_ref, abi_ref, bbr_ref, bbi_ref):
    are = are_ref[...]
    aim = aim_ref[...]
    dt = jnp.exp(ldt_ref[...])
    mag = jnp.exp(are * dt)
    ang = aim * dt
    abr = mag * jnp.cos(ang)
    abi = mag * jnp.sin(ang)
    den = are * are + aim * aim
    nr = abr - 1.0
    cre = (nr * are + abi * aim) / den
    cim = (abi * are - nr * aim) / den
    bre = bre_ref[...]
    bim = bim_ref[...]
    abr_ref[...] = abr
    abi_ref[...] = abi
    bbr_ref[...] = cre * bre - cim * bim
    bbi_ref[...] = cre * bim + cim * bre


def _ssm_prep(a_re, a_im, log_dt, b_re, b_im):
    L = a_re.shape[0]
    rows = L * SSM_GROUPS
    wide = SSM_GROUP * SSM_STATE
    tile = lambda a: jnp.tile(a.reshape(rows, SSM_STATE), (1, SSM_GROUP))
    ldt = jnp.broadcast_to(log_dt.reshape(rows, 1), (rows, wide))
    bt = lambda b: jnp.swapaxes(b, 2, 3).reshape(rows, wide)
    outs = pl.pallas_call(
        _ssm_prep_body,
        name="ssm_prep",
        out_shape=[jax.ShapeDtypeStruct((rows, wide), F32)] * 4,
    )(tile(a_re), tile(a_im), ldt, bt(b_re), bt(b_im))
    abr, abi, bbr, bbi = outs
    abr = abr[:, :SSM_STATE].reshape(L, SSM_GROUPS, SSM_STATE)
    abi = abi[:, :SSM_STATE].reshape(L, SSM_GROUPS, SSM_STATE)
    bbr = bbr.reshape(L, SSM_GROUPS, SSM_GROUP, SSM_STATE)
    bbi = bbi.reshape(L, SSM_GROUPS, SSM_GROUP, SSM_STATE)
    return abr, abi, bbr, bbi


def _mix_in_body(x_ref, w1_ref, b1_ref, pw_ref, ps_ref, are_ref, aim_ref, bbr_ref, bbi_ref,
                 cre_ref, cim_ref, d_ref, wglu_ref, bglu_ref, sel_ref,
                 ya_ref, yc_ref, q_ref, k_ref, v_ref,
                 xb_scr, z_scr, zc_scr, ext_scr, lv2_scr, lv4_scr, lv8_scr, slin_scr, slout_scr, utm_scr,
                 xr_scr, xi_scr, stre_scr, stim_scr, stf_scr, *, tt, nb):
    rows = nb * tt
    pitch = tt + 4
    hrows = POOL_HISTORY * nb
    step = pl.program_id(0)

    @pl.when(step == 0)
    def _():
        ext_scr[0:hrows, :] = jnp.zeros((hrows, D_A), F32)
        stre_scr[...] = jnp.zeros_like(stre_scr)
        stim_scr[...] = jnp.zeros_like(stim_scr)
        stf_scr[...] = jnp.zeros_like(stf_scr)

    xb_scr[...] = x_ref[...].reshape(rows, D_MODEL).astype(BF16)
    for c0 in range(0, Q_OFF, 256):
        c1 = min(c0 + 256, Q_OFF)
        z_scr[:, c0:c1] = (jnp.dot(xb_scr[...], w1_ref[:, c0:c1], preferred_element_type=F32)
                           + b1_ref[:, c0:c1])

    for b in range(nb):
        r0 = b * tt
        dst = slice(b * pitch, b * pitch + tt)
        slin_scr[0, dst, :] = z_scr[r0:r0 + tt, ZC_OFF:ZC_OFF + LANES]
        slin_scr[1, dst, :] = z_scr[r0:r0 + tt, ZC_OFF + LANES:ZC_OFF + 2 * LANES]
        slin_scr[2, dst, :] = _log_sigmoid(z_scr[r0:r0 + tt, ZF_OFF:ZF_OFF + LANES])
        slin_scr[3, dst, :] = z_scr[r0:r0 + tt, ZA_OFF:ZA_OFF + LANES]
        slin_scr[4, dst, :] = z_scr[r0:r0 + tt, ZA_OFF + LANES:ZA_OFF + 2 * LANES]

    def gather_t(t, carry):
        r = pl.ds(pl.multiple_of(t * nb, nb), nb)
        e = pl.ds(pl.multiple_of(hrows + t * nb, nb), nb)
        utm_scr[r, 0:LANES] = slin_scr[0, pl.ds(t, nb, stride=pitch), :]
        utm_scr[r, LANES:2 * LANES] = slin_scr[1, pl.ds(t, nb, stride=pitch), :]
        ext_scr[e, 0:LANES] = slin_scr[3, pl.ds(t, nb, stride=pitch), :]
        ext_scr[e, LANES:2 * LANES] = slin_scr[4, pl.ds(t, nb, stride=pitch), :]
        return carry

    lax.fori_loop(0, tt, gather_t, 0, unroll=4)

    ext_len = rows + hrows
    lv2_scr[nb:ext_len, :] = ext_scr[nb:ext_len, :] + ext_scr[0:ext_len - nb, :]
    lv4_scr[3 * nb:ext_len, :] = lv2_scr[3 * nb:ext_len, :] + lv2_scr[nb:ext_len - 2 * nb, :]
    lv8_scr[7 * nb:ext_len, :] = lv4_scr[7 * nb:ext_len, :] + lv4_scr[3 * nb:ext_len - 4 * nb, :]
    s16 = lv8_scr[hrows:ext_len, :] + lv8_scr[hrows - 8 * nb:ext_len - 8 * nb, :]
    lane_a = lax.broadcasted_iota(jnp.int32, (rows, D_A), 1)
    time_a = step * tt + lax.broadcasted_iota(jnp.int32, (tt, nb, D_A), 0).reshape(rows, D_A)
    win = jnp.where(lane_a < 64, 2, jnp.where(lane_a < 128, 4, jnp.where(lane_a < 192, 8, 16)))
    cnt = jnp.minimum(time_a + 1, win).astype(F32)
    wsum = jnp.where(lane_a < 64, lv2_scr[hrows:ext_len, :],
                     jnp.where(lane_a < 128, lv4_scr[hrows:ext_len, :],
                               jnp.where(lane_a < 192, lv8_scr[hrows:ext_len, :], s16)))
    pooled = wsum / cnt - ext_scr[hrows:ext_len, :]
    ya = jnp.dot(pooled.astype(BF16), pw_ref[...], preferred_element_type=F32) * ps_ref[...]
    slout_scr[3] = ya[:, 0:LANES]
    slout_scr[4] = ya[:, LANES:2 * LANES]
    ext_scr[0:hrows, :] = ext_scr[rows:ext_len, :]


    u_tm = utm_scr[...]
    ub = u_tm.astype(BF16)
    xr_scr[...] = jnp.dot(ub, bbr_ref[...], preferred_element_type=F32)
    xi_scr[...] = jnp.dot(ub, bbi_ref[...], preferred_element_type=F32)

    ar = are_ref[...]
    ai = aim_ref[...]

    def scan_t(t, carry):
        sr, si, f = carry
        r = pl.ds(pl.multiple_of(t * nb, nb), nb)
        nr = ar * sr - ai * si + xr_scr[r, :]
        ni = ar * si + ai * sr + xi_scr[r, :]
        xr_scr[r, :] = nr
        xi_scr[r, :] = ni
        f = f + slin_scr[2, pl.ds(t, nb, stride=pitch), :]
        slout_scr[2, r, :] = f
        return nr, ni, f

    n_chunks = (C1 - Q_OFF) // 256
    carry = (stre_scr[...], stim_scr[...], stf_scr[...])
    for t in range(tt):
        if t % (tt // n_chunks) == 0 and t // (tt // n_chunks) < n_chunks:
            chunk = t // (tt // n_chunks)
            c0 = Q_OFF + chunk * 256
            zc_scr[chunk] = (jnp.dot(xb_scr[...], w1_ref[:, c0:c0 + 256],
                                     preferred_element_type=F32) + b1_ref[:, c0:c0 + 256])
        carry = scan_t(t, carry)
    sr, si, f = carry
    stre_scr[...] = sr
    stim_scr[...] = si
    stf_scr[...] = f

    y = (jnp.dot(xr_scr[...].astype(BF16), cre_ref[...], preferred_element_type=F32)
         - jnp.dot(xi_scr[...].astype(BF16), cim_ref[...], preferred_element_type=F32))
    y = _gelu_tanh(y + d_ref[...] * u_tm)
    glu = jnp.dot(y.astype(BF16), wglu_ref[...], preferred_element_type=F32) + bglu_ref[...]
    y = y * _sigmoid(glu)
    slout_scr[0] = y[:, 0:LANES]
    slout_scr[1] = y[:, LANES:2 * LANES]

    lane_r = lax.broadcasted_iota(jnp.int32, (rows, LANES), 1)
    f_tm = slout_scr[2] * LOG2E
    hi = f_tm.astype(BF16).astype(F32)
    r1 = f_tm - hi
    mid = r1.astype(BF16).astype(F32)
    lo = r1 - mid
    fa = jnp.where(lane_r < 8, hi,
                   jnp.where(lane_r < 16, pltpu.roll(mid, 8, axis=1),
                             jnp.where(lane_r < 24, pltpu.roll(lo, 16, axis=1), 0.0))).astype(BF16)
    slout_scr[2] = jnp.dot(fa, sel_ref[...], preferred_element_type=F32)

    for b in range(nb):
        ya_ref[b, :, 0:LANES] = slout_scr[3, pl.ds(b, tt, stride=nb), :]
        ya_ref[b, :, LANES:2 * LANES] = slout_scr[4, pl.ds(b, tt, stride=nb), :]
        yc_ref[b, :, 0:LANES] = slout_scr[0, pl.ds(b, tt, stride=nb), :]
        yc_ref[b, :, LANES:2 * LANES] = slout_scr[1, pl.ds(b, tt, stride=nb), :]

    lane = lax.broadcasted_iota(jnp.int32, (tt, LANES), 1)
    in_lo = lane < AUG + 3
    in_hi = (lane >= AUG + 3) & (lane < AUG + 6)
    ones_lo = jnp.where((lane >= AUG) & in_lo, 1.0, 0.0)
    ones_hi = jnp.where(in_hi, 1.0, 0.0)
    out_refs = (q_ref, k_ref)
    for chunk in range(n_chunks):
        kind, half = divmod(chunk, 2)
        for b in range(nb):
            if kind == 2:
                for pr in range(2):
                    v_ref[b, 2 * half + pr] = zc_scr[
                        chunk, b * tt:(b + 1) * tt, pr * LANES:(pr + 1) * LANES].astype(BF16)
                continue
            fg = slout_scr[2, pl.ds(b, tt, stride=nb), :]
            for hl in range(4):
                h = 4 * half + hl
                c = (hl // 2) * LANES
                z = zc_scr[chunk, b * tt:(b + 1) * tt, c:c + LANES]
                if hl % 2:
                    z = pltpu.roll(z, HEAD_DIM, axis=1)
                if kind == 0:
                    aug = jnp.where(in_lo, pltpu.roll(fg, AUG - 8 * h, axis=1), ones_hi)
                else:
                    aug = jnp.where(in_hi, pltpu.roll(fg, AUG - 8 * h, axis=1), ones_lo)
                out_refs[kind][b, h] = jnp.where(lane < AUG, z, aug).astype(BF16)


def _const_spec(shape):
    nd = len(shape)
    return pl.BlockSpec(shape, lambda *_: (0,) * nd, pipeline_mode=pl.Buffered(1))


def _layer_spec(stacked, layer):
    nd = stacked.ndim - 1
    return pl.BlockSpec((None,) + stacked.shape[1:], lambda *_: (layer,) + (0,) * nd,
                        pipeline_mode=pl.Buffered(1))


def _mix_in(x, p, sel, layer, tt):
    nb, seq, _ = x.shape
    nt = seq // tt
    rows = nb * tt
    pitch = tt + 4
    hrows = POOL_HISTORY * nb
    consts = [p[name] for name in ("w1", "b1", "pw", "ps", "are", "aim", "bbr", "bbi", "cre", "cim",
                                   "d", "wglu", "bglu")]
    in_specs = [pl.BlockSpec((nb, tt, D_MODEL), lambda i: (0, i, 0))]
    in_specs += [_layer_spec(c, layer) for c in consts] + [_const_spec(sel.shape)]
    out_shape = [
        jax.ShapeDtypeStruct((nb, seq, D_A), F32),
        jax.ShapeDtypeStruct((nb, seq, D_C), F32),
        jax.ShapeDtypeStruct((nb, N_HEADS, seq, LANES), BF16),
        jax.ShapeDtypeStruct((nb, N_HEADS, seq, LANES), BF16),
        jax.ShapeDtypeStruct((nb, N_HEADS // 2, seq, LANES), BF16),
    ]
    out_specs = [
        pl.BlockSpec((nb, tt, D_A), lambda i: (0, i, 0)),
        pl.BlockSpec((nb, tt, D_C), lambda i: (0, i, 0)),
        pl.BlockSpec((nb, N_HEADS, tt, LANES), lambda i: (0, 0, i, 0)),
        pl.BlockSpec((nb, N_HEADS, tt, LANES), lambda i: (0, 0, i, 0)),
        pl.BlockSpec((nb, N_HEADS // 2, tt, LANES), lambda i: (0, 0, i, 0)),
    ]
    scratch = [
        pltpu.VMEM((rows, D_MODEL), BF16),
        pltpu.VMEM((rows, Q_OFF), F32),
        pltpu.VMEM(((C1 - Q_OFF) // 256, rows, 256), F32),
        pltpu.VMEM((rows + hrows, D_A), F32),
        pltpu.VMEM((rows + hrows, D_A), F32),
        pltpu.VMEM((rows + hrows, D_A), F32),
        pltpu.VMEM((rows + hrows, D_A), F32),
        pltpu.VMEM((5, nb * pitch, LANES), F32),
        pltpu.VMEM((5, rows, LANES), F32),
        pltpu.VMEM((rows, D_C), F32),
        pltpu.VMEM((rows, N_STATE), F32),
        pltpu.VMEM((rows, N_STATE), F32),
        pltpu.VMEM((nb, N_STATE), F32),
        pltpu.VMEM((nb, N_STATE), F32),
        pltpu.VMEM((nb, LANES), F32),
    ]
    return pl.pallas_call(
        functools.partial(_mix_in_body, tt=tt, nb=nb),
        name="mix_in",
        grid=(nt,),
        in_specs=in_specs,
        out_specs=out_specs,
        out_shape=out_shape,
        scratch_shapes=scratch,
        compiler_params=pltpu.CompilerParams(
            dimension_semantics=("arbitrary",), vmem_limit_bytes=VMEM_LIMIT),
    )(x, *consts, sel)


def _attn_body(q_ref, k_ref, v_ref, o_ref, vt_scr, m_scr, acc_scr, st_scr, *, t, hp):
    i = pl.program_id(2)
    dims = (((1,), (1,)), ((), ()))

    for hh in range(hp):
        m_scr[hh] = jnp.full((1, t), NEG_BIG, F32)
        acc_scr[hh] = jnp.zeros((LANES, t), F32)

    key = lax.broadcasted_iota(jnp.int32, (t, t), 0)
    qry = lax.broadcasted_iota(jnp.int32, (t, t), 1)

    def scores(hh, j):
        rows = pl.ds(pl.multiple_of(j * t, t), t)
        return lax.dot_general(k_ref[0, hh, rows, :], q_ref[0, hh], dims,
                               preferred_element_type=F32)

    st_scr[0] = scores(0, 0)

    def read_scores(hh, masked):
        st = st_scr[hh]
        return jnp.where(key <= qry, st, NEG_BIG) if masked else st

    def block(j, masked):
        for hh in range(hp):
            if hh + 1 < hp:
                st_scr[hh + 1] = scores(hh + 1, j)
            if not masked and hh == hp - NEXT_BLOCK_LEAD:
                st_scr[0] = scores(0, j + 1)
            m_old = m_scr[hh]
            m_new = jnp.maximum(m_old, jnp.max(read_scores(hh, masked), axis=0, keepdims=True))
            p = jnp.exp2(read_scores(hh, masked) - m_new).astype(BF16)
            alpha = jnp.exp2(m_old - m_new)
            acc_scr[hh, 0:V_ROWS, :] = alpha * acc_scr[hh, 0:V_ROWS, :] + jnp.dot(
                vt_scr[hh, j], p, preferred_element_type=F32)
            m_scr[hh] = m_new

    def body(j, carry):
        block(j, False)
        return carry

    lax.fori_loop(0, i, body, 0)
    pad_rows = lax.broadcasted_iota(jnp.int32, (V_ROWS - HEAD_DIM, t), 0)
    ones_row = jnp.where(pad_rows == 0, 1.0, 0.0).astype(BF16)
    for pr in range(hp // 2):
        vt = v_ref[0, pr].astype(F32).T.astype(BF16)
        for hh in (2 * pr, 2 * pr + 1):
            vt_scr[hh, i, 0:HEAD_DIM, :] = vt[(hh % 2) * HEAD_DIM:(hh % 2 + 1) * HEAD_DIM]
            vt_scr[hh, i, HEAD_DIM:V_ROWS, :] = ones_row
    block(i, True)
    for pr in range(hp // 2):
        halves = []
        for hh in (2 * pr, 2 * pr + 1):
            inv = 1.0 / acc_scr[hh, HEAD_DIM:HEAD_DIM + 1, :]
            halves.append(acc_scr[hh, 0:HEAD_DIM, :] * inv)
        o_ref[0, :, pr * LANES:(pr + 1) * LANES] = jnp.concatenate(halves, axis=0).T


def _attention(q, k, v, t, hp):
    nb, nh, seq, _ = q.shape
    nq = seq // t
    return pl.pallas_call(
        functools.partial(_attn_body, t=t, hp=hp),
        name="fox_attention",
        grid=(nb, nh // hp, nq),
        in_specs=[
            pl.BlockSpec((1, hp, t, LANES), lambda b, h, i: (b, h, i, 0)),
            pl.BlockSpec((1, hp, seq, LANES), lambda b, h, i: (b, h, 0, 0)),
            pl.BlockSpec((1, hp // 2, t, LANES), lambda b, h, i: (b, h, i, 0)),
        ],
        out_specs=pl.BlockSpec((1, t, hp * HEAD_DIM), lambda b, h, i: (b, i, h)),
        out_shape=jax.ShapeDtypeStruct((nb, seq, D_B), F32),
        scratch_shapes=[
            pltpu.VMEM((hp, seq // t, V_ROWS, t), BF16),
            pltpu.VMEM((hp, 1, t), F32),
            pltpu.VMEM((hp, LANES, t), F32),
            pltpu.VMEM((hp, t, t), F32),
        ],
        compiler_params=pltpu.CompilerParams(
            dimension_semantics=("parallel", "parallel", "arbitrary"),
            vmem_limit_bytes=VMEM_LIMIT),
    )(q, k, v)


def _merge_out_body(x_ref, ya_ref, yb_ref, yc_ref, w3_ref, b3_ref, wa_ref, wb_ref, wc_ref, wo_ref,
                    g_ref, beta_ref, o_ref, *, alpha):
    x = x_ref[...]
    xb = x.astype(BF16)

    def proj(c0, c1):
        return jnp.dot(xb, w3_ref[:, c0:c1], preferred_element_type=F32) + b3_ref[:, c0:c1]

    ya = (ya_ref[...] * _silu(proj(0, 256))).astype(BF16)
    yc = (yc_ref[...] * _silu(proj(256, 512))).astype(BF16)
    yb = (yb_ref[...] * _silu(proj(512, 1024))).astype(BF16)
    merged = _sigmoid(proj(1024, 2048)) * jnp.dot(ya, wa_ref[...], preferred_element_type=F32)
    merged += _sigmoid(proj(2048, 3072)) * jnp.dot(yb, wb_ref[...], preferred_element_type=F32)
    merged += _sigmoid(proj(3072, 4096)) * jnp.dot(yc, wc_ref[...], preferred_element_type=F32)
    out = jnp.dot(merged.astype(BF16), wo_ref[...], preferred_element_type=F32)
    y = alpha * x + out
    mu = jnp.mean(y, axis=-1, keepdims=True)
    yc0 = y - mu
    var = jnp.mean(yc0 * yc0, axis=-1, keepdims=True)
    o_ref[...] = yc0 * lax.rsqrt(var + LN_EPS) * g_ref[...] + beta_ref[...]


def _merge_out(x2, ya2, yb2, yc2, p, layer, tm, alpha):
    n = x2.shape[0]
    consts = [p[name] for name in ("w3", "b3", "wa", "wb", "wc", "wo", "lng", "lnb")]
    row_spec = lambda w: pl.BlockSpec((tm, w), lambda i: (i, 0))
    return pl.pallas_call(
        functools.partial(_merge_out_body, alpha=alpha),
        name="merge_out",
        grid=(n // tm,),
        in_specs=[row_spec(D_MODEL), row_spec(D_A), row_spec(D_B), row_spec(D_C)]
        + [_layer_spec(c, layer) for c in consts],
        out_specs=row_spec(D_MODEL),
        out_shape=jax.ShapeDtypeStruct((n, D_MODEL), F32),
        compiler_params=pltpu.CompilerParams(
            dimension_semantics=("parallel",), vmem_limit_bytes=VMEM_LIMIT),
    )(x2, ya2, yb2, yc2, *consts)


def _aug_selector():
    sel = np.zeros((LANES, LANES), np.float32)
    for h in range(N_HEADS):
        for part in range(3):
            sel[part * 8 + h, 8 * h + part] = 1.0
            sel[part * 8 + h, 8 * h + 3 + part] = -1.0
    return jnp.asarray(sel, BF16)


def _block_diag(blocks):
    nl, g, r, c = blocks.shape
    eye = jnp.eye(g, dtype=blocks.dtype)
    return jnp.einsum("lgrc,gk->lgrkc", blocks, eye).reshape(nl, g * r, g * c)


def _stacked_params(nb, w_in, b_in, pool_w, pool_scale, abr, abi, bbr, bbi, c_re, c_im, ssm_d,
                    w_glu, b_glu, w_up_a, w_up_b, w_up_c, w_out, ln_g, ln_b):
    nl = w_in.shape[0]
    offs = np.cumsum([0, D_A, D_A, D_B, D_B, D_B, N_HEADS, D_B, D_C, D_C, 3 * D_MODEL])
    za, ga, zq, zk, zv, zf, gb, zc, gc, zg = [slice(int(offs[i]), int(offs[i + 1])) for i in range(10)]
    scale = LOG2E / math.sqrt(HEAD_DIM)
    fpad = LANES - N_HEADS
    b = b_in
    w = lax.optimization_barrier(w_in.astype(BF16))
    wq = (w_in[..., zq] * scale).astype(BF16)
    w1 = jnp.concatenate([w[..., za], w[..., zc], jnp.pad(w[..., zf], ((0, 0), (0, 0), (0, fpad))),
                          wq, w[..., zk], w[..., zv]], axis=-1)
    b1 = jnp.concatenate([b[:, za], b[:, zc], jnp.pad(b[:, zf], ((0, 0), (0, fpad))),
                          b[:, zq] * scale, b[:, zk], b[:, zv]], axis=-1)
    w3 = jnp.concatenate([w[..., ga], w[..., gc], w[..., gb], w[..., zg]], axis=-1)
    b3 = jnp.concatenate([b[:, ga], b[:, gc], b[:, gb], b[:, zg]], axis=-1)
    row = lambda a: a.reshape(nl, 1, -1).astype(F32)
    tile_nb = lambda a: jnp.broadcast_to(a.reshape(nl, 1, N_STATE), (nl, nb, N_STATE))
    return {
        "w1": w1, "b1": row(b1),
        "pw": _block_diag(pool_w).astype(BF16), "ps": row(pool_scale),
        "are": tile_nb(abr), "aim": tile_nb(abi),
        "bbr": _block_diag(bbr).astype(BF16),
        "bbi": _block_diag(bbi).astype(BF16),
        "cre": _block_diag(jnp.swapaxes(c_re, 2, 3)).astype(BF16),
        "cim": _block_diag(jnp.swapaxes(c_im, 2, 3)).astype(BF16),
        "d": row(ssm_d), "wglu": w_glu.astype(BF16), "bglu": row(b_glu),
        "w3": w3, "b3": row(b3),
        "wa": w_up_a.astype(BF16), "wb": w_up_b.astype(BF16), "wc": w_up_c.astype(BF16),
        "wo": w_out.astype(BF16), "lng": row(ln_g), "lnb": row(ln_b),
    }


def _trunk(x, w_in, b_in, pool_w, pool_scale, ssm_a_re, ssm_a_im, ssm_log_dt, ssm_b_re, ssm_b_im,
           ssm_c_re, ssm_c_im, ssm_d, w_glu, b_glu, w_up_a, w_up_b, w_up_c, w_out, ln_g, ln_b,
           *, tt, tq, tm, hp=8, alpha=None):
    nb, seq, _ = x.shape
    depth = w_in.shape[0]
    if alpha is None:
        alpha = (2.0 * depth) ** 0.25
    abr, abi, bbr, bbi = _ssm_prep(ssm_a_re, ssm_a_im, ssm_log_dt, ssm_b_re, ssm_b_im)
    p = _stacked_params(nb, w_in, b_in, pool_w, pool_scale, abr, abi, bbr, bbi, ssm_c_re, ssm_c_im,
                        ssm_d, w_glu, b_glu, w_up_a, w_up_b, w_up_c, w_out, ln_g, ln_b)
    sel = _aug_selector()
    h = x
    for l in range(depth):
        ya, yc, q, k, v = _mix_in(h, p, sel, l, tt)
        yb = _attention(q, k, v, tq, hp)
        n = nb * seq
        h = _merge_out(h.reshape(n, D_MODEL), ya.reshape(n, D_A), yb.reshape(n, D_B),
                       yc.reshape(n, D_C), p, l, tm, alpha).reshape(nb, seq, D_MODEL)
    return h


def kernel(x, w_in, b_in, pool_w, pool_scale, ssm_a_re, ssm_a_im, ssm_log_dt, ssm_b_re, ssm_b_im,
           ssm_c_re, ssm_c_im, ssm_d, w_glu, b_glu, w_up_a, w_up_b, w_up_c, w_out, ln_g, ln_b):
    return _trunk(x, w_in, b_in, pool_w, pool_scale, ssm_a_re, ssm_a_im, ssm_log_dt, ssm_b_re,
                  ssm_b_im, ssm_c_re, ssm_c_im, ssm_d, w_glu, b_glu, w_up_a, w_up_b, w_up_c, w_out,
                  ln_g, ln_b, tt=64, tq=512, tm=1024)
```

```python
import functools
import math

import jax
import jax.numpy as jnp
import numpy as np
from jax import lax
from jax.experimental import pallas as pl
from jax.experimental.pallas import tpu as pltpu

F32 = jnp.float32
BF16 = jnp.bfloat16

D_MODEL = 1024
N_HEADS = 8
HEAD_DIM = 64
D_A = 256
D_B = N_HEADS * HEAD_DIM
D_C = 256
POOL_WINDOWS = (2, 4, 8, 16)
POOL_GROUP = 64
SSM_GROUPS = 16
SSM_GROUP = 16
SSM_STATE = 64
N_STATE = SSM_GROUPS * SSM_STATE
LN_EPS = 1e-5
LANES = 128
SUBLANES = 8
NEG_BIG = -1e30
LOG2E = math.log2(math.e)
NEXT_BLOCK_LEAD = 1
V_ROWS = 80

ZA_OFF, ZC_OFF, ZF_OFF, Q_OFF, K_OFF, V_OFF = 0, 256, 512, 640, 1152, 1664
C1 = 2176
C3 = 4096
AUG = HEAD_DIM
POOL_HISTORY = 16

VMEM_LIMIT = 56 * 1024 * 1024


def _sigmoid(x):
    return 0.5 * (jnp.tanh(0.5 * x) + 1.0)


def _silu(x):
    return x * _sigmoid(x)


def _gelu_tanh(x):
    return 0.5 * x * (1.0 + jnp.tanh(math.sqrt(2.0 / math.pi) * (x + 0.044715 * (x * x * x))))


def _log_sigmoid(x):
    return jnp.minimum(x, 0.0) - jnp.log(1.0 + jnp.exp(-jnp.abs(x)))


def _ssm_prep_body(are_ref, aim_ref, ldt_ref, bre_ref, bim_ref, abr_ref, abi_ref, bbr_ref, bbi_ref):
    are = are_ref[...]
    aim = aim_ref[...]
    dt = jnp.exp(ldt_ref[...])
    mag = jnp.exp(are * dt)
    ang = aim * dt
    abr = mag * jnp.cos(ang)
    abi = mag * jnp.sin(ang)
    den = are * are + aim * aim
    nr = abr - 1.0
    cre = (nr * are + abi * aim) / den
    cim = (abi * are - nr * aim) / den
    bre = bre_ref[...]
    bim = bim_ref[...]
    abr_ref[...] = abr
    abi_ref[...] = abi
    bbr_ref[...] = cre * bre - cim * bim
    bbi_ref[...] = cre * bim + cim * bre


def _ssm_prep(a_re, a_im, log_dt, b_re, b_im):
    L = a_re.shape[0]
    rows = L * SSM_GROUPS
    wide = SSM_GROUP * SSM_STATE
    tile = lambda a: jnp.tile(a.reshape(rows, SSM_STATE), (1, SSM_GROUP))
    ldt = jnp.broadcast_to(log_dt.reshape(rows, 1), (rows, wide))
    bt = lambda b: jnp.swapaxes(b, 2, 3).reshape(rows, wide)
    outs = pl.pallas_call(
        _ssm_prep_body,
        name="ssm_prep",
        out_shape=[jax.ShapeDtypeStruct((rows, wide), F32)] * 4,
    )(tile(a_re), tile(a_im), ldt, bt(b_re), bt(b_im))
    abr, abi, bbr, bbi = outs
    abr = abr[:, :SSM_STATE].reshape(L, SSM_GROUPS, SSM_STATE)
    abi = abi[:, :SSM_STATE].reshape(L, SSM_GROUPS, SSM_STATE)
    bbr = bbr.reshape(L, SSM_GROUPS, SSM_GROUP, SSM_STATE)
    bbi = bbi.reshape(L, SSM_GROUPS, SSM_GROUP, SSM_STATE)
    return abr, abi, bbr, bbi


def _mix_in_body(x_ref, w1_ref, b1_ref, pw_ref, ps_ref, are_ref, aim_ref, bbr_ref, bbi_ref,
                 cre_ref, cim_ref, d_ref, wglu_ref, bglu_ref, sel_ref,
                 ya_ref, yc_ref, q_ref, k_ref, v_ref,
                 xb_scr, z_scr, zc_scr, ext_scr, lv2_scr, lv4_scr, lv8_scr, slin_scr, slout_scr, utm_scr,
                 xr_scr, xi_scr, stre_scr, stim_scr, stf_scr, *, tt, nb):
    rows = nb * tt
    pitch = tt + 4
    hrows = POOL_HISTORY * nb
    step = pl.program_id(0)

    @pl.when(step == 0)
    def _():
        ext_scr[0:hrows, :] = jnp.zeros((hrows, D_A), F32)
        stre_scr[...] = jnp.zeros_like(stre_scr)
        stim_scr[...] = jnp.zeros_like(stim_scr)
        stf_scr[...] = jnp.zeros_like(stf_scr)

    xb_scr[...] = x_ref[...].reshape(rows, D_MODEL).astype(BF16)
    n_chunks = (C1 - Q_OFF) // 256

    def project_chunk(chunk):
        c0 = Q_OFF + chunk * 256
        zc_scr[chunk] = (jnp.dot(xb_scr[...], w1_ref[:, c0:c0 + 256],
                                 preferred_element_type=F32) + b1_ref[:, c0:c0 + 256])

    for c0 in range(0, Q_OFF, 256):
        c1 = min(c0 + 256, Q_OFF)
        z_scr[:, c0:c1] = (jnp.dot(xb_scr[...], w1_ref[:, c0:c1], preferred_element_type=F32)
                           + b1_ref[:, c0:c1])
    project_chunk(0)

    for b in range(nb):
        r0 = b * tt
        dst = slice(b * pitch, b * pitch + tt)
        slin_scr[0, dst, :] = z_scr[r0:r0 + tt, ZC_OFF:ZC_OFF + LANES]
        slin_scr[1, dst, :] = z_scr[r0:r0 + tt, ZC_OFF + LANES:ZC_OFF + 2 * LANES]
        slin_scr[2, dst, :] = _log_sigmoid(z_scr[r0:r0 + tt, ZF_OFF:ZF_OFF + LANES])
        slin_scr[3, dst, :] = z_scr[r0:r0 + tt, ZA_OFF:ZA_OFF + LANES]
        slin_scr[4, dst, :] = z_scr[r0:r0 + tt, ZA_OFF + LANES:ZA_OFF + 2 * LANES]
    for t in range(tt):
        r = slice(t * nb, (t + 1) * nb)
        e = slice(hrows + t * nb, hrows + (t + 1) * nb)
        utm_scr[r, 0:LANES] = slin_scr[0, pl.ds(t, nb, stride=pitch), :]
        utm_scr[r, LANES:2 * LANES] = slin_scr[1, pl.ds(t, nb, stride=pitch), :]
        ext_scr[e, 0:LANES] = slin_scr[3, pl.ds(t, nb, stride=pitch), :]
        ext_scr[e, LANES:2 * LANES] = slin_scr[4, pl.ds(t, nb, stride=pitch), :]
    project_chunk(1)

    ext_len = rows + hrows
    lv2_scr[nb:ext_len, :] = ext_scr[nb:ext_len, :] + ext_scr[0:ext_len - nb, :]
    lv4_scr[3 * nb:ext_len, :] = lv2_scr[3 * nb:ext_len, :] + lv2_scr[nb:ext_len - 2 * nb, :]
    lv8_scr[7 * nb:ext_len, :] = lv4_scr[7 * nb:ext_len, :] + lv4_scr[3 * nb:ext_len - 4 * nb, :]
    s16 = lv8_scr[hrows:ext_len, :] + lv8_scr[hrows - 8 * nb:ext_len - 8 * nb, :]
    lane_a = lax.broadcasted_iota(jnp.int32, (rows, D_A), 1)
    time_a = step * tt + lax.broadcasted_iota(jnp.int32, (tt, nb, D_A), 0).reshape(rows, D_A)
    win = jnp.where(lane_a < 64, 2, jnp.where(lane_a < 128, 4, jnp.where(lane_a < 192, 8, 16)))
    cnt = jnp.minimum(time_a + 1, win).astype(F32)
    wsum = jnp.where(lane_a < 64, lv2_scr[hrows:ext_len, :],
                     jnp.where(lane_a < 128, lv4_scr[hrows:ext_len, :],
                               jnp.where(lane_a < 192, lv8_scr[hrows:ext_len, :], s16)))
    pooled = wsum / cnt - ext_scr[hrows:ext_len, :]
    ya = jnp.dot(pooled.astype(BF16), pw_ref[...], preferred_element_type=F32) * ps_ref[...]
    slout_scr[3] = ya[:, 0:LANES]
    slout_scr[4] = ya[:, LANES:2 * LANES]
    ext_scr[0:hrows, :] = ext_scr[rows:ext_len, :]

    u_tm = utm_scr[...]
    ub = u_tm.astype(BF16)
    xr_scr[...] = jnp.dot(ub, bbr_ref[...], preferred_element_type=F32)
    xi_scr[...] = jnp.dot(ub, bbi_ref[...], preferred_element_type=F32)

    ar = are_ref[...]
    ai = aim_ref[...]
    sr, si, f = stre_scr[...], stim_scr[...], stf_scr[...]
    late_chunks = n_chunks - 2
    for t in range(tt):
        if t % (tt // late_chunks) == 0 and t // (tt // late_chunks) < late_chunks:
            project_chunk(2 + t // (tt // late_chunks))
        r = slice(t * nb, (t + 1) * nb)
        nr = ar * sr - ai * si + xr_scr[r, :]
        ni = ar * si + ai * sr + xi_scr[r, :]
        xr_scr[r, :] = nr
        xi_scr[r, :] = ni
        f = f + slin_scr[2, pl.ds(t, nb, stride=pitch), :]
        slout_scr[2, r, :] = f
        sr, si = nr, ni
    stre_scr[...] = sr
    stim_scr[...] = si
    stf_scr[...] = f

    lane_r = lax.broadcasted_iota(jnp.int32, (rows, LANES), 1)
    f_tm = slout_scr[2] * LOG2E
    hi = f_tm.astype(BF16).astype(F32)
    r1 = f_tm - hi
    mid = r1.astype(BF16).astype(F32)
    lo = r1 - mid
    fa = jnp.where(lane_r < 8, hi,
                   jnp.where(lane_r < 16, pltpu.roll(mid, 8, axis=1),
                             jnp.where(lane_r < 24, pltpu.roll(lo, 16, axis=1), 0.0))).astype(BF16)
    slout_scr[2] = jnp.dot(fa, sel_ref[...], preferred_element_type=F32)

    lane = lax.broadcasted_iota(jnp.int32, (tt, LANES), 1)
    in_lo = lane < AUG + 3
    in_hi = (lane >= AUG + 3) & (lane < AUG + 6)
    ones_lo = jnp.where((lane >= AUG) & in_lo, 1.0, 0.0)
    ones_hi = jnp.where(in_hi, 1.0, 0.0)
    out_refs = (q_ref, k_ref)

    def emit_chunk(chunk):
        kind, half = divmod(chunk, 2)
        for b in range(nb):
            if kind == 2:
                for pr in range(2):
                    v_ref[b, 2 * half + pr] = zc_scr[
                        chunk, b * tt:(b + 1) * tt, pr * LANES:(pr + 1) * LANES].astype(BF16)
                continue
            fg = slout_scr[2, pl.ds(b, tt, stride=nb), :]
            for hl in range(4):
                h = 4 * half + hl
                c = (hl // 2) * LANES
                z = zc_scr[chunk, b * tt:(b + 1) * tt, c:c + LANES]
                if hl % 2:
                    z = pltpu.roll(z, HEAD_DIM, axis=1)
                if kind == 0:
                    aug = jnp.where(in_lo, pltpu.roll(fg, AUG - 8 * h, axis=1), ones_hi)
                else:
                    aug = jnp.where(in_hi, pltpu.roll(fg, AUG - 8 * h, axis=1), ones_lo)
                out_refs[kind][b, h] = jnp.where(lane < AUG, z, aug).astype(BF16)

    emit_chunk(0)
    y = jnp.dot(xr_scr[...].astype(BF16), cre_ref[...], preferred_element_type=F32)
    emit_chunk(1)
    y = y - jnp.dot(xi_scr[...].astype(BF16), cim_ref[...], preferred_element_type=F32)
    emit_chunk(2)
    y = _gelu_tanh(y + d_ref[...] * u_tm)
    glu = jnp.dot(y.astype(BF16), wglu_ref[...], preferred_element_type=F32) + bglu_ref[...]
    emit_chunk(3)
    y = y * _sigmoid(glu)
    slout_scr[0] = y[:, 0:LANES]
    slout_scr[1] = y[:, LANES:2 * LANES]
    emit_chunk(4)
    emit_chunk(5)

    for b in range(nb):
        ya_ref[b, :, 0:LANES] = slout_scr[3, pl.ds(b, tt, stride=nb), :]
        ya_ref[b, :, LANES:2 * LANES] = slout_scr[4, pl.ds(b, tt, stride=nb), :]
        yc_ref[b, :, 0:LANES] = slout_scr[0, pl.ds(b, tt, stride=nb), :]
        yc_ref[b, :, LANES:2 * LANES] = slout_scr[1, pl.ds(b, tt, stride=nb), :]


def _const_spec(shape):
    nd = len(shape)
    return pl.BlockSpec(shape, lambda *_: (0,) * nd, pipeline_mode=pl.Buffered(1))


def _layer_spec(stacked, layer):
    nd = stacked.ndim - 1
    return pl.BlockSpec((None,) + stacked.shape[1:], lambda *_: (layer,) + (0,) * nd,
                        pipeline_mode=pl.Buffered(1))


def _mix_in(x, p, sel, layer, tt):
    nb, seq, _ = x.shape
    nt = seq // tt
    rows = nb * tt
    pitch = tt + 4
    hrows = POOL_HISTORY * nb
    consts = [p[name] for name in ("w1", "b1", "pw", "ps", "are", "aim", "bbr", "bbi", "cre", "cim",
                                   "d", "wglu", "bglu")]
    in_specs = [pl.BlockSpec((nb, tt, D_MODEL), lambda i: (0, i, 0))]
    in_specs += [_layer_spec(c, layer) for c in consts] + [_const_spec(sel.shape)]
    out_shape = [
        jax.ShapeDtypeStruct((nb, seq, D_A), F32),
        jax.ShapeDtypeStruct((nb, seq, D_C), F32),
        jax.ShapeDtypeStruct((nb, N_HEADS, seq, LANES), BF16),
        jax.ShapeDtypeStruct((nb, N_HEADS, seq, LANES), BF16),
        jax.ShapeDtypeStruct((nb, N_HEADS // 2, seq, LANES), BF16),
    ]
    out_specs = [
        pl.BlockSpec((nb, tt, D_A), lambda i: (0, i, 0)),
        pl.BlockSpec((nb, tt, D_C), lambda i: (0, i, 0)),
        pl.BlockSpec((nb, N_HEADS, tt, LANES), lambda i: (0, 0, i, 0)),
        pl.BlockSpec((nb, N_HEADS, tt, LANES), lambda i: (0, 0, i, 0)),
        pl.BlockSpec((nb, N_HEADS // 2, tt, LANES), lambda i: (0, 0, i, 0)),
    ]
    scratch = [
        pltpu.VMEM((rows, D_MODEL), BF16),
        pltpu.VMEM((rows, Q_OFF), F32),
        pltpu.VMEM(((C1 - Q_OFF) // 256, rows, 256), F32),
        pltpu.VMEM((rows + hrows, D_A), F32),
        pltpu.VMEM((rows + hrows, D_A), F32),
        pltpu.VMEM((rows + hrows, D_A), F32),
        pltpu.VMEM((rows + hrows, D_A), F32),
        pltpu.VMEM((5, nb * pitch, LANES), F32),
        pltpu.VMEM((5, rows, LANES), F32),
        pltpu.VMEM((rows, D_C), F32),
        pltpu.VMEM((rows, N_STATE), F32),
        pltpu.VMEM((rows, N_STATE), F32),
        pltpu.VMEM((nb, N_STATE), F32),
        pltpu.VMEM((nb, N_STATE), F32),
        pltpu.VMEM((nb, LANES), F32),
    ]
    return pl.pallas_call(
        functools.partial(_mix_in_body, tt=tt, nb=nb),
        name="mix_in",
        grid=(nt,),
        in_specs=in_specs,
        out_specs=out_specs,
        out_shape=out_shape,
        scratch_shapes=scratch,
        compiler_params=pltpu.CompilerParams(
            dimension_semantics=("arbitrary",), vmem_limit_bytes=VMEM_LIMIT),
    )(x, *consts, sel)


def _attn_body(q_ref, k_ref, v_ref, o_ref, vt_scr, m_scr, acc_scr, st_scr, *, t, hp):
    i = pl.program_id(2)
    dims = (((1,), (1,)), ((), ()))

    for hh in range(hp):
        m_scr[hh] = jnp.full((1, t), NEG_BIG, F32)
        acc_scr[hh] = jnp.zeros((LANES, t), F32)

    key = lax.broadcasted_iota(jnp.int32, (t, t), 0)
    qry = lax.broadcasted_iota(jnp.int32, (t, t), 1)

    def scores(hh, j):
        rows = pl.ds(pl.multiple_of(j * t, t), t)
        return lax.dot_general(k_ref[0, hh, rows, :], q_ref[0, hh], dims,
                               preferred_element_type=F32)

    st_scr[0] = scores(0, 0)

    def read_scores(hh, masked):
        st = st_scr[hh]
        return jnp.where(key <= qry, st, NEG_BIG) if masked else st

    def block(j, masked):
        for hh in range(hp):
            if hh + 1 < hp:
                st_scr[hh + 1] = scores(hh + 1, j)
            if not masked and hh == hp - NEXT_BLOCK_LEAD:
                st_scr[0] = scores(0, j + 1)
            m_old = m_scr[hh]
            m_new = jnp.maximum(m_old, jnp.max(read_scores(hh, masked), axis=0, keepdims=True))
            p = jnp.exp2(read_scores(hh, masked) - m_new).astype(BF16)
            alpha = jnp.exp2(m_old - m_new)
            acc_scr[hh, 0:V_ROWS, :] = alpha * acc_scr[hh, 0:V_ROWS, :] + jnp.dot(
                vt_scr[hh, j], p, preferred_element_type=F32)
            m_scr[hh] = m_new

    def body(j, carry):
        block(j, False)
        return carry

    lax.fori_loop(0, i, body, 0)
    pad_rows = lax.broadcasted_iota(jnp.int32, (V_ROWS - HEAD_DIM, t), 0)
    ones_row = jnp.where(pad_rows == 0, 1.0, 0.0).astype(BF16)
    for pr in range(hp // 2):
        vt = v_ref[0, pr].astype(F32).T.astype(BF16)
        for hh in (2 * pr, 2 * pr + 1):
            vt_scr[hh, i, 0:HEAD_DIM, :] = vt[(hh % 2) * HEAD_DIM:(hh % 2 + 1) * HEAD_DIM]
            vt_scr[hh, i, HEAD_DIM:V_ROWS, :] = ones_row
    block(i, True)
    for pr in range(hp // 2):
        halves = []
        for hh in (2 * pr, 2 * pr + 1):
            inv = 1.0 / acc_scr[hh, HEAD_DIM:HEAD_DIM + 1, :]
            halves.append(acc_scr[hh, 0:HEAD_DIM, :] * inv)
        o_ref[0, :, pr * LANES:(pr + 1) * LANES] = jnp.concatenate(halves, axis=0).T


def _attention(q, k, v, t, hp):
    nb, nh, seq, _ = q.shape
    nq = seq // t
    return pl.pallas_call(
        functools.partial(_attn_body, t=t, hp=hp),
        name="fox_attention",
        grid=(nb, nh // hp, nq),
        in_specs=[
            pl.BlockSpec((1, hp, t, LANES), lambda b, h, i: (b, h, i, 0)),
            pl.BlockSpec((1, hp, seq, LANES), lambda b, h, i: (b, h, 0, 0)),
            pl.BlockSpec((1, hp // 2, t, LANES), lambda b, h, i: (b, h, i, 0)),
        ],
        out_specs=pl.BlockSpec((1, t, hp * HEAD_DIM), lambda b, h, i: (b, i, h)),
        out_shape=jax.ShapeDtypeStruct((nb, seq, D_B), F32),
        scratch_shapes=[
            pltpu.VMEM((hp, seq // t, V_ROWS, t), BF16),
            pltpu.VMEM((hp, 1, t), F32),
            pltpu.VMEM((hp, LANES, t), F32),
            pltpu.VMEM((hp, t, t), F32),
        ],
        compiler_params=pltpu.CompilerParams(
            dimension_semantics=("parallel", "parallel", "arbitrary"),
            vmem_limit_bytes=VMEM_LIMIT),
    )(q, k, v)


def _merge_out_body(x_ref, ya_ref, yb_ref, yc_ref, w3_ref, b3_ref, wa_ref, wb_ref, wc_ref, wo_ref,
                    g_ref, beta_ref, o_ref, *, alpha):
    x = x_ref[...]
    xb = x.astype(BF16)

    def proj(c0, c1):
        return jnp.dot(xb, w3_ref[:, c0:c1], preferred_element_type=F32) + b3_ref[:, c0:c1]

    ya = (ya_ref[...] * _silu(proj(0, 256))).astype(BF16)
    yc = (yc_ref[...] * _silu(proj(256, 512))).astype(BF16)
    yb = (yb_ref[...] * _silu(proj(512, 1024))).astype(BF16)
    merged = _sigmoid(proj(1024, 2048)) * jnp.dot(ya, wa_ref[...], preferred_element_type=F32)
    merged += _sigmoid(proj(2048, 3072)) * jnp.dot(yb, wb_ref[...], preferred_element_type=F32)
    merged += _sigmoid(proj(3072, 4096)) * jnp.dot(yc, wc_ref[...], preferred_element_type=F32)
    out = jnp.dot(merged.astype(BF16), wo_ref[...], preferred_element_type=F32)
    y = alpha * x + out
    mu = jnp.mean(y, axis=-1, keepdims=True)
    yc0 = y - mu
    var = jnp.mean(yc0 * yc0, axis=-1, keepdims=True)
    o_ref[...] = yc0 * lax.rsqrt(var + LN_EPS) * g_ref[...] + beta_ref[...]


def _merge_out(x2, ya2, yb2, yc2, p, layer, tm, alpha):
    n = x2.shape[0]
    consts = [p[name] for name in ("w3", "b3", "wa", "wb", "wc", "wo", "lng", "lnb")]
    row_spec = lambda w: pl.BlockSpec((tm, w), lambda i: (i, 0))
    return pl.pallas_call(
        functools.partial(_merge_out_body, alpha=alpha),
        name="merge_out",
        grid=(n // tm,),
        in_specs=[row_spec(D_MODEL), row_spec(D_A), row_spec(D_B), row_spec(D_C)]
        + [_layer_spec(c, layer) for c in consts],
        out_specs=row_spec(D_MODEL),
        out_shape=jax.ShapeDtypeStruct((n, D_MODEL), F32),
        compiler_params=pltpu.CompilerParams(
            dimension_semantics=("parallel",), vmem_limit_bytes=VMEM_LIMIT),
    )(x2, ya2, yb2, yc2, *consts)


def _aug_selector():
    sel = np.zeros((LANES, LANES), np.float32)
    for h in range(N_HEADS):
        for part in range(3):
            sel[part * 8 + h, 8 * h + part] = 1.0
            sel[part * 8 + h, 8 * h + 3 + part] = -1.0
    return jnp.asarray(sel, BF16)


def _block_diag(blocks):
    nl, g, r, c = blocks.shape
    eye = jnp.eye(g, dtype=blocks.dtype)
    return jnp.einsum("lgrc,gk->lgrkc", blocks, eye).reshape(nl, g * r, g * c)


def _stacked_params(nb, w_in, b_in, pool_w, pool_scale, abr, abi, bbr, bbi, c_re, c_im, ssm_d,
                    w_glu, b_glu, w_up_a, w_up_b, w_up_c, w_out, ln_g, ln_b):
    nl = w_in.shape[0]
    offs = np.cumsum([0, D_A, D_A, D_B, D_B, D_B, N_HEADS, D_B, D_C, D_C, 3 * D_MODEL])
    za, ga, zq, zk, zv, zf, gb, zc, gc, zg = [slice(int(offs[i]), int(offs[i + 1])) for i in range(10)]
    scale = LOG2E / math.sqrt(HEAD_DIM)
    fpad = LANES - N_HEADS
    b = b_in
    w = lax.optimization_barrier(w_in.astype(BF16))
    wq = (w_in[..., zq] * scale).astype(BF16)
    w1 = jnp.concatenate([w[..., za], w[..., zc], jnp.pad(w[..., zf], ((0, 0), (0, 0), (0, fpad))),
                          wq, w[..., zk], w[..., zv]], axis=-1)
    b1 = jnp.concatenate([b[:, za], b[:, zc], jnp.pad(b[:, zf], ((0, 0), (0, fpad))),
                          b[:, zq] * scale, b[:, zk], b[:, zv]], axis=-1)
    w3 = jnp.concatenate([w[..., ga], w[..., gc], w[..., gb], w[..., zg]], axis=-1)
    b3 = jnp.concatenate([b[:, ga], b[:, gc], b[:, gb], b[:, zg]], axis=-1)
    row = lambda a: a.reshape(nl, 1, -1).astype(F32)
    tile_nb = lambda a: jnp.broadcast_to(a.reshape(nl, 1, N_STATE), (nl, nb, N_STATE))
    return {
        "w1": w1, "b1": row(b1),
        "pw": _block_diag(pool_w).astype(BF16), "ps": row(pool_scale),
        "are": tile_nb(abr), "aim": tile_nb(abi),
        "bbr": _block_diag(bbr).astype(BF16),
        "bbi": _block_diag(bbi).astype(BF16),
        "cre": _block_diag(jnp.swapaxes(c_re, 2, 3)).astype(BF16),
        "cim": _block_diag(jnp.swapaxes(c_im, 2, 3)).astype(BF16),
        "d": row(ssm_d), "wglu": w_glu.astype(BF16), "bglu": row(b_glu),
        "w3": w3, "b3": row(b3),
        "wa": w_up_a.astype(BF16), "wb": w_up_b.astype(BF16), "wc": w_up_c.astype(BF16),
        "wo": w_out.astype(BF16), "lng": row(ln_g), "lnb": row(ln_b),
    }


def _trunk(x, w_in, b_in, pool_w, pool_scale, ssm_a_re, ssm_a_im, ssm_log_dt, ssm_b_re, ssm_b_im,
           ssm_c_re, ssm_c_im, ssm_d, w_glu, b_glu, w_up_a, w_up_b, w_up_c, w_out, ln_g, ln_b,
           *, tt, tq, tm, hp=8, alpha=None):
    nb, seq, _ = x.shape
    depth = w_in.shape[0]
    if alpha is None:
        alpha = (2.0 * depth) ** 0.25
    abr, abi, bbr, bbi = _ssm_prep(ssm_a_re, ssm_a_im, ssm_log_dt, ssm_b_re, ssm_b_im)
    p = _stacked_params(nb, w_in, b_in, pool_w, pool_scale, abr, abi, bbr, bbi, ssm_c_re, ssm_c_im,
                        ssm_d, w_glu, b_glu, w_up_a, w_up_b, w_up_c, w_out, ln_g, ln_b)
    sel = _aug_selector()
    h = x
    for l in range(depth):
        ya, yc, q, k, v = _mix_in(h, p, sel, l, tt)
        yb = _attention(q, k, v, tq, hp)
        n = nb * seq
        h = _merge_out(h.reshape(n, D_MODEL), ya.reshape(n, D_A), yb.reshape(n, D_B),
                       yc.reshape(n, D_C), p, l, tm, alpha).reshape(nb, seq, D_MODEL)
    return h


def kernel(x, w_in, b_in, pool_w, pool_scale, ssm_a_re, ssm_a_im, ssm_log_dt, ssm_b_re, ssm_b_im,
           ssm_c_re, ssm_c_im, ssm_d, w_glu, b_glu, w_up_a, w_up_b, w_up_c, w_out, ln_g, ln_b):
    return _trunk(x, w_in, b_in, pool_w, pool_scale, ssm_a_re, ssm_a_im, ssm_log_dt, ssm_b_re,
                  ssm_b_im, ssm_c_re, ssm_c_im, ssm_d, w_glu, b_glu, w_up_a, w_up_b, w_up_c, w_out,
                  ln_g, ln_b, tt=64, tq=512, tm=1024)
```

```python
import functools
import math

import jax
import jax.numpy as jnp
import numpy as np
from jax import lax
from jax.experimental import pallas as pl
from jax.experimental.pallas import tpu as pltpu

F32 = jnp.float32
BF16 = jnp.bfloat16

D_MODEL = 1024
N_HEADS = 8
HEAD_DIM = 64
D_A = 256
D_B = N_HEADS * HEAD_DIM
D_C = 256
POOL_WINDOWS = (2, 4, 8, 16)
POOL_GROUP = 64
SSM_GROUPS = 16
SSM_GROUP = 16
SSM_STATE = 64
N_STATE = SSM_GROUPS * SSM_STATE
LN_EPS = 1e-5
LANES = 128
SUBLANES = 8
NEG_BIG = -1e30
LOG2E = math.log2(math.e)
NEXT_BLOCK_LEAD = 1
MERGE_SUB_ROWS = 256
V_ROWS = 80

ZA_OFF, ZC_OFF, ZF_OFF, Q_OFF, K_OFF, V_OFF = 0, 256, 512, 640, 1152, 1664
C1 = 2176
C3 = 4096
AUG = HEAD_DIM
POOL_HISTORY = 16

VMEM_LIMIT = 56 * 1024 * 1024


def _sigmoid(x):
    return 0.5 * (jnp.tanh(0.5 * x) + 1.0)


def _silu(x):
    return x * _sigmoid(x)


def _gelu_tanh(x):
    return 0.5 * x * (1.0 + jnp.tanh(math.sqrt(2.0 / math.pi) * (x + 0.044715 * (x * x * x))))


def _log_sigmoid(x):
    return jnp.minimum(x, 0.0) - jnp.log(1.0 + jnp.exp(-jnp.abs(x)))


def _ssm_prep_body(are_ref, aim_ref, ldt_ref, bre_ref, bim_ref, abr_ref, abi_ref, bbr_ref, bbi_ref):
    are = are_ref[...]
    aim = aim_ref[...]
    dt = jnp.exp(ldt_ref[...])
    mag = jnp.exp(are * dt)
    ang = aim * dt
    abr = mag * jnp.cos(ang)
    abi = mag * jnp.sin(ang)
    den = are * are + aim * aim
    nr = abr - 1.0
    cre = (nr * are + abi * aim) / den
    cim = (abi * are - nr * aim) / den
    bre = bre_ref[...]
    bim = bim_ref[...]
    abr_ref[...] = abr
    abi_ref[...] = abi
    bbr_ref[...] = cre * bre - cim * bim
    bbi_ref[...] = cre * bim + cim * bre


def _ssm_prep(a_re, a_im, log_dt, b_re, b_im):
    L = a_re.shape[0]
    rows = L * SSM_GROUPS
    wide = SSM_GROUP * SSM_STATE
    tile = lambda a: jnp.tile(a.reshape(rows, SSM_STATE), (1, SSM_GROUP))
    ldt = jnp.broadcast_to(log_dt.reshape(rows, 1), (rows, wide))
    bt = lambda b: jnp.swapaxes(b, 2, 3).reshape(rows, wide)
    outs = pl.pallas_call(
        _ssm_prep_body,
        name="ssm_prep",
        out_shape=[jax.ShapeDtypeStruct((rows, wide), F32)] * 4,
    )(tile(a_re), tile(a_im), ldt, bt(b_re), bt(b_im))
    abr, abi, bbr, bbi = outs
    abr = abr[:, :SSM_STATE].reshape(L, SSM_GROUPS, SSM_STATE)
    abi = abi[:, :SSM_STATE].reshape(L, SSM_GROUPS, SSM_STATE)
    bbr = bbr.reshape(L, SSM_GROUPS, SSM_GROUP, SSM_STATE)
    bbi = bbi.reshape(L, SSM_GROUPS, SSM_GROUP, SSM_STATE)
    return abr, abi, bbr, bbi


def _mix_in_body(x_ref, w1_ref, b1_ref, pw_ref, ps_ref, are_ref, aim_ref, bbr_ref, bbi_ref,
                 cre_ref, cim_ref, d_ref, wglu_ref, bglu_ref, sel_ref,
                 ya_ref, yc_ref, q_ref, k_ref, v_ref,
                 xb_scr, z_scr, zc_scr, ext_scr, lv2_scr, lv4_scr, lv8_scr, slin_scr, slout_scr, utm_scr,
                 xr_scr, xi_scr, stre_scr, stim_scr, stf_scr, *, tt, nb):
    rows = nb * tt
    pitch = tt + 4
    hrows = POOL_HISTORY * nb
    step = pl.program_id(0)

    @pl.when(step == 0)
    def _():
        ext_scr[0:hrows, :] = jnp.zeros((hrows, D_A), F32)
        stre_scr[...] = jnp.zeros_like(stre_scr)
        stim_scr[...] = jnp.zeros_like(stim_scr)
        stf_scr[...] = jnp.zeros_like(stf_scr)

    xb_scr[...] = x_ref[...].reshape(rows, D_MODEL).astype(BF16)
    n_chunks = (C1 - Q_OFF) // 256

    def project_chunk(chunk):
        c0 = Q_OFF + chunk * 256
        zc_scr[chunk] = (jnp.dot(xb_scr[...], w1_ref[:, c0:c0 + 256],
                                 preferred_element_type=F32) + b1_ref[:, c0:c0 + 256])

    for c0 in range(0, Q_OFF, 256):
        c1 = min(c0 + 256, Q_OFF)
        z_scr[:, c0:c1] = (jnp.dot(xb_scr[...], w1_ref[:, c0:c1], preferred_element_type=F32)
                           + b1_ref[:, c0:c1])
    project_chunk(0)

    for b in range(nb):
        r0 = b * tt
        dst = slice(b * pitch, b * pitch + tt)
        slin_scr[0, dst, :] = z_scr[r0:r0 + tt, ZC_OFF:ZC_OFF + LANES]
        slin_scr[1, dst, :] = z_scr[r0:r0 + tt, ZC_OFF + LANES:ZC_OFF + 2 * LANES]
        slin_scr[2, dst, :] = _log_sigmoid(z_scr[r0:r0 + tt, ZF_OFF:ZF_OFF + LANES])
        slin_scr[3, dst, :] = z_scr[r0:r0 + tt, ZA_OFF:ZA_OFF + LANES]
        slin_scr[4, dst, :] = z_scr[r0:r0 + tt, ZA_OFF + LANES:ZA_OFF + 2 * LANES]
    for t in range(tt):
        r = slice(t * nb, (t + 1) * nb)
        e = slice(hrows + t * nb, hrows + (t + 1) * nb)
        utm_scr[r, 0:LANES] = slin_scr[0, pl.ds(t, nb, stride=pitch), :]
        utm_scr[r, LANES:2 * LANES] = slin_scr[1, pl.ds(t, nb, stride=pitch), :]
        ext_scr[e, 0:LANES] = slin_scr[3, pl.ds(t, nb, stride=pitch), :]
        ext_scr[e, LANES:2 * LANES] = slin_scr[4, pl.ds(t, nb, stride=pitch), :]
    project_chunk(1)

    ext_len = rows + hrows
    lv2_scr[nb:ext_len, :] = ext_scr[nb:ext_len, :] + ext_scr[0:ext_len - nb, :]
    lv4_scr[3 * nb:ext_len, :] = lv2_scr[3 * nb:ext_len, :] + lv2_scr[nb:ext_len - 2 * nb, :]
    lv8_scr[7 * nb:ext_len, :] = lv4_scr[7 * nb:ext_len, :] + lv4_scr[3 * nb:ext_len - 4 * nb, :]
    s16 = lv8_scr[hrows:ext_len, :] + lv8_scr[hrows - 8 * nb:ext_len - 8 * nb, :]
    lane_a = lax.broadcasted_iota(jnp.int32, (rows, D_A), 1)
    time_a = step * tt + lax.broadcasted_iota(jnp.int32, (tt, nb, D_A), 0).reshape(rows, D_A)
    win = jnp.where(lane_a < 64, 2, jnp.where(lane_a < 128, 4, jnp.where(lane_a < 192, 8, 16)))
    cnt = jnp.minimum(time_a + 1, win).astype(F32)
    wsum = jnp.where(lane_a < 64, lv2_scr[hrows:ext_len, :],
                     jnp.where(lane_a < 128, lv4_scr[hrows:ext_len, :],
                               jnp.where(lane_a < 192, lv8_scr[hrows:ext_len, :], s16)))
    pooled = wsum / cnt - ext_scr[hrows:ext_len, :]
    ya = jnp.dot(pooled.astype(BF16), pw_ref[...], preferred_element_type=F32) * ps_ref[...]
    slout_scr[3] = ya[:, 0:LANES]
    slout_scr[4] = ya[:, LANES:2 * LANES]
    ext_scr[0:hrows, :] = ext_scr[rows:ext_len, :]

    u_tm = utm_scr[...]
    ub = u_tm.astype(BF16)
    xr_scr[...] = jnp.dot(ub, bbr_ref[...], preferred_element_type=F32)
    xi_scr[...] = jnp.dot(ub, bbi_ref[...], preferred_element_type=F32)

    ar = are_ref[...]
    ai = aim_ref[...]
    sr, si, f = stre_scr[...], stim_scr[...], stf_scr[...]
    late_chunks = n_chunks - 2
    for t in range(tt):
        if t % (tt // late_chunks) == 0 and t // (tt // late_chunks) < late_chunks:
            project_chunk(2 + t // (tt // late_chunks))
        r = slice(t * nb, (t + 1) * nb)
        nr = ar * sr - ai * si + xr_scr[r, :]
        ni = ar * si + ai * sr + xi_scr[r, :]
        xr_scr[r, :] = nr
        xi_scr[r, :] = ni
        f = f + slin_scr[2, pl.ds(t, nb, stride=pitch), :]
        slout_scr[2, r, :] = f
        sr, si = nr, ni
    stre_scr[...] = sr
    stim_scr[...] = si
    stf_scr[...] = f

    lane_r = lax.broadcasted_iota(jnp.int32, (rows, LANES), 1)
    f_tm = slout_scr[2] * LOG2E
    hi = f_tm.astype(BF16).astype(F32)
    r1 = f_tm - hi
    mid = r1.astype(BF16).astype(F32)
    lo = r1 - mid
    fa = jnp.where(lane_r < 8, hi,
                   jnp.where(lane_r < 16, pltpu.roll(mid, 8, axis=1),
                             jnp.where(lane_r < 24, pltpu.roll(lo, 16, axis=1), 0.0))).astype(BF16)
    slout_scr[2] = jnp.dot(fa, sel_ref[...], preferred_element_type=F32)

    lane = lax.broadcasted_iota(jnp.int32, (tt, LANES), 1)
    in_lo = lane < AUG + 3
    in_hi = (lane >= AUG + 3) & (lane < AUG + 6)
    ones_lo = jnp.where((lane >= AUG) & in_lo, 1.0, 0.0)
    ones_hi = jnp.where(in_hi, 1.0, 0.0)
    out_refs = (q_ref, k_ref)

    def emit_chunk(chunk):
        kind, half = divmod(chunk, 2)
        for b in range(nb):
            if kind == 2:
                for pr in range(2):
                    v_ref[b, 2 * half + pr] = zc_scr[
                        chunk, b * tt:(b + 1) * tt, pr * LANES:(pr + 1) * LANES].astype(BF16)
                continue
            fg = slout_scr[2, pl.ds(b, tt, stride=nb), :]
            for hl in range(4):
                h = 4 * half + hl
                c = (hl // 2) * LANES
                z = zc_scr[chunk, b * tt:(b + 1) * tt, c:c + LANES]
                if hl % 2:
                    z = pltpu.roll(z, HEAD_DIM, axis=1)
                if kind == 0:
                    aug = jnp.where(in_lo, pltpu.roll(fg, AUG - 8 * h, axis=1), ones_hi)
                else:
                    aug = jnp.where(in_hi, pltpu.roll(fg, AUG - 8 * h, axis=1), ones_lo)
                out_refs[kind][b, h] = jnp.where(lane < AUG, z, aug).astype(BF16)

    emit_chunk(0)
    y = jnp.dot(xr_scr[...].astype(BF16), cre_ref[...], preferred_element_type=F32)
    emit_chunk(1)
    y = y - jnp.dot(xi_scr[...].astype(BF16), cim_ref[...], preferred_element_type=F32)
    emit_chunk(2)
    y = _gelu_tanh(y + d_ref[...] * u_tm)
    glu = jnp.dot(y.astype(BF16), wglu_ref[...], preferred_element_type=F32) + bglu_ref[...]
    emit_chunk(3)
    y = y * _sigmoid(glu)
    slout_scr[0] = y[:, 0:LANES]
    slout_scr[1] = y[:, LANES:2 * LANES]
    emit_chunk(4)
    emit_chunk(5)

    for b in range(nb):
        ya_ref[b, :, 0:LANES] = slout_scr[3, pl.ds(b, tt, stride=nb), :]
        ya_ref[b, :, LANES:2 * LANES] = slout_scr[4, pl.ds(b, tt, stride=nb), :]
        yc_ref[b, :, 0:LANES] = slout_scr[0, pl.ds(b, tt, stride=nb), :]
        yc_ref[b, :, LANES:2 * LANES] = slout_scr[1, pl.ds(b, tt, stride=nb), :]


def _const_spec(shape):
    nd = len(shape)
    return pl.BlockSpec(shape, lambda *_: (0,) * nd, pipeline_mode=pl.Buffered(1))


def _layer_spec(stacked, layer):
    nd = stacked.ndim - 1
    return pl.BlockSpec((None,) + stacked.shape[1:], lambda *_: (layer,) + (0,) * nd,
                        pipeline_mode=pl.Buffered(1))


def _mix_in(x, p, sel, layer, tt):
    nb, seq, _ = x.shape
    nt = seq // tt
    rows = nb * tt
    pitch = tt + 4
    hrows = POOL_HISTORY * nb
    consts = [p[name] for name in ("w1", "b1", "pw", "ps", "are", "aim", "bbr", "bbi", "cre", "cim",
                                   "d", "wglu", "bglu")]
    in_specs = [pl.BlockSpec((nb, tt, D_MODEL), lambda i: (0, i, 0))]
    in_specs += [_layer_spec(c, layer) for c in consts] + [_const_spec(sel.shape)]
    out_shape = [
        jax.ShapeDtypeStruct((nb, seq, D_A), F32),
        jax.ShapeDtypeStruct((nb, seq, D_C), F32),
        jax.ShapeDtypeStruct((nb, N_HEADS, seq, LANES), BF16),
        jax.ShapeDtypeStruct((nb, N_HEADS, seq, LANES), BF16),
        jax.ShapeDtypeStruct((nb, N_HEADS // 2, seq, LANES), BF16),
    ]
    out_specs = [
        pl.BlockSpec((nb, tt, D_A), lambda i: (0, i, 0)),
        pl.BlockSpec((nb, tt, D_C), lambda i: (0, i, 0)),
        pl.BlockSpec((nb, N_HEADS, tt, LANES), lambda i: (0, 0, i, 0)),
        pl.BlockSpec((nb, N_HEADS, tt, LANES), lambda i: (0, 0, i, 0)),
        pl.BlockSpec((nb, N_HEADS // 2, tt, LANES), lambda i: (0, 0, i, 0)),
    ]
    scratch = [
        pltpu.VMEM((rows, D_MODEL), BF16),
        pltpu.VMEM((rows, Q_OFF), F32),
        pltpu.VMEM(((C1 - Q_OFF) // 256, rows, 256), F32),
        pltpu.VMEM((rows + hrows, D_A), F32),
        pltpu.VMEM((rows + hrows, D_A), F32),
        pltpu.VMEM((rows + hrows, D_A), F32),
        pltpu.VMEM((rows + hrows, D_A), F32),
        pltpu.VMEM((5, nb * pitch, LANES), F32),
        pltpu.VMEM((5, rows, LANES), F32),
        pltpu.VMEM((rows, D_C), F32),
        pltpu.VMEM((rows, N_STATE), F32),
        pltpu.VMEM((rows, N_STATE), F32),
        pltpu.VMEM((nb, N_STATE), F32),
        pltpu.VMEM((nb, N_STATE), F32),
        pltpu.VMEM((nb, LANES), F32),
    ]
    return pl.pallas_call(
        functools.partial(_mix_in_body, tt=tt, nb=nb),
        name="mix_in",
        grid=(nt,),
        in_specs=in_specs,
        out_specs=out_specs,
        out_shape=out_shape,
        scratch_shapes=scratch,
        compiler_params=pltpu.CompilerParams(
            dimension_semantics=("arbitrary",), vmem_limit_bytes=VMEM_LIMIT),
    )(x, *consts, sel)


def _attn_body(q_ref, k_ref, v_ref, o_ref, vt_scr, m_scr, acc_scr, st_scr, *, t, hp):
    i = pl.program_id(2)
    dims = (((1,), (1,)), ((), ()))

    for hh in range(hp):
        m_scr[hh] = jnp.full((1, t), NEG_BIG, F32)
        acc_scr[hh] = jnp.zeros((LANES, t), F32)

    key = lax.broadcasted_iota(jnp.int32, (t, t), 0)
    qry = lax.broadcasted_iota(jnp.int32, (t, t), 1)

    def scores(hh, j):
        rows = pl.ds(pl.multiple_of(j * t, t), t)
        return lax.dot_general(k_ref[0, hh, rows, :], q_ref[0, hh], dims,
                               preferred_element_type=F32)

    st_scr[0] = scores(0, 0)

    def read_scores(hh, masked):
        st = st_scr[hh]
        return jnp.where(key <= qry, st, NEG_BIG) if masked else st

    def block(j, masked):
        for hh in range(hp):
            if hh + 1 < hp:
                st_scr[hh + 1] = scores(hh + 1, j)
            if not masked and hh == hp - NEXT_BLOCK_LEAD:
                st_scr[0] = scores(0, j + 1)
            m_old = m_scr[hh]
            m_new = jnp.maximum(m_old, jnp.max(read_scores(hh, masked), axis=0, keepdims=True))
            p = jnp.exp2(read_scores(hh, masked) - m_new).astype(BF16)
            alpha = jnp.exp2(m_old - m_new)
            acc_scr[hh, 0:V_ROWS, :] = alpha * acc_scr[hh, 0:V_ROWS, :] + jnp.dot(
                vt_scr[hh, j], p, preferred_element_type=F32)
            m_scr[hh] = m_new

    def body(jj, carry):
        block(2 * jj, False)
        block(2 * jj + 1, False)
        return carry

    lax.fori_loop(0, i // 2, body, 0)

    @pl.when(i % 2 == 1)
    def _():
        block(i - 1, False)

    pad_rows = lax.broadcasted_iota(jnp.int32, (V_ROWS - HEAD_DIM, t), 0)
    ones_row = jnp.where(pad_rows == 0, 1.0, 0.0).astype(BF16)
    for pr in range(hp // 2):
        vt = v_ref[0, pr].astype(F32).T.astype(BF16)
        for hh in (2 * pr, 2 * pr + 1):
            vt_scr[hh, i, 0:HEAD_DIM, :] = vt[(hh % 2) * HEAD_DIM:(hh % 2 + 1) * HEAD_DIM]
            vt_scr[hh, i, HEAD_DIM:V_ROWS, :] = ones_row
    block(i, True)
    for pr in range(hp // 2):
        halves = []
        for hh in (2 * pr, 2 * pr + 1):
            inv = 1.0 / acc_scr[hh, HEAD_DIM:HEAD_DIM + 1, :]
            halves.append(acc_scr[hh, 0:HEAD_DIM, :] * inv)
        o_ref[0, :, pr * LANES:(pr + 1) * LANES] = jnp.concatenate(halves, axis=0).T


def _attention(q, k, v, t, hp):
    nb, nh, seq, _ = q.shape
    nq = seq // t
    return pl.pallas_call(
        functools.partial(_attn_body, t=t, hp=hp),
        name="fox_attention",
        grid=(nb, nh // hp, nq),
        in_specs=[
            pl.BlockSpec((1, hp, t, LANES), lambda b, h, i: (b, h, i, 0)),
            pl.BlockSpec((1, hp, seq, LANES), lambda b, h, i: (b, h, 0, 0)),
            pl.BlockSpec((1, hp // 2, t, LANES), lambda b, h, i: (b, h, i, 0)),
        ],
        out_specs=pl.BlockSpec((1, t, hp * HEAD_DIM), lambda b, h, i: (b, i, h)),
        out_shape=jax.ShapeDtypeStruct((nb, seq, D_B), F32),
        scratch_shapes=[
            pltpu.VMEM((hp, seq // t, V_ROWS, t), BF16),
            pltpu.VMEM((hp, 1, t), F32),
            pltpu.VMEM((hp, LANES, t), F32),
            pltpu.VMEM((hp, t, t), F32),
        ],
        compiler_params=pltpu.CompilerParams(
            dimension_semantics=("parallel", "parallel", "arbitrary"),
            vmem_limit_bytes=VMEM_LIMIT),
    )(q, k, v)


def _merge_out_body(x_ref, ya_ref, yb_ref, yc_ref, w3_ref, b3_ref, wa_ref, wb_ref, wc_ref, wo_ref,
                    g_ref, beta_ref, o_ref, *, alpha, sub):
    for s0 in range(0, x_ref.shape[0], sub):
        rs = slice(s0, s0 + sub)
        x = x_ref[rs, :]
        xb = x.astype(BF16)

        def proj(c0, c1):
            return jnp.dot(xb, w3_ref[:, c0:c1], preferred_element_type=F32) + b3_ref[:, c0:c1]

        ya = (ya_ref[rs, :] * _silu(proj(0, 256))).astype(BF16)
        yc = (yc_ref[rs, :] * _silu(proj(256, 512))).astype(BF16)
        yb = (yb_ref[rs, :] * _silu(proj(512, 1024))).astype(BF16)
        merged = _sigmoid(proj(1024, 2048)) * jnp.dot(ya, wa_ref[...], preferred_element_type=F32)
        merged += _sigmoid(proj(2048, 3072)) * jnp.dot(yb, wb_ref[...], preferred_element_type=F32)
        merged += _sigmoid(proj(3072, 4096)) * jnp.dot(yc, wc_ref[...], preferred_element_type=F32)
        out = jnp.dot(merged.astype(BF16), wo_ref[...], preferred_element_type=F32)
        y = alpha * x + out
        mu = jnp.mean(y, axis=-1, keepdims=True)
        yc0 = y - mu
        var = jnp.mean(yc0 * yc0, axis=-1, keepdims=True)
        o_ref[rs, :] = yc0 * lax.rsqrt(var + LN_EPS) * g_ref[...] + beta_ref[...]


def _merge_out(x2, ya2, yb2, yc2, p, layer, tm, alpha):
    n = x2.shape[0]
    consts = [p[name] for name in ("w3", "b3", "wa", "wb", "wc", "wo", "lng", "lnb")]
    row_spec = lambda w: pl.BlockSpec((tm, w), lambda i: (i, 0))
    return pl.pallas_call(
        functools.partial(_merge_out_body, alpha=alpha, sub=min(tm, MERGE_SUB_ROWS)),
        name="merge_out",
        grid=(n // tm,),
        in_specs=[row_spec(D_MODEL), row_spec(D_A), row_spec(D_B), row_spec(D_C)]
        + [_layer_spec(c, layer) for c in consts],
        out_specs=row_spec(D_MODEL),
        out_shape=jax.ShapeDtypeStruct((n, D_MODEL), F32),
        compiler_params=pltpu.CompilerParams(
            dimension_semantics=("parallel",), vmem_limit_bytes=VMEM_LIMIT),
    )(x2, ya2, yb2, yc2, *consts)


def _aug_selector():
    sel = np.zeros((LANES, LANES), np.float32)
    for h in range(N_HEADS):
        for part in range(3):
            sel[part * 8 + h, 8 * h + part] = 1.0
            sel[part * 8 + h, 8 * h + 3 + part] = -1.0
    return jnp.asarray(sel, BF16)


def _block_diag(blocks):
    nl, g, r, c = blocks.shape
    eye = jnp.eye(g, dtype=blocks.dtype)
    return jnp.einsum("lgrc,gk->lgrkc", blocks, eye).reshape(nl, g * r, g * c)


def _stacked_params(nb, w_in, b_in, pool_w, pool_scale, abr, abi, bbr, bbi, c_re, c_im, ssm_d,
                    w_glu, b_glu, w_up_a, w_up_b, w_up_c, w_out, ln_g, ln_b):
    nl = w_in.shape[0]
    offs = np.cumsum([0, D_A, D_A, D_B, D_B, D_B, N_HEADS, D_B, D_C, D_C, 3 * D_MODEL])
    za, ga, zq, zk, zv, zf, gb, zc, gc, zg = [slice(int(offs[i]), int(offs[i + 1])) for i in range(10)]
    scale = LOG2E / math.sqrt(HEAD_DIM)
    fpad = LANES - N_HEADS
    b = b_in
    w = lax.optimization_barrier(w_in.astype(BF16))
    wq = (w_in[..., zq] * scale).astype(BF16)
    w1 = jnp.concatenate([w[..., za], w[..., zc], jnp.pad(w[..., zf], ((0, 0), (0, 0), (0, fpad))),
                          wq, w[..., zk], w[..., zv]], axis=-1)
    b1 = jnp.concatenate([b[:, za], b[:, zc], jnp.pad(b[:, zf], ((0, 0), (0, fpad))),
                          b[:, zq] * scale, b[:, zk], b[:, zv]], axis=-1)
    w3 = jnp.concatenate([w[..., ga], w[..., gc], w[..., gb], w[..., zg]], axis=-1)
    b3 = jnp.concatenate([b[:, ga], b[:, gc], b[:, gb], b[:, zg]], axis=-1)
    row = lambda a: a.reshape(nl, 1, -1).astype(F32)
    tile_nb = lambda a: jnp.broadcast_to(a.reshape(nl, 1, N_STATE), (nl, nb, N_STATE))
    return {
        "w1": w1, "b1": row(b1),
        "pw": _block_diag(pool_w).astype(BF16), "ps": row(pool_scale),
        "are": tile_nb(abr), "aim": tile_nb(abi),
        "bbr": _block_diag(bbr).astype(BF16),
        "bbi": _block_diag(bbi).astype(BF16),
        "cre": _block_diag(jnp.swapaxes(c_re, 2, 3)).astype(BF16),
        "cim": _block_diag(jnp.swapaxes(c_im, 2, 3)).astype(BF16),
        "d": row(ssm_d), "wglu": w_glu.astype(BF16), "bglu": row(b_glu),
        "w3": w3, "b3": row(b3),
        "wa": w_up_a.astype(BF16), "wb": w_up_b.astype(BF16), "wc": w_up_c.astype(BF16),
        "wo": w_out.astype(BF16), "lng": row(ln_g), "lnb": row(ln_b),
    }


def _trunk(x, w_in, b_in, pool_w, pool_scale, ssm_a_re, ssm_a_im, ssm_log_dt, ssm_b_re, ssm_b_im,
           ssm_c_re, ssm_c_im, ssm_d, w_glu, b_glu, w_up_a, w_up_b, w_up_c, w_out, ln_g, ln_b,
           *, tt, tq, tm, hp=8, alpha=None):
    nb, seq, _ = x.shape
    depth = w_in.shape[0]
    if alpha is None:
        alpha = (2.0 * depth) ** 0.25
    abr, abi, bbr, bbi = _ssm_prep(ssm_a_re, ssm_a_im, ssm_log_dt, ssm_b_re, ssm_b_im)
    p = _stacked_params(nb, w_in, b_in, pool_w, pool_scale, abr, abi, bbr, bbi, ssm_c_re, ssm_c_im,
                        ssm_d, w_glu, b_glu, w_up_a, w_up_b, w_up_c, w_out, ln_g, ln_b)
    sel = _aug_selector()
    h = x
    for l in range(depth):
        ya, yc, q, k, v = _mix_in(h, p, sel, l, tt)
        yb = _attention(q, k, v, tq, hp)
        n = nb * seq
        h = _merge_out(h.reshape(n, D_MODEL), ya.reshape(n, D_A), yb.reshape(n, D_B),
                       yc.reshape(n, D_C), p, l, tm, alpha).reshape(nb, seq, D_MODEL)
    return h


def kernel(x, w_in, b_in, pool_w, pool_scale, ssm_a_re, ssm_a_im, ssm_log_dt, ssm_b_re, ssm_b_im,
           ssm_c_re, ssm_c_im, ssm_d, w_glu, b_glu, w_up_a, w_up_b, w_up_c, w_out, ln_g, ln_b):
    return _trunk(x, w_in, b_in, pool_w, pool_scale, ssm_a_re, ssm_a_im, ssm_log_dt, ssm_b_re,
                  ssm_b_im, ssm_c_re, ssm_c_im, ssm_d, w_glu, b_glu, w_up_a, w_up_b, w_up_c, w_out,
                  ln_g, ln_b, tt=64, tq=512, tm=1024)
```

```python
import functools
import math

import jax
import jax.numpy as jnp
import numpy as np
from jax import lax
from jax.experimental import pallas as pl
from jax.experimental.pallas import tpu as pltpu

F32 = jnp.float32
BF16 = jnp.bfloat16

D_MODEL = 1024
N_HEADS = 8
HEAD_DIM = 64
D_A = 256
D_B = N_HEADS * HEAD_DIM
D_C = 256
POOL_WINDOWS = (2, 4, 8, 16)
POOL_GROUP = 64
SSM_GROUPS = 16
SSM_GROUP = 16
SSM_STATE = 64
N_STATE = SSM_GROUPS * SSM_STATE
LN_EPS = 1e-5
LANES = 128
SUBLANES = 8
NEG_BIG = -1e30
LOG2E = math.log2(math.e)
NEXT_BLOCK_LEAD = 1
MERGE_SUB_ROWS = 1024
V_ROWS = 80

ZA_OFF, ZC_OFF, ZF_OFF, Q_OFF, K_OFF, V_OFF = 0, 256, 512, 640, 1152, 1664
C1 = 2176
C3 = 4096
AUG = HEAD_DIM
POOL_HISTORY = 16

VMEM_LIMIT = 56 * 1024 * 1024


def _sigmoid(x):
    return 0.5 * (jnp.tanh(0.5 * x) + 1.0)


def _silu(x):
    return x * _sigmoid(x)


def _gelu_tanh(x):
    return 0.5 * x * (1.0 + jnp.tanh(math.sqrt(2.0 / math.pi) * (x + 0.044715 * (x * x * x))))


def _log_sigmoid(x):
    return jnp.minimum(x, 0.0) - jnp.log(1.0 + jnp.exp(-jnp.abs(x)))


def _ssm_prep_body(are_ref, aim_ref, ldt_ref, bre_ref, bim_ref, abr_ref, abi_ref, bbr_ref, bbi_ref):
    are = are_ref[...]
    aim = aim_ref[...]
    dt = jnp.exp(ldt_ref[...])
    mag = jnp.exp(are * dt)
    ang = aim * dt
    abr = mag * jnp.cos(ang)
    abi = mag * jnp.sin(ang)
    den = are * are + aim * aim
    nr = abr - 1.0
    cre = (nr * are + abi * aim) / den
    cim = (abi * are - nr * aim) / den
    bre = bre_ref[...]
    bim = bim_ref[...]
    abr_ref[...] = abr
    abi_ref[...] = abi
    bbr_ref[...] = cre * bre - cim * bim
    bbi_ref[...] = cre * bim + cim * bre


def _ssm_prep(a_re, a_im, log_dt, b_re, b_im):
    L = a_re.shape[0]
    rows = L * SSM_GROUPS
    wide = SSM_GROUP * SSM_STATE
    tile = lambda a: jnp.tile(a.reshape(rows, SSM_STATE), (1, SSM_GROUP))
    ldt = jnp.broadcast_to(log_dt.reshape(rows, 1), (rows, wide))
    bt = lambda b: jnp.swapaxes(b, 2, 3).reshape(rows, wide)
    outs = pl.pallas_call(
        _ssm_prep_body,
        name="ssm_prep",
        out_shape=[jax.ShapeDtypeStruct((rows, wide), F32)] * 4,
    )(tile(a_re), tile(a_im), ldt, bt(b_re), bt(b_im))
    abr, abi, bbr, bbi = outs
    abr = abr[:, :SSM_STATE].reshape(L, SSM_GROUPS, SSM_STATE)
    abi = abi[:, :SSM_STATE].reshape(L, SSM_GROUPS, SSM_STATE)
    bbr = bbr.reshape(L, SSM_GROUPS, SSM_GROUP, SSM_STATE)
    bbi = bbi.reshape(L, SSM_GROUPS, SSM_GROUP, SSM_STATE)
    return abr, abi, bbr, bbi


def _mix_in_body(x_ref, w1_ref, b1_ref, pw_ref, ps_ref, are_ref, aim_ref, bbr_ref, bbi_ref,
                 cre_ref, cim_ref, d_ref, wglu_ref, bglu_ref, sel_ref,
                 ya_ref, yc_ref, q_ref, k_ref, v_ref,
                 xb_scr, z_scr, zc_scr, ext_scr, lv2_scr, lv4_scr, lv8_scr, slin_scr, slout_scr, utm_scr,
                 xr_scr, xi_scr, stre_scr, stim_scr, stf_scr, *, tt, nb):
    rows = nb * tt
    pitch = tt + 4
    hrows = POOL_HISTORY * nb
    step = pl.program_id(0)

    @pl.when(step == 0)
    def _():
        ext_scr[0:hrows, :] = jnp.zeros((hrows, D_A), F32)
        stre_scr[...] = jnp.zeros_like(stre_scr)
        stim_scr[...] = jnp.zeros_like(stim_scr)
        stf_scr[...] = jnp.zeros_like(stf_scr)

    xb_scr[...] = x_ref[...].reshape(rows, D_MODEL).astype(BF16)
    n_chunks = (C1 - Q_OFF) // 256

    def project_chunk(chunk):
        c0 = Q_OFF + chunk * 256
        zc_scr[chunk] = (jnp.dot(xb_scr[...], w1_ref[:, c0:c0 + 256],
                                 preferred_element_type=F32) + b1_ref[:, c0:c0 + 256])

    for c0 in range(0, Q_OFF, 256):
        c1 = min(c0 + 256, Q_OFF)
        z_scr[:, c0:c1] = (jnp.dot(xb_scr[...], w1_ref[:, c0:c1], preferred_element_type=F32)
                           + b1_ref[:, c0:c1])
    project_chunk(0)

    for b in range(nb):
        r0 = b * tt
        dst = slice(b * pitch, b * pitch + tt)
        slin_scr[0, dst, :] = z_scr[r0:r0 + tt, ZC_OFF:ZC_OFF + LANES]
        slin_scr[1, dst, :] = z_scr[r0:r0 + tt, ZC_OFF + LANES:ZC_OFF + 2 * LANES]
        slin_scr[2, dst, :] = _log_sigmoid(z_scr[r0:r0 + tt, ZF_OFF:ZF_OFF + LANES])
        slin_scr[3, dst, :] = z_scr[r0:r0 + tt, ZA_OFF:ZA_OFF + LANES]
        slin_scr[4, dst, :] = z_scr[r0:r0 + tt, ZA_OFF + LANES:ZA_OFF + 2 * LANES]
    for t in range(tt):
        r = slice(t * nb, (t + 1) * nb)
        e = slice(hrows + t * nb, hrows + (t + 1) * nb)
        utm_scr[r, 0:LANES] = slin_scr[0, pl.ds(t, nb, stride=pitch), :]
        utm_scr[r, LANES:2 * LANES] = slin_scr[1, pl.ds(t, nb, stride=pitch), :]
        ext_scr[e, 0:LANES] = slin_scr[3, pl.ds(t, nb, stride=pitch), :]
        ext_scr[e, LANES:2 * LANES] = slin_scr[4, pl.ds(t, nb, stride=pitch), :]
    project_chunk(1)

    ext_len = rows + hrows
    lv2_scr[nb:ext_len, :] = ext_scr[nb:ext_len, :] + ext_scr[0:ext_len - nb, :]
    lv4_scr[3 * nb:ext_len, :] = lv2_scr[3 * nb:ext_len, :] + lv2_scr[nb:ext_len - 2 * nb, :]
    lv8_scr[7 * nb:ext_len, :] = lv4_scr[7 * nb:ext_len, :] + lv4_scr[3 * nb:ext_len - 4 * nb, :]
    s16 = lv8_scr[hrows:ext_len, :] + lv8_scr[hrows - 8 * nb:ext_len - 8 * nb, :]
    lane_a = lax.broadcasted_iota(jnp.int32, (rows, D_A), 1)
    time_a = step * tt + lax.broadcasted_iota(jnp.int32, (tt, nb, D_A), 0).reshape(rows, D_A)
    win = jnp.where(lane_a < 64, 2, jnp.where(lane_a < 128, 4, jnp.where(lane_a < 192, 8, 16)))
    cnt = jnp.minimum(time_a + 1, win).astype(F32)
    wsum = jnp.where(lane_a < 64, lv2_scr[hrows:ext_len, :],
                     jnp.where(lane_a < 128, lv4_scr[hrows:ext_len, :],
                               jnp.where(lane_a < 192, lv8_scr[hrows:ext_len, :], s16)))
    pooled = wsum / cnt - ext_scr[hrows:ext_len, :]
    ya = jnp.dot(pooled.astype(BF16), pw_ref[...], preferred_element_type=F32) * ps_ref[...]
    slout_scr[3] = ya[:, 0:LANES]
    slout_scr[4] = ya[:, LANES:2 * LANES]
    ext_scr[0:hrows, :] = ext_scr[rows:ext_len, :]

    u_tm = utm_scr[...]
    ub = u_tm.astype(BF16)
    xr_scr[...] = jnp.dot(ub, bbr_ref[...], preferred_element_type=F32)
    xi_scr[...] = jnp.dot(ub, bbi_ref[...], preferred_element_type=F32)

    ar = are_ref[...]
    ai = aim_ref[...]
    sr, si, f = stre_scr[...], stim_scr[...], stf_scr[...]
    late_chunks = n_chunks - 2
    for t in range(tt):
        if t % (tt // late_chunks) == 0 and t // (tt // late_chunks) < late_chunks:
            project_chunk(2 + t // (tt // late_chunks))
        r = slice(t * nb, (t + 1) * nb)
        nr = ar * sr - ai * si + xr_scr[r, :]
        ni = ar * si + ai * sr + xi_scr[r, :]
        xr_scr[r, :] = nr
        xi_scr[r, :] = ni
        f = f + slin_scr[2, pl.ds(t, nb, stride=pitch), :]
        slout_scr[2, r, :] = f
        sr, si = nr, ni
    stre_scr[...] = sr
    stim_scr[...] = si
    stf_scr[...] = f

    lane_r = lax.broadcasted_iota(jnp.int32, (rows, LANES), 1)
    f_tm = slout_scr[2] * LOG2E
    hi = f_tm.astype(BF16).astype(F32)
    r1 = f_tm - hi
    mid = r1.astype(BF16).astype(F32)
    lo = r1 - mid
    fa = jnp.where(lane_r < 8, hi,
                   jnp.where(lane_r < 16, pltpu.roll(mid, 8, axis=1),
                             jnp.where(lane_r < 24, pltpu.roll(lo, 16, axis=1), 0.0))).astype(BF16)
    slout_scr[2] = jnp.dot(fa, sel_ref[...], preferred_element_type=F32)

    lane = lax.broadcasted_iota(jnp.int32, (tt, LANES), 1)
    in_lo = lane < AUG + 3
    in_hi = (lane >= AUG + 3) & (lane < AUG + 6)
    ones_lo = jnp.where((lane >= AUG) & in_lo, 1.0, 0.0)
    ones_hi = jnp.where(in_hi, 1.0, 0.0)
    out_refs = (q_ref, k_ref)

    def emit_chunk(chunk):
        kind, half = divmod(chunk, 2)
        for b in range(nb):
            if kind == 2:
                for pr in range(2):
                    v_ref[b, 2 * half + pr] = zc_scr[
                        chunk, b * tt:(b + 1) * tt, pr * LANES:(pr + 1) * LANES].astype(BF16)
                continue
            fg = slout_scr[2, pl.ds(b, tt, stride=nb), :]
            for hl in range(4):
                h = 4 * half + hl
                c = (hl // 2) * LANES
                z = zc_scr[chunk, b * tt:(b + 1) * tt, c:c + LANES]
                if hl % 2:
                    z = pltpu.roll(z, HEAD_DIM, axis=1)
                if kind == 0:
                    aug = jnp.where(in_lo, pltpu.roll(fg, AUG - 8 * h, axis=1), ones_hi)
                else:
                    aug = jnp.where(in_hi, pltpu.roll(fg, AUG - 8 * h, axis=1), ones_lo)
                out_refs[kind][b, h] = jnp.where(lane < AUG, z, aug).astype(BF16)

    emit_chunk(0)
    y = jnp.dot(xr_scr[...].astype(BF16), cre_ref[...], preferred_element_type=F32)
    emit_chunk(1)
    y = y - jnp.dot(xi_scr[...].astype(BF16), cim_ref[...], preferred_element_type=F32)
    emit_chunk(2)
    y = _gelu_tanh(y + d_ref[...] * u_tm)
    glu = jnp.dot(y.astype(BF16), wglu_ref[...], preferred_element_type=F32) + bglu_ref[...]
    emit_chunk(3)
    y = y * _sigmoid(glu)
    slout_scr[0] = y[:, 0:LANES]
    slout_scr[1] = y[:, LANES:2 * LANES]
    emit_chunk(4)
    emit_chunk(5)

    for b in range(nb):
        ya_ref[b, :, 0:LANES] = slout_scr[3, pl.ds(b, tt, stride=nb), :]
        ya_ref[b, :, LANES:2 * LANES] = slout_scr[4, pl.ds(b, tt, stride=nb), :]
        yc_ref[b, :, 0:LANES] = slout_scr[0, pl.ds(b, tt, stride=nb), :]
        yc_ref[b, :, LANES:2 * LANES] = slout_scr[1, pl.ds(b, tt, stride=nb), :]


def _const_spec(shape):
    nd = len(shape)
    return pl.BlockSpec(shape, lambda *_: (0,) * nd, pipeline_mode=pl.Buffered(1))


def _layer_spec(stacked, layer):
    nd = stacked.ndim - 1
    return pl.BlockSpec((None,) + stacked.shape[1:], lambda *_: (layer,) + (0,) * nd,
                        pipeline_mode=pl.Buffered(1))


def _mix_in(x, p, sel, layer, tt):
    nb, seq, _ = x.shape
    nt = seq // tt
    rows = nb * tt
    pitch = tt + 4
    hrows = POOL_HISTORY * nb
    consts = [p[name] for name in ("w1", "b1", "pw", "ps", "are", "aim", "bbr", "bbi", "cre", "cim",
                                   "d", "wglu", "bglu")]
    in_specs = [pl.BlockSpec((nb, tt, D_MODEL), lambda i: (0, i, 0))]
    in_specs += [_layer_spec(c, layer) for c in consts] + [_const_spec(sel.shape)]
    out_shape = [
        jax.ShapeDtypeStruct((nb, seq, D_A), F32),
        jax.ShapeDtypeStruct((nb, seq, D_C), F32),
        jax.ShapeDtypeStruct((nb, N_HEADS, seq, LANES), BF16),
        jax.ShapeDtypeStruct((nb, N_HEADS, seq, LANES), BF16),
        jax.ShapeDtypeStruct((nb, N_HEADS // 2, seq, LANES), BF16),
    ]
    out_specs = [
        pl.BlockSpec((nb, tt, D_A), lambda i: (0, i, 0)),
        pl.BlockSpec((nb, tt, D_C), lambda i: (0, i, 0)),
        pl.BlockSpec((nb, N_HEADS, tt, LANES), lambda i: (0, 0, i, 0)),
        pl.BlockSpec((nb, N_HEADS, tt, LANES), lambda i: (0, 0, i, 0)),
        pl.BlockSpec((nb, N_HEADS // 2, tt, LANES), lambda i: (0, 0, i, 0)),
    ]
    scratch = [
        pltpu.VMEM((rows, D_MODEL), BF16),
        pltpu.VMEM((rows, Q_OFF), F32),
        pltpu.VMEM(((C1 - Q_OFF) // 256, rows, 256), F32),
        pltpu.VMEM((rows + hrows, D_A), F32),
        pltpu.VMEM((rows + hrows, D_A), F32),
        pltpu.VMEM((rows + hrows, D_A), F32),
        pltpu.VMEM((rows + hrows, D_A), F32),
        pltpu.VMEM((5, nb * pitch, LANES), F32),
        pltpu.VMEM((5, rows, LANES), F32),
        pltpu.VMEM((rows, D_C), F32),
        pltpu.VMEM((rows, N_STATE), F32),
        pltpu.VMEM((rows, N_STATE), F32),
        pltpu.VMEM((nb, N_STATE), F32),
        pltpu.VMEM((nb, N_STATE), F32),
        pltpu.VMEM((nb, LANES), F32),
    ]
    return pl.pallas_call(
        functools.partial(_mix_in_body, tt=tt, nb=nb),
        name="mix_in",
        grid=(nt,),
        in_specs=in_specs,
        out_specs=out_specs,
        out_shape=out_shape,
        scratch_shapes=scratch,
        compiler_params=pltpu.CompilerParams(
            dimension_semantics=("arbitrary",), vmem_limit_bytes=VMEM_LIMIT),
    )(x, *consts, sel)


def _attn_body(q_ref, k_ref, v_ref, o_ref, vt_scr, m_scr, acc_scr, st_scr, *, t, hp):
    i = pl.program_id(2)
    dims = (((1,), (1,)), ((), ()))

    for hh in range(hp):
        m_scr[hh] = jnp.full((1, t), NEG_BIG, F32)
        acc_scr[hh] = jnp.zeros((LANES, t), F32)

    key = lax.broadcasted_iota(jnp.int32, (t, t), 0)
    qry = lax.broadcasted_iota(jnp.int32, (t, t), 1)

    def scores(hh, j):
        rows = pl.ds(pl.multiple_of(j * t, t), t)
        return lax.dot_general(k_ref[0, hh, rows, :], q_ref[0, hh], dims,
                               preferred_element_type=F32)

    st_scr[0] = scores(0, 0)

    pad_rows = lax.broadcasted_iota(jnp.int32, (V_ROWS - HEAD_DIM, t), 0)
    ones_row = jnp.where(pad_rows == 0, 1.0, 0.0).astype(BF16)
    for pr in range(hp // 2):
        vt = v_ref[0, pr].astype(F32).T.astype(BF16)
        for hh in (2 * pr, 2 * pr + 1):
            vt_scr[hh, i, 0:HEAD_DIM, :] = vt[(hh % 2) * HEAD_DIM:(hh % 2 + 1) * HEAD_DIM]
            vt_scr[hh, i, HEAD_DIM:V_ROWS, :] = ones_row

    def read_scores(hh, masked):
        st = st_scr[hh]
        return jnp.where(key <= qry, st, NEG_BIG) if masked else st

    def block(j, masked):
        for hh in range(hp):
            if hh + 1 < hp:
                st_scr[hh + 1] = scores(hh + 1, j)
            if not masked and hh == hp - NEXT_BLOCK_LEAD:
                st_scr[0] = scores(0, j + 1)
            m_old = m_scr[hh]
            m_new = jnp.maximum(m_old, jnp.max(read_scores(hh, masked), axis=0, keepdims=True))
            p = jnp.exp2(read_scores(hh, masked) - m_new).astype(BF16)
            alpha = jnp.exp2(m_old - m_new)
            acc_scr[hh, 0:V_ROWS, :] = alpha * acc_scr[hh, 0:V_ROWS, :] + jnp.dot(
                vt_scr[hh, j], p, preferred_element_type=F32)
            m_scr[hh] = m_new

    def body(jj, carry):
        block(2 * jj, False)
        block(2 * jj + 1, False)
        return carry

    lax.fori_loop(0, i // 2, body, 0)

    @pl.when(i % 2 == 1)
    def _():
        block(i - 1, False)

    block(i, True)
    for pr in range(hp // 2):
        halves = []
        for hh in (2 * pr, 2 * pr + 1):
            inv = 1.0 / acc_scr[hh, HEAD_DIM:HEAD_DIM + 1, :]
            halves.append(acc_scr[hh, 0:HEAD_DIM, :] * inv)
        o_ref[0, :, pr * LANES:(pr + 1) * LANES] = jnp.concatenate(halves, axis=0).T


def _attention(q, k, v, t, hp):
    nb, nh, seq, _ = q.shape
    nq = seq // t
    return pl.pallas_call(
        functools.partial(_attn_body, t=t, hp=hp),
        name="fox_attention",
        grid=(nb, nh // hp, nq),
        in_specs=[
            pl.BlockSpec((1, hp, t, LANES), lambda b, h, i: (b, h, i, 0)),
            pl.BlockSpec((1, hp, seq, LANES), lambda b, h, i: (b, h, 0, 0)),
            pl.BlockSpec((1, hp // 2, t, LANES), lambda b, h, i: (b, h, i, 0)),
        ],
        out_specs=pl.BlockSpec((1, t, hp * HEAD_DIM), lambda b, h, i: (b, i, h)),
        out_shape=jax.ShapeDtypeStruct((nb, seq, D_B), F32),
        scratch_shapes=[
            pltpu.VMEM((hp, seq // t, V_ROWS, t), BF16),
            pltpu.VMEM((hp, 1, t), F32),
            pltpu.VMEM((hp, LANES, t), F32),
            pltpu.VMEM((hp, t, t), F32),
        ],
        compiler_params=pltpu.CompilerParams(
            dimension_semantics=("parallel", "parallel", "arbitrary"),
            vmem_limit_bytes=VMEM_LIMIT),
    )(q, k, v)


def _merge_out_body(x_ref, ya_ref, yb_ref, yc_ref, w3_ref, b3_ref, wa_ref, wb_ref, wc_ref, wo_ref,
                    g_ref, beta_ref, o_ref, *, alpha, sub):
    for s0 in range(0, x_ref.shape[0], sub):
        rs = slice(s0, s0 + sub)
        x = x_ref[rs, :]
        xb = x.astype(BF16)

        def proj(c0, c1):
            return jnp.dot(xb, w3_ref[:, c0:c1], preferred_element_type=F32) + b3_ref[:, c0:c1]

        ya = (ya_ref[rs, :] * _silu(proj(0, 256))).astype(BF16)
        yc = (yc_ref[rs, :] * _silu(proj(256, 512))).astype(BF16)
        yb = (yb_ref[rs, :] * _silu(proj(512, 1024))).astype(BF16)
        merged = _sigmoid(proj(1024, 2048)) * jnp.dot(ya, wa_ref[...], preferred_element_type=F32)
        merged += _sigmoid(proj(2048, 3072)) * jnp.dot(yb, wb_ref[...], preferred_element_type=F32)
        merged += _sigmoid(proj(3072, 4096)) * jnp.dot(yc, wc_ref[...], preferred_element_type=F32)
        out = jnp.dot(merged.astype(BF16), wo_ref[...], preferred_element_type=F32)
        y = alpha * x + out
        mu = jnp.mean(y, axis=-1, keepdims=True)
        yc0 = y - mu
        var = jnp.mean(yc0 * yc0, axis=-1, keepdims=True)
        o_ref[rs, :] = yc0 * lax.rsqrt(var + LN_EPS) * g_ref[...] + beta_ref[...]


def _merge_out(x2, ya2, yb2, yc2, p, layer, tm, alpha):
    n = x2.shape[0]
    consts = [p[name] for name in ("w3", "b3", "wa", "wb", "wc", "wo", "lng", "lnb")]
    row_spec = lambda w: pl.BlockSpec((tm, w), lambda i: (i, 0))
    return pl.pallas_call(
        functools.partial(_merge_out_body, alpha=alpha, sub=min(tm, MERGE_SUB_ROWS)),
        name="merge_out",
        grid=(n // tm,),
        in_specs=[row_spec(D_MODEL), row_spec(D_A), row_spec(D_B), row_spec(D_C)]
        + [_layer_spec(c, layer) for c in consts],
        out_specs=row_spec(D_MODEL),
        out_shape=jax.ShapeDtypeStruct((n, D_MODEL), F32),
        compiler_params=pltpu.CompilerParams(
            dimension_semantics=("parallel",), vmem_limit_bytes=VMEM_LIMIT),
    )(x2, ya2, yb2, yc2, *consts)


def _aug_selector():
    sel = np.zeros((LANES, LANES), np.float32)
    for h in range(N_HEADS):
        for part in range(3):
            sel[part * 8 + h, 8 * h + part] = 1.0
            sel[part * 8 + h, 8 * h + 3 + part] = -1.0
    return jnp.asarray(sel, BF16)


def _block_diag(blocks):
    nl, g, r, c = blocks.shape
    eye = jnp.eye(g, dtype=blocks.dtype)
    return jnp.einsum("lgrc,gk->lgrkc", blocks, eye).reshape(nl, g * r, g * c)


def _stacked_params(nb, w_in, b_in, pool_w, pool_scale, abr, abi, bbr, bbi, c_re, c_im, ssm_d,
                    w_glu, b_glu, w_up_a, w_up_b, w_up_c, w_out, ln_g, ln_b):
    nl = w_in.shape[0]
    offs = np.cumsum([0, D_A, D_A, D_B, D_B, D_B, N_HEADS, D_B, D_C, D_C, 3 * D_MODEL])
    za, ga, zq, zk, zv, zf, gb, zc, gc, zg = [slice(int(offs[i]), int(offs[i + 1])) for i in range(10)]
    scale = LOG2E / math.sqrt(HEAD_DIM)
    fpad = LANES - N_HEADS
    b = b_in
    w = lax.optimization_barrier(w_in.astype(BF16))
    wq = (w_in[..., zq] * scale).astype(BF16)
    w1 = jnp.concatenate([w[..., za], w[..., zc], jnp.pad(w[..., zf], ((0, 0), (0, 0), (0, fpad))),
                          wq, w[..., zk], w[..., zv]], axis=-1)
    b1 = jnp.concatenate([b[:, za], b[:, zc], jnp.pad(b[:, zf], ((0, 0), (0, fpad))),
                          b[:, zq] * scale, b[:, zk], b[:, zv]], axis=-1)
    w3 = jnp.concatenate([w[..., ga], w[..., gc], w[..., gb], w[..., zg]], axis=-1)
    b3 = jnp.concatenate([b[:, ga], b[:, gc], b[:, gb], b[:, zg]], axis=-1)
    row = lambda a: a.reshape(nl, 1, -1).astype(F32)
    tile_nb = lambda a: jnp.broadcast_to(a.reshape(nl, 1, N_STATE), (nl, nb, N_STATE))
    return {
        "w1": w1, "b1": row(b1),
        "pw": _block_diag(pool_w).astype(BF16), "ps": row(pool_scale),
        "are": tile_nb(abr), "aim": tile_nb(abi),
        "bbr": _block_diag(bbr).astype(BF16),
        "bbi": _block_diag(bbi).astype(BF16),
        "cre": _block_diag(jnp.swapaxes(c_re, 2, 3)).astype(BF16),
        "cim": _block_diag(jnp.swapaxes(c_im, 2, 3)).astype(BF16),
        "d": row(ssm_d), "wglu": w_glu.astype(BF16), "bglu": row(b_glu),
        "w3": w3, "b3": row(b3),
        "wa": w_up_a.astype(BF16), "wb": w_up_b.astype(BF16), "wc": w_up_c.astype(BF16),
        "wo": w_out.astype(BF16), "lng": row(ln_g), "lnb": row(ln_b),
    }


def _trunk(x, w_in, b_in, pool_w, pool_scale, ssm_a_re, ssm_a_im, ssm_log_dt, ssm_b_re, ssm_b_im,
           ssm_c_re, ssm_c_im, ssm_d, w_glu, b_glu, w_up_a, w_up_b, w_up_c, w_out, ln_g, ln_b,
           *, tt, tq, tm, hp=8, alpha=None):
    nb, seq, _ = x.shape
    depth = w_in.shape[0]
    if alpha is None:
        alpha = (2.0 * depth) ** 0.25
    abr, abi, bbr, bbi = _ssm_prep(ssm_a_re, ssm_a_im, ssm_log_dt, ssm_b_re, ssm_b_im)
    p = _stacked_params(nb, w_in, b_in, pool_w, pool_scale, abr, abi, bbr, bbi, ssm_c_re, ssm_c_im,
                        ssm_d, w_glu, b_glu, w_up_a, w_up_b, w_up_c, w_out, ln_g, ln_b)
    sel = _aug_selector()
    h = x
    for l in range(depth):
        ya, yc, q, k, v = _mix_in(h, p, sel, l, tt)
        yb = _attention(q, k, v, tq, hp)
        n = nb * seq
        h = _merge_out(h.reshape(n, D_MODEL), ya.reshape(n, D_A), yb.reshape(n, D_B),
                       yc.reshape(n, D_C), p, l, tm, alpha).reshape(nb, seq, D_MODEL)
    return h


def kernel(x, w_in, b_in, pool_w, pool_scale, ssm_a_re, ssm_a_im, ssm_log_dt, ssm_b_re, ssm_b_im,
           ssm_c_re, ssm_c_im, ssm_d, w_glu, b_glu, w_up_a, w_up_b, w_up_c, w_out, ln_g, ln_b):
    return _trunk(x, w_in, b_in, pool_w, pool_scale, ssm_a_re, ssm_a_im, ssm_log_dt, ssm_b_re,
                  ssm_b_im, ssm_c_re, ssm_c_im, ssm_d, w_glu, b_glu, w_up_a, w_up_b, w_up_c, w_out,
                  ln_g, ln_b, tt=64, tq=512, tm=1024)
```

```python
import functools
import math

import jax
import jax.numpy as jnp
import numpy as np
from jax import lax
from jax.experimental import pallas as pl
from jax.experimental.pallas import tpu as pltpu

F32 = jnp.float32
BF16 = jnp.bfloat16

D_MODEL = 1024
N_HEADS = 8
HEAD_DIM = 64
D_A = 256
D_B = N_HEADS * HEAD_DIM
D_C = 256
POOL_WINDOWS = (2, 4, 8, 16)
POOL_GROUP = 64
SSM_GROUPS = 16
SSM_GROUP = 16
SSM_STATE = 64
N_STATE = SSM_GROUPS * SSM_STATE
LN_EPS = 1e-5
LANES = 128
SUBLANES = 8
NEG_BIG = -1e30
LOG2E = math.log2(math.e)
NEXT_BLOCK_LEAD = 1
MERGE_SUB_ROWS = 1024
V_ROWS = 80

ZA_OFF, ZC_OFF, ZF_OFF, Q_OFF, K_OFF, V_OFF = 0, 256, 512, 640, 1152, 1664
C1 = 2176
C3 = 4096
AUG = HEAD_DIM
POOL_HISTORY = 16

VMEM_LIMIT = 56 * 1024 * 1024


def _sigmoid(x):
    return 0.5 * (jnp.tanh(0.5 * x) + 1.0)


def _silu(x):
    return x * _sigmoid(x)


def _gelu_tanh(x):
    return 0.5 * x * (1.0 + jnp.tanh(math.sqrt(2.0 / math.pi) * (x + 0.044715 * (x * x * x))))


def _log_sigmoid(x):
    return jnp.minimum(x, 0.0) - jnp.log(1.0 + jnp.exp(-jnp.abs(x)))


def _ssm_prep_body(are_ref, aim_ref, ldt_ref, bre_ref, bim_ref, abr_ref, abi_ref, bbr_ref, bbi_ref):
    are = are_ref[...]
    aim = aim_ref[...]
    dt = jnp.exp(ldt_ref[...])
    mag = jnp.exp(are * dt)
    ang = aim * dt
    abr = mag * jnp.cos(ang)
    abi = mag * jnp.sin(ang)
    den = are * are + aim * aim
    nr = abr - 1.0
    cre = (nr * are + abi * aim) / den
    cim = (abi * are - nr * aim) / den
    bre = bre_ref[...]
    bim = bim_ref[...]
    abr_ref[...] = abr
    abi_ref[...] = abi
    bbr_ref[...] = cre * bre - cim * bim
    bbi_ref[...] = cre * bim + cim * bre


def _ssm_prep(a_re, a_im, log_dt, b_re, b_im):
    L = a_re.shape[0]
    rows = L * SSM_GROUPS
    wide = SSM_GROUP * SSM_STATE
    tile = lambda a: jnp.tile(a.reshape(rows, SSM_STATE), (1, SSM_GROUP))
    ldt = jnp.broadcast_to(log_dt.reshape(rows, 1), (rows, wide))
    bt = lambda b: jnp.swapaxes(b, 2, 3).reshape(rows, wide)
    outs = pl.pallas_call(
        _ssm_prep_body,
        name="ssm_prep",
        out_shape=[jax.ShapeDtypeStruct((rows, wide), F32)] * 4,
    )(tile(a_re), tile(a_im), ldt, bt(b_re), bt(b_im))
    abr, abi, bbr, bbi = outs
    abr = abr[:, :SSM_STATE].reshape(L, SSM_GROUPS, SSM_STATE)
    abi = abi[:, :SSM_STATE].reshape(L, SSM_GROUPS, SSM_STATE)
    bbr = bbr.reshape(L, SSM_GROUPS, SSM_GROUP, SSM_STATE)
    bbi = bbi.reshape(L, SSM_GROUPS, SSM_GROUP, SSM_STATE)
    return abr, abi, bbr, bbi


def _mix_in_body(x_ref, w1_ref, b1_ref, pw_ref, ps_ref, are_ref, aim_ref, bbr_ref, bbi_ref,
                 cre_ref, cim_ref, d_ref, wglu_ref, bglu_ref, sel_ref,
                 ya_ref, yc_ref, q_ref, k_ref, v_ref,
                 xb_scr, z_scr, zc_scr, ext_scr, lv2_scr, lv4_scr, lv8_scr, slin_scr, slout_scr, utm_scr,
                 xr_scr, xi_scr, stre_scr, stim_scr, stf_scr, *, tt, nb):
    rows = nb * tt
    pitch = tt + 4
    hrows = POOL_HISTORY * nb
    step = pl.program_id(0)

    @pl.when(step == 0)
    def _():
        ext_scr[0:hrows, :] = jnp.zeros((hrows, D_A), F32)
        stre_scr[...] = jnp.zeros_like(stre_scr)
        stim_scr[...] = jnp.zeros_like(stim_scr)
        stf_scr[...] = jnp.zeros_like(stf_scr)

    xb_scr[...] = x_ref[...].reshape(rows, D_MODEL).astype(BF16)
    n_chunks = (C1 - Q_OFF) // 256

    def project_chunk(chunk):
        c0 = Q_OFF + chunk * 256
        zc_scr[chunk] = (jnp.dot(xb_scr[...], w1_ref[:, c0:c0 + 256],
                                 preferred_element_type=F32) + b1_ref[:, c0:c0 + 256])

    for c0 in range(0, Q_OFF, 256):
        c1 = min(c0 + 256, Q_OFF)
        z_scr[:, c0:c1] = (jnp.dot(xb_scr[...], w1_ref[:, c0:c1], preferred_element_type=F32)
                           + b1_ref[:, c0:c1])
    project_chunk(0)

    for b in range(nb):
        r0 = b * tt
        dst = slice(b * pitch, b * pitch + tt)
        slin_scr[0, dst, :] = z_scr[r0:r0 + tt, ZC_OFF:ZC_OFF + LANES]
        slin_scr[1, dst, :] = z_scr[r0:r0 + tt, ZC_OFF + LANES:ZC_OFF + 2 * LANES]
        slin_scr[2, dst, :] = _log_sigmoid(z_scr[r0:r0 + tt, ZF_OFF:ZF_OFF + LANES])
        slin_scr[3, dst, :] = z_scr[r0:r0 + tt, ZA_OFF:ZA_OFF + LANES]
        slin_scr[4, dst, :] = z_scr[r0:r0 + tt, ZA_OFF + LANES:ZA_OFF + 2 * LANES]
    for t in range(tt):
        r = slice(t * nb, (t + 1) * nb)
        e = slice(hrows + t * nb, hrows + (t + 1) * nb)
        utm_scr[r, 0:LANES] = slin_scr[0, pl.ds(t, nb, stride=pitch), :]
        utm_scr[r, LANES:2 * LANES] = slin_scr[1, pl.ds(t, nb, stride=pitch), :]
        ext_scr[e, 0:LANES] = slin_scr[3, pl.ds(t, nb, stride=pitch), :]
        ext_scr[e, LANES:2 * LANES] = slin_scr[4, pl.ds(t, nb, stride=pitch), :]
    project_chunk(1)

    ext_len = rows + hrows
    lv2_scr[nb:ext_len, :] = ext_scr[nb:ext_len, :] + ext_scr[0:ext_len - nb, :]
    lv4_scr[3 * nb:ext_len, :] = lv2_scr[3 * nb:ext_len, :] + lv2_scr[nb:ext_len - 2 * nb, :]
    lv8_scr[7 * nb:ext_len, :] = lv4_scr[7 * nb:ext_len, :] + lv4_scr[3 * nb:ext_len - 4 * nb, :]
    s16 = lv8_scr[hrows:ext_len, :] + lv8_scr[hrows - 8 * nb:ext_len - 8 * nb, :]
    lane_a = lax.broadcasted_iota(jnp.int32, (rows, D_A), 1)
    time_a = step * tt + lax.broadcasted_iota(jnp.int32, (tt, nb, D_A), 0).reshape(rows, D_A)
    win = jnp.where(lane_a < 64, 2, jnp.where(lane_a < 128, 4, jnp.where(lane_a < 192, 8, 16)))
    cnt = jnp.minimum(time_a + 1, win).astype(F32)
    wsum = jnp.where(lane_a < 64, lv2_scr[hrows:ext_len, :],
                     jnp.where(lane_a < 128, lv4_scr[hrows:ext_len, :],
                               jnp.where(lane_a < 192, lv8_scr[hrows:ext_len, :], s16)))
    pooled = wsum / cnt - ext_scr[hrows:ext_len, :]
    ya = jnp.dot(pooled.astype(BF16), pw_ref[...], preferred_element_type=F32) * ps_ref[...]
    slout_scr[3] = ya[:, 0:LANES]
    slout_scr[4] = ya[:, LANES:2 * LANES]
    ext_scr[0:hrows, :] = ext_scr[rows:ext_len, :]

    u_tm = utm_scr[...]
    ub = u_tm.astype(BF16)
    xr_scr[...] = jnp.dot(ub, bbr_ref[...], preferred_element_type=F32)
    xi_scr[...] = jnp.dot(ub, bbi_ref[...], preferred_element_type=F32)

    ar = are_ref[...]
    ai = aim_ref[...]
    sr, si, f = stre_scr[...], stim_scr[...], stf_scr[...]
    late_chunks = n_chunks - 2
    for t in range(tt):
        if t % (tt // late_chunks) == 0 and t // (tt // late_chunks) < late_chunks:
            project_chunk(2 + t // (tt // late_chunks))
        r = slice(t * nb, (t + 1) * nb)
        nr = ar * sr - ai * si + xr_scr[r, :]
        ni = ar * si + ai * sr + xi_scr[r, :]
        xr_scr[r, :] = nr
        xi_scr[r, :] = ni
        f = f + slin_scr[2, pl.ds(t, nb, stride=pitch), :]
        slout_scr[2, r, :] = f
        sr, si = nr, ni
    stre_scr[...] = sr
    stim_scr[...] = si
    stf_scr[...] = f

    lane_r = lax.broadcasted_iota(jnp.int32, (rows, LANES), 1)
    f_tm = slout_scr[2] * LOG2E
    hi = f_tm.astype(BF16).astype(F32)
    r1 = f_tm - hi
    mid = r1.astype(BF16).astype(F32)
    lo = r1 - mid
    fa = jnp.where(lane_r < 8, hi,
                   jnp.where(lane_r < 16, pltpu.roll(mid, 8, axis=1),
                             jnp.where(lane_r < 24, pltpu.roll(lo, 16, axis=1), 0.0))).astype(BF16)
    slout_scr[2] = jnp.dot(fa, sel_ref[...], preferred_element_type=F32)

    lane = lax.broadcasted_iota(jnp.int32, (tt, LANES), 1)
    in_lo = lane < AUG + 3
    in_hi = (lane >= AUG + 3) & (lane < AUG + 6)
    ones_lo = jnp.where((lane >= AUG) & in_lo, 1.0, 0.0)
    ones_hi = jnp.where(in_hi, 1.0, 0.0)
    out_refs = (q_ref, k_ref)

    def emit_chunk(chunk):
        kind, half = divmod(chunk, 2)
        for b in range(nb):
            if kind == 2:
                for pr in range(2):
                    v_ref[b, 2 * half + pr] = zc_scr[
                        chunk, b * tt:(b + 1) * tt, pr * LANES:(pr + 1) * LANES].astype(BF16)
                continue
            fg = slout_scr[2, pl.ds(b, tt, stride=nb), :]
            for hl in range(4):
                h = 4 * half + hl
                c = (hl // 2) * LANES
                z = zc_scr[chunk, b * tt:(b + 1) * tt, c:c + LANES]
                if hl % 2:
                    z = pltpu.roll(z, HEAD_DIM, axis=1)
                if kind == 0:
                    aug = jnp.where(in_lo, pltpu.roll(fg, AUG - 8 * h, axis=1), ones_hi)
                else:
                    aug = jnp.where(in_hi, pltpu.roll(fg, AUG - 8 * h, axis=1), ones_lo)
                out_refs[kind][b, h] = jnp.where(lane < AUG, z, aug).astype(BF16)

    emit_chunk(0)
    y = jnp.dot(xr_scr[...].astype(BF16), cre_ref[...], preferred_element_type=F32)
    emit_chunk(1)
    y = y - jnp.dot(xi_scr[...].astype(BF16), cim_ref[...], preferred_element_type=F32)
    emit_chunk(2)
    y = _gelu_tanh(y + d_ref[...] * u_tm)
    glu = jnp.dot(y.astype(BF16), wglu_ref[...], preferred_element_type=F32) + bglu_ref[...]
    emit_chunk(3)
    y = y * _sigmoid(glu)
    slout_scr[0] = y[:, 0:LANES]
    slout_scr[1] = y[:, LANES:2 * LANES]
    emit_chunk(4)
    emit_chunk(5)

    for b in range(nb):
        ya_ref[b, :, 0:LANES] = slout_scr[3, pl.ds(b, tt, stride=nb), :]
        ya_ref[b, :, LANES:2 * LANES] = slout_scr[4, pl.ds(b, tt, stride=nb), :]
        yc_ref[b, :, 0:LANES] = slout_scr[0, pl.ds(b, tt, stride=nb), :]
        yc_ref[b, :, LANES:2 * LANES] = slout_scr[1, pl.ds(b, tt, stride=nb), :]


def _const_spec(shape):
    nd = len(shape)
    return pl.BlockSpec(shape, lambda *_: (0,) * nd, pipeline_mode=pl.Buffered(1))


def _layer_spec(stacked, layer):
    nd = stacked.ndim - 1
    return pl.BlockSpec((None,) + stacked.shape[1:], lambda *_: (layer,) + (0,) * nd,
                        pipeline_mode=pl.Buffered(1))


def _mix_in(x, p, sel, layer, tt):
    nb, seq, _ = x.shape
    nt = seq // tt
    rows = nb * tt
    pitch = tt + 4
    hrows = POOL_HISTORY * nb
    consts = [p[name] for name in ("w1", "b1", "pw", "ps", "are", "aim", "bbr", "bbi", "cre", "cim",
                                   "d", "wglu", "bglu")]
    in_specs = [pl.BlockSpec((nb, tt, D_MODEL), lambda i: (0, i, 0))]
    in_specs += [_layer_spec(c, layer) for c in consts] + [_const_spec(sel.shape)]
    out_shape = [
        jax.ShapeDtypeStruct((nb, seq, D_A), F32),
        jax.ShapeDtypeStruct((nb, seq, D_C), F32),
        jax.ShapeDtypeStruct((nb, N_HEADS, seq, LANES), BF16),
        jax.ShapeDtypeStruct((nb, N_HEADS, seq, LANES), BF16),
        jax.ShapeDtypeStruct((nb, N_HEADS // 2, seq, LANES), BF16),
    ]
    out_specs = [
        pl.BlockSpec((nb, tt, D_A), lambda i: (0, i, 0)),
        pl.BlockSpec((nb, tt, D_C), lambda i: (0, i, 0)),
        pl.BlockSpec((nb, N_HEADS, tt, LANES), lambda i: (0, 0, i, 0)),
        pl.BlockSpec((nb, N_HEADS, tt, LANES), lambda i: (0, 0, i, 0)),
        pl.BlockSpec((nb, N_HEADS // 2, tt, LANES), lambda i: (0, 0, i, 0)),
    ]
    scratch = [
        pltpu.VMEM((rows, D_MODEL), BF16),
        pltpu.VMEM((rows, Q_OFF), F32),
        pltpu.VMEM(((C1 - Q_OFF) // 256, rows, 256), F32),
        pltpu.VMEM((rows + hrows, D_A), F32),
        pltpu.VMEM((rows + hrows, D_A), F32),
        pltpu.VMEM((rows + hrows, D_A), F32),
        pltpu.VMEM((rows + hrows, D_A), F32),
        pltpu.VMEM((5, nb * pitch, LANES), F32),
        pltpu.VMEM((5, rows, LANES), F32),
        pltpu.VMEM((rows, D_C), F32),
        pltpu.VMEM((rows, N_STATE), F32),
        pltpu.VMEM((rows, N_STATE), F32),
        pltpu.VMEM((nb, N_STATE), F32),
        pltpu.VMEM((nb, N_STATE), F32),
        pltpu.VMEM((nb, LANES), F32),
    ]
    return pl.pallas_call(
        functools.partial(_mix_in_body, tt=tt, nb=nb),
        name="mix_in",
        grid=(nt,),
        in_specs=in_specs,
        out_specs=out_specs,
        out_shape=out_shape,
        scratch_shapes=scratch,
        compiler_params=pltpu.CompilerParams(
            dimension_semantics=("arbitrary",), vmem_limit_bytes=VMEM_LIMIT),
    )(x, *consts, sel)


def _attn_body(q_ref, k_ref, v_ref, o_ref, vt_scr, m_scr, acc_scr, st_scr, *, t, hp):
    i = pl.program_id(2)
    dims = (((1,), (1,)), ((), ()))

    for hh in range(hp):
        m_scr[hh] = jnp.full((1, t), NEG_BIG, F32)
        acc_scr[hh] = jnp.zeros((LANES, t), F32)

    key = lax.broadcasted_iota(jnp.int32, (t, t), 0)
    qry = lax.broadcasted_iota(jnp.int32, (t, t), 1)

    def scores(hh, j):
        rows = pl.ds(pl.multiple_of(j * t, t), t)
        return lax.dot_general(k_ref[0, hh, rows, :], q_ref[0, hh], dims,
                               preferred_element_type=F32)

    st_scr[0] = scores(0, 0)

    pad_rows = lax.broadcasted_iota(jnp.int32, (V_ROWS - HEAD_DIM, t), 0)
    ones_row = jnp.where(pad_rows == 0, 1.0, 0.0).astype(BF16)
    for pr in range(hp // 2):
        vt = v_ref[0, pr].astype(F32).T.astype(BF16)
        for hh in (2 * pr, 2 * pr + 1):
            vt_scr[hh, i, 0:HEAD_DIM, :] = vt[(hh % 2) * HEAD_DIM:(hh % 2 + 1) * HEAD_DIM]
            vt_scr[hh, i, HEAD_DIM:V_ROWS, :] = ones_row

    def read_scores(hh, masked):
        st = st_scr[hh]
        return jnp.where(key <= qry, st, NEG_BIG) if masked else st

    def block(j, masked):
        for hh in range(hp):
            if hh + 1 < hp:
                st_scr[hh + 1] = scores(hh + 1, j)
            if not masked and hh == hp - NEXT_BLOCK_LEAD:
                st_scr[0] = scores(0, j + 1)
            m_old = m_scr[hh]
            m_new = jnp.maximum(m_old, jnp.max(read_scores(hh, masked), axis=0, keepdims=True))
            p = jnp.exp2(read_scores(hh, masked) - m_new).astype(BF16)
            alpha = jnp.exp2(m_old - m_new)
            acc_scr[hh, 0:V_ROWS, :] = alpha * acc_scr[hh, 0:V_ROWS, :] + jnp.dot(
                vt_scr[hh, j], p, preferred_element_type=F32)
            m_scr[hh] = m_new

    def body(jj, carry):
        block(2 * jj, False)
        block(2 * jj + 1, False)
        return carry

    lax.fori_loop(0, i // 2, body, 0)

    @pl.when(i % 2 == 1)
    def _():
        block(i - 1, False)

    block(i, True)
    for pr in range(hp // 2):
        halves = []
        for hh in (2 * pr, 2 * pr + 1):
            inv = 1.0 / acc_scr[hh, HEAD_DIM:HEAD_DIM + 1, :]
            halves.append(acc_scr[hh, 0:HEAD_DIM, :] * inv)
        o_ref[0, :, pr * LANES:(pr + 1) * LANES] = jnp.concatenate(halves, axis=0).T


def _attention(q, k, v, t, hp):
    nb, nh, seq, _ = q.shape
    nq = seq // t
    return pl.pallas_call(
        functools.partial(_attn_body, t=t, hp=hp),
        name="fox_attention",
        grid=(nb, nh // hp, nq),
        in_specs=[
            pl.BlockSpec((1, hp, t, LANES), lambda b, h, i: (b, h, i, 0)),
            pl.BlockSpec((1, hp, seq, LANES), lambda b, h, i: (b, h, 0, 0)),
            pl.BlockSpec((1, hp // 2, t, LANES), lambda b, h, i: (b, h, i, 0)),
        ],
        out_specs=pl.BlockSpec((1, t, hp * HEAD_DIM), lambda b, h, i: (b, i, h)),
        out_shape=jax.ShapeDtypeStruct((nb, seq, D_B), F32),
        scratch_shapes=[
            pltpu.VMEM((hp, seq // t, V_ROWS, t), BF16),
            pltpu.VMEM((hp, 1, t), F32),
            pltpu.VMEM((hp, LANES, t), F32),
            pltpu.VMEM((hp, t, t), F32),
        ],
        compiler_params=pltpu.CompilerParams(
            dimension_semantics=("parallel", "parallel", "arbitrary"),
            vmem_limit_bytes=VMEM_LIMIT),
    )(q, k, v)


def _merge_out_body(x_ref, ya_ref, yb_ref, yc_ref, w3_ref, b3_ref, wa_ref, wb_ref, wc_ref, wo_ref,
                    g_ref, beta_ref, o_ref, *, alpha, sub):
    for s0 in range(0, x_ref.shape[0], sub):
        rs = slice(s0, s0 + sub)
        x = x_ref[rs, :]
        xb = x.astype(BF16)

        def proj(c0, c1):
            return jnp.dot(xb, w3_ref[:, c0:c1], preferred_element_type=F32) + b3_ref[:, c0:c1]

        ya = (ya_ref[rs, :] * _silu(proj(0, 256))).astype(BF16)
        yc = (yc_ref[rs, :] * _silu(proj(256, 512))).astype(BF16)
        yb = (yb_ref[rs, :] * _silu(proj(512, 1024))).astype(BF16)
        merged = _sigmoid(proj(1024, 2048)) * jnp.dot(ya, wa_ref[...], preferred_element_type=F32)
        merged += _sigmoid(proj(2048, 3072)) * jnp.dot(yb, wb_ref[...], preferred_element_type=F32)
        merged += _sigmoid(proj(3072, 4096)) * jnp.dot(yc, wc_ref[...], preferred_element_type=F32)
        out = jnp.dot(merged.astype(BF16), wo_ref[...], preferred_element_type=F32)
        y = alpha * x + out
        mu = jnp.mean(y, axis=-1, keepdims=True)
        yc0 = y - mu
        var = jnp.mean(yc0 * yc0, axis=-1, keepdims=True)
        o_ref[rs, :] = yc0 * lax.rsqrt(var + LN_EPS) * g_ref[...] + beta_ref[...]


def _merge_out(x2, ya2, yb2, yc2, p, layer, tm, alpha):
    n = x2.shape[0]
    consts = [p[name] for name in ("w3", "b3", "wa", "wb", "wc", "wo", "lng", "lnb")]
    row_spec = lambda w: pl.BlockSpec((tm, w), lambda i: (i, 0))
    return pl.pallas_call(
        functools.partial(_merge_out_body, alpha=alpha, sub=min(tm, MERGE_SUB_ROWS)),
        name="merge_out",
        grid=(n // tm,),
        in_specs=[row_spec(D_MODEL), row_spec(D_A), row_spec(D_B), row_spec(D_C)]
        + [_layer_spec(c, layer) for c in consts],
        out_specs=row_spec(D_MODEL),
        out_shape=jax.ShapeDtypeStruct((n, D_MODEL), F32),
        compiler_params=pltpu.CompilerParams(
            dimension_semantics=("parallel",), vmem_limit_bytes=VMEM_LIMIT),
    )(x2, ya2, yb2, yc2, *consts)


def _aug_selector():
    sel = np.zeros((LANES, LANES), np.float32)
    for h in range(N_HEADS):
        for part in range(3):
            sel[part * 8 + h, 8 * h + part] = 1.0
            sel[part * 8 + h, 8 * h + 3 + part] = -1.0
    return jnp.asarray(sel, BF16)


def _block_diag(blocks):
    nl, g, r, c = blocks.shape
    eye = jnp.eye(g, dtype=blocks.dtype)
    return jnp.einsum("lgrc,gk->lgrkc", blocks, eye).reshape(nl, g * r, g * c)


def _stacked_params(nb, w_in, b_in, pool_w, pool_scale, abr, abi, bbr, bbi, c_re, c_im, ssm_d,
                    w_glu, b_glu, w_up_a, w_up_b, w_up_c, w_out, ln_g, ln_b):
    nl = w_in.shape[0]
    offs = np.cumsum([0, D_A, D_A, D_B, D_B, D_B, N_HEADS, D_B, D_C, D_C, 3 * D_MODEL])
    za, ga, zq, zk, zv, zf, gb, zc, gc, zg = [slice(int(offs[i]), int(offs[i + 1])) for i in range(10)]
    scale = LOG2E / math.sqrt(HEAD_DIM)
    fpad = LANES - N_HEADS
    b = b_in
    w = lax.optimization_barrier(w_in.astype(BF16))
    wq = (w_in[..., zq] * scale).astype(BF16)
    w1 = jnp.concatenate([w[..., za], w[..., zc], jnp.pad(w[..., zf], ((0, 0), (0, 0), (0, fpad))),
                          wq, w[..., zk], w[..., zv]], axis=-1)
    b1 = jnp.concatenate([b[:, za], b[:, zc], jnp.pad(b[:, zf], ((0, 0), (0, fpad))),
                          b[:, zq] * scale, b[:, zk], b[:, zv]], axis=-1)
    w3 = jnp.concatenate([w[..., ga], w[..., gc], w[..., gb], w[..., zg]], axis=-1)
    b3 = jnp.concatenate([b[:, ga], b[:, gc], b[:, gb], b[:, zg]], axis=-1)
    row = lambda a: a.reshape(nl, 1, -1).astype(F32)
    tile_nb = lambda a: jnp.broadcast_to(a.reshape(nl, 1, N_STATE), (nl, nb, N_STATE))
    return {
        "w1": w1, "b1": row(b1),
        "pw": _block_diag(pool_w).astype(BF16), "ps": row(pool_scale),
        "are": tile_nb(abr), "aim": tile_nb(abi),
        "bbr": _block_diag(bbr).astype(BF16),
        "bbi": _block_diag(bbi).astype(BF16),
        "cre": _block_diag(jnp.swapaxes(c_re, 2, 3)).astype(BF16),
        "cim": _block_diag(jnp.swapaxes(c_im, 2, 3)).astype(BF16),
        "d": row(ssm_d), "wglu": w_glu.astype(BF16), "bglu": row(b_glu),
        "w3": w3, "b3": row(b3),
        "wa": w_up_a.astype(BF16), "wb": w_up_b.astype(BF16), "wc": w_up_c.astype(BF16),
        "wo": w_out.astype(BF16), "lng": row(ln_g), "lnb": row(ln_b),
    }


def _trunk(x, w_in, b_in, pool_w, pool_scale, ssm_a_re, ssm_a_im, ssm_log_dt, ssm_b_re, ssm_b_im,
           ssm_c_re, ssm_c_im, ssm_d, w_glu, b_glu, w_up_a, w_up_b, w_up_c, w_out, ln_g, ln_b,
           *, tt, tq, tm, hp=8, alpha=None):
    nb, seq, _ = x.shape
    depth = w_in.shape[0]
    if alpha is None:
        alpha = (2.0 * depth) ** 0.25
    abr, abi, bbr, bbi = _ssm_prep(ssm_a_re, ssm_a_im, ssm_log_dt, ssm_b_re, ssm_b_im)
    p = _stacked_params(nb, w_in, b_in, pool_w, pool_scale, abr, abi, bbr, bbi, ssm_c_re, ssm_c_im,
                        ssm_d, w_glu, b_glu, w_up_a, w_up_b, w_up_c, w_out, ln_g, ln_b)
    sel = _aug_selector()
    h = x
    for l in range(depth):
        ya, yc, q, k, v = _mix_in(h, p, sel, l, tt)
        yb = _attention(q, k, v, tq, hp)
        n = nb * seq
        h = _merge_out(h.reshape(n, D_MODEL), ya.reshape(n, D_A), yb.reshape(n, D_B),
                       yc.reshape(n, D_C), p, l, tm, alpha).reshape(nb, seq, D_MODEL)
    return h


def kernel(x, w_in, b_in, pool_w, pool_scale, ssm_a_re, ssm_a_im, ssm_log_dt, ssm_b_re, ssm_b_im,
           ssm_c_re, ssm_c_im, ssm_d, w_glu, b_glu, w_up_a, w_up_b, w_up_c, w_out, ln_g, ln_b):
    return _trunk(x, w_in, b_in, pool_w, pool_scale, ssm_a_re, ssm_a_im, ssm_log_dt, ssm_b_re,
                  ssm_b_im, ssm_c_re, ssm_c_im, ssm_d, w_glu, b_glu, w_up_a, w_up_b, w_up_c, w_out,
                  ln_g, ln_b, tt=128, tq=512, tm=1024)
```

```python
import functools
import math

import jax
import jax.numpy as jnp
import numpy as np
from jax import lax
from jax.experimental import pallas as pl
from jax.experimental.pallas import tpu as pltpu

F32 = jnp.float32
BF16 = jnp.bfloat16

D_MODEL = 1024
N_HEADS = 8
HEAD_DIM = 64
D_A = 256
D_B = N_HEADS * HEAD_DIM
D_C = 256
POOL_WINDOWS = (2, 4, 8, 16)
POOL_GROUP = 64
SSM_GROUPS = 16
SSM_GROUP = 16
SSM_STATE = 64
N_STATE = SSM_GROUPS * SSM_STATE
LN_EPS = 1e-5
LANES = 128
SUBLANES = 8
NEG_BIG = -1e30
LOG2E = math.log2(math.e)
NEXT_BLOCK_LEAD = 1
MERGE_SUB_ROWS = 1024
V_ROWS = 80

ZA_OFF, ZC_OFF, ZF_OFF, Q_OFF, K_OFF, V_OFF = 0, 256, 512, 640, 1152, 1664
C1 = 2176
G_GB, G_ZC, G_GC, G_ZG = 0, 512, 768, 1024
AUG = HEAD_DIM
POOL_HISTORY = 16

VMEM_LIMIT = 56 * 1024 * 1024


def _sigmoid(x):
    return 0.5 * (jnp.tanh(0.5 * x) + 1.0)


def _silu(x):
    return x * _sigmoid(x)


def _gelu_tanh(x):
    return 0.5 * x * (1.0 + jnp.tanh(math.sqrt(2.0 / math.pi) * (x + 0.044715 * (x * x * x))))


def _log_sigmoid(x):
    return jnp.minimum(x, 0.0) - jnp.log(1.0 + jnp.exp(-jnp.abs(x)))


def _ssm_prep_body(are_ref, aim_ref, ldt_ref, bre_ref, bim_ref, abr_ref, abi_ref, bbr_ref, bbi_ref):
    are = are_ref[...]
    aim = aim_ref[...]
    dt = jnp.exp(ldt_ref[...])
    mag = jnp.exp(are * dt)
    ang = aim * dt
    abr = mag * jnp.cos(ang)
    abi = mag * jnp.sin(ang)
    den = are * are + aim * aim
    nr = abr - 1.0
    cre = (nr * are + abi * aim) / den
    cim = (abi * are - nr * aim) / den
    bre = bre_ref[...]
    bim = bim_ref[...]
    abr_ref[...] = abr
    abi_ref[...] = abi
    bbr_ref[...] = cre * bre - cim * bim
    bbi_ref[...] = cre * bim + cim * bre


def _ssm_prep(a_re, a_im, log_dt, b_re, b_im):
    L = a_re.shape[0]
    rows = L * SSM_GROUPS
    wide = SSM_GROUP * SSM_STATE
    tile = lambda a: jnp.tile(a.reshape(rows, SSM_STATE), (1, SSM_GROUP))
    ldt = jnp.broadcast_to(log_dt.reshape(rows, 1), (rows, wide))
    bt = lambda b: jnp.swapaxes(b, 2, 3).reshape(rows, wide)
    outs = pl.pallas_call(
        _ssm_prep_body,
        name="ssm_prep",
        out_shape=[jax.ShapeDtypeStruct((rows, wide), F32)] * 4,
    )(tile(a_re), tile(a_im), ldt, bt(b_re), bt(b_im))
    abr, abi, bbr, bbi = outs
    abr = abr[:, :SSM_STATE].reshape(L, SSM_GROUPS, SSM_STATE)
    abi = abi[:, :SSM_STATE].reshape(L, SSM_GROUPS, SSM_STATE)
    bbr = bbr.reshape(L, SSM_GROUPS, SSM_GROUP, SSM_STATE)
    bbi = bbi.reshape(L, SSM_GROUPS, SSM_GROUP, SSM_STATE)
    return abr, abi, bbr, bbi


def _mix_in_body(x_ref, w1_ref, b1_ref, pw_ref, ps_ref, are_ref, aim_ref, bbr_ref, bbi_ref,
                 cre_ref, cim_ref, d_ref, wglu_ref, bglu_ref, sel_ref,
                 ya_ref, yc_ref, q_ref, k_ref, v_ref,
                 xb_scr, z_scr, zc_scr, ext_scr, lv2_scr, lv4_scr, lv8_scr, slin_scr, slout_scr, utm_scr,
                 xr_scr, xi_scr, stre_scr, stim_scr, stf_scr, *, tt, nb):
    rows = nb * tt
    pitch = tt + 4
    hrows = POOL_HISTORY * nb
    step = pl.program_id(0)

    @pl.when(step == 0)
    def _():
        ext_scr[0:hrows, :] = jnp.zeros((hrows, D_A), F32)
        stre_scr[...] = jnp.zeros_like(stre_scr)
        stim_scr[...] = jnp.zeros_like(stim_scr)
        stf_scr[...] = jnp.zeros_like(stf_scr)

    xb_scr[...] = x_ref[...].reshape(rows, D_MODEL).astype(BF16)
    n_chunks = (C1 - Q_OFF) // 256

    def project_chunk(chunk):
        c0 = Q_OFF + chunk * 256
        zc_scr[chunk] = (jnp.dot(xb_scr[...], w1_ref[:, c0:c0 + 256],
                                 preferred_element_type=F32) + b1_ref[:, c0:c0 + 256])

    for c0 in range(0, Q_OFF, 256):
        c1 = min(c0 + 256, Q_OFF)
        z_scr[:, c0:c1] = (jnp.dot(xb_scr[...], w1_ref[:, c0:c1], preferred_element_type=F32)
                           + b1_ref[:, c0:c1])
    project_chunk(0)

    for b in range(nb):
        r0 = b * tt
        dst = slice(b * pitch, b * pitch + tt)
        slin_scr[0, dst, :] = z_scr[r0:r0 + tt, ZC_OFF:ZC_OFF + LANES]
        slin_scr[1, dst, :] = z_scr[r0:r0 + tt, ZC_OFF + LANES:ZC_OFF + 2 * LANES]
        slin_scr[2, dst, :] = _log_sigmoid(z_scr[r0:r0 + tt, ZF_OFF:ZF_OFF + LANES])
        slin_scr[3, dst, :] = z_scr[r0:r0 + tt, ZA_OFF:ZA_OFF + LANES]
        slin_scr[4, dst, :] = z_scr[r0:r0 + tt, ZA_OFF + LANES:ZA_OFF + 2 * LANES]
    for t in range(tt):
        r = slice(t * nb, (t + 1) * nb)
        e = slice(hrows + t * nb, hrows + (t + 1) * nb)
        utm_scr[r, 0:LANES] = slin_scr[0, pl.ds(t, nb, stride=pitch), :]
        utm_scr[r, LANES:2 * LANES] = slin_scr[1, pl.ds(t, nb, stride=pitch), :]
        ext_scr[e, 0:LANES] = slin_scr[3, pl.ds(t, nb, stride=pitch), :]
        ext_scr[e, LANES:2 * LANES] = slin_scr[4, pl.ds(t, nb, stride=pitch), :]
    project_chunk(1)

    ext_len = rows + hrows
    lv2_scr[nb:ext_len, :] = ext_scr[nb:ext_len, :] + ext_scr[0:ext_len - nb, :]
    lv4_scr[3 * nb:ext_len, :] = lv2_scr[3 * nb:ext_len, :] + lv2_scr[nb:ext_len - 2 * nb, :]
    lv8_scr[7 * nb:ext_len, :] = lv4_scr[7 * nb:ext_len, :] + lv4_scr[3 * nb:ext_len - 4 * nb, :]
    s16 = lv8_scr[hrows:ext_len, :] + lv8_scr[hrows - 8 * nb:ext_len - 8 * nb, :]
    lane_a = lax.broadcasted_iota(jnp.int32, (rows, D_A), 1)
    time_a = step * tt + lax.broadcasted_iota(jnp.int32, (tt, nb, D_A), 0).reshape(rows, D_A)
    win = jnp.where(lane_a < 64, 2, jnp.where(lane_a < 128, 4, jnp.where(lane_a < 192, 8, 16)))
    cnt = jnp.minimum(time_a + 1, win).astype(F32)
    wsum = jnp.where(lane_a < 64, lv2_scr[hrows:ext_len, :],
                     jnp.where(lane_a < 128, lv4_scr[hrows:ext_len, :],
                               jnp.where(lane_a < 192, lv8_scr[hrows:ext_len, :], s16)))
    pooled = wsum / cnt - ext_scr[hrows:ext_len, :]
    ya = jnp.dot(pooled.astype(BF16), pw_ref[...], preferred_element_type=F32) * ps_ref[...]
    slout_scr[3] = ya[:, 0:LANES]
    slout_scr[4] = ya[:, LANES:2 * LANES]
    ext_scr[0:hrows, :] = ext_scr[rows:ext_len, :]

    u_tm = utm_scr[...]
    ub = u_tm.astype(BF16)
    xr_scr[...] = jnp.dot(ub, bbr_ref[...], preferred_element_type=F32)
    xi_scr[...] = jnp.dot(ub, bbi_ref[...], preferred_element_type=F32)

    ar = are_ref[...]
    ai = aim_ref[...]
    sr, si, f = stre_scr[...], stim_scr[...], stf_scr[...]
    late_chunks = n_chunks - 2
    for t in range(tt):
        if t % (tt // late_chunks) == 0 and t // (tt // late_chunks) < late_chunks:
            project_chunk(2 + t // (tt // late_chunks))
        r = slice(t * nb, (t + 1) * nb)
        nr = ar * sr - ai * si + xr_scr[r, :]
        ni = ar * si + ai * sr + xi_scr[r, :]
        xr_scr[r, :] = nr
        xi_scr[r, :] = ni
        f = f + slin_scr[2, pl.ds(t, nb, stride=pitch), :]
        slout_scr[2, r, :] = f
        sr, si = nr, ni
    stre_scr[...] = sr
    stim_scr[...] = si
    stf_scr[...] = f

    lane_r = lax.broadcasted_iota(jnp.int32, (rows, LANES), 1)
    f_tm = slout_scr[2] * LOG2E
    hi = f_tm.astype(BF16).astype(F32)
    r1 = f_tm - hi
    mid = r1.astype(BF16).astype(F32)
    lo = r1 - mid
    fa = jnp.where(lane_r < 8, hi,
                   jnp.where(lane_r < 16, pltpu.roll(mid, 8, axis=1),
                             jnp.where(lane_r < 24, pltpu.roll(lo, 16, axis=1), 0.0))).astype(BF16)
    slout_scr[2] = jnp.dot(fa, sel_ref[...], preferred_element_type=F32)

    lane = lax.broadcasted_iota(jnp.int32, (tt, LANES), 1)
    in_lo = lane < AUG + 3
    in_hi = (lane >= AUG + 3) & (lane < AUG + 6)
    ones_lo = jnp.where((lane >= AUG) & in_lo, 1.0, 0.0)
    ones_hi = jnp.where(in_hi, 1.0, 0.0)
    out_refs = (q_ref, k_ref)

    def emit_chunk(chunk):
        kind, half = divmod(chunk, 2)
        for b in range(nb):
            if kind == 2:
                for pr in range(2):
                    v_ref[b, 2 * half + pr] = zc_scr[
                        chunk, b * tt:(b + 1) * tt, pr * LANES:(pr + 1) * LANES].astype(BF16)
                continue
            fg = slout_scr[2, pl.ds(b, tt, stride=nb), :]
            for hl in range(4):
                h = 4 * half + hl
                c = (hl // 2) * LANES
                z = zc_scr[chunk, b * tt:(b + 1) * tt, c:c + LANES]
                if hl % 2:
                    z = pltpu.roll(z, HEAD_DIM, axis=1)
                if kind == 0:
                    aug = jnp.where(in_lo, pltpu.roll(fg, AUG - 8 * h, axis=1), ones_hi)
                else:
                    aug = jnp.where(in_hi, pltpu.roll(fg, AUG - 8 * h, axis=1), ones_lo)
                out_refs[kind][b, h] = jnp.where(lane < AUG, z, aug).astype(BF16)

    emit_chunk(0)
    y = jnp.dot(xr_scr[...].astype(BF16), cre_ref[...], preferred_element_type=F32)
    emit_chunk(1)
    y = y - jnp.dot(xi_scr[...].astype(BF16), cim_ref[...], preferred_element_type=F32)
    emit_chunk(2)
    y = _gelu_tanh(y + d_ref[...] * u_tm)
    glu = jnp.dot(y.astype(BF16), wglu_ref[...], preferred_element_type=F32) + bglu_ref[...]
    emit_chunk(3)
    y = y * _sigmoid(glu)
    slout_scr[0] = y[:, 0:LANES]
    slout_scr[1] = y[:, LANES:2 * LANES]
    emit_chunk(4)
    emit_chunk(5)

    for b in range(nb):
        ya_ref[b, :, 0:LANES] = slout_scr[3, pl.ds(b, tt, stride=nb), :]
        ya_ref[b, :, LANES:2 * LANES] = slout_scr[4, pl.ds(b, tt, stride=nb), :]
        yc_ref[b, :, 0:LANES] = slout_scr[0, pl.ds(b, tt, stride=nb), :]
        yc_ref[b, :, LANES:2 * LANES] = slout_scr[1, pl.ds(b, tt, stride=nb), :]


def _const_spec(shape):
    nd = len(shape)
    return pl.BlockSpec(shape, lambda *_: (0,) * nd, pipeline_mode=pl.Buffered(1))


def _layer_spec(stacked, layer):
    nd = stacked.ndim - 1
    return pl.BlockSpec((None,) + stacked.shape[1:], lambda *_: (layer,) + (0,) * nd,
                        pipeline_mode=pl.Buffered(1))


def _mix_in(x, p, sel, layer, tt):
    nb, seq, _ = x.shape
    nt = seq // tt
    rows = nb * tt
    pitch = tt + 4
    hrows = POOL_HISTORY * nb
    consts = [p[name] for name in ("w1", "b1", "pw", "ps", "are", "aim", "bbr", "bbi", "cre", "cim",
                                   "d", "wglu", "bglu")]
    in_specs = [pl.BlockSpec((nb, tt, D_MODEL), lambda i: (0, i, 0))]
    in_specs += [_layer_spec(c, layer) for c in consts] + [_const_spec(sel.shape)]
    out_shape = [
        jax.ShapeDtypeStruct((nb, seq, D_A), F32),
        jax.ShapeDtypeStruct((nb, seq, D_C), F32),
        jax.ShapeDtypeStruct((nb, N_HEADS, seq, LANES), BF16),
        jax.ShapeDtypeStruct((nb, N_HEADS, seq, LANES), BF16),
        jax.ShapeDtypeStruct((nb, N_HEADS // 2, seq, LANES), BF16),
    ]
    out_specs = [
        pl.BlockSpec((nb, tt, D_A), lambda i: (0, i, 0)),
        pl.BlockSpec((nb, tt, D_C), lambda i: (0, i, 0)),
        pl.BlockSpec((nb, N_HEADS, tt, LANES), lambda i: (0, 0, i, 0)),
        pl.BlockSpec((nb, N_HEADS, tt, LANES), lambda i: (0, 0, i, 0)),
        pl.BlockSpec((nb, N_HEADS // 2, tt, LANES), lambda i: (0, 0, i, 0)),
    ]
    scratch = [
        pltpu.VMEM((rows, D_MODEL), BF16),
        pltpu.VMEM((rows, Q_OFF), F32),
        pltpu.VMEM(((C1 - Q_OFF) // 256, rows, 256), F32),
        pltpu.VMEM((rows + hrows, D_A), F32),
        pltpu.VMEM((rows + hrows, D_A), F32),
        pltpu.VMEM((rows + hrows, D_A), F32),
        pltpu.VMEM((rows + hrows, D_A), F32),
        pltpu.VMEM((5, nb * pitch, LANES), F32),
        pltpu.VMEM((5, rows, LANES), F32),
        pltpu.VMEM((rows, D_C), F32),
        pltpu.VMEM((rows, N_STATE), F32),
        pltpu.VMEM((rows, N_STATE), F32),
        pltpu.VMEM((nb, N_STATE), F32),
        pltpu.VMEM((nb, N_STATE), F32),
        pltpu.VMEM((nb, LANES), F32),
    ]
    return pl.pallas_call(
        functools.partial(_mix_in_body, tt=tt, nb=nb),
        name="mix_in",
        grid=(nt,),
        in_specs=in_specs,
        out_specs=out_specs,
        out_shape=out_shape,
        scratch_shapes=scratch,
        compiler_params=pltpu.CompilerParams(
            dimension_semantics=("arbitrary",), vmem_limit_bytes=VMEM_LIMIT),
    )(x, *consts, sel)


def _attn_body(q_ref, k_ref, v_ref, o_ref, vt_scr, m_scr, acc_scr, st_scr, *, t, hp):
    i = pl.program_id(2)
    dims = (((1,), (1,)), ((), ()))

    for hh in range(hp):
        m_scr[hh] = jnp.full((1, t), NEG_BIG, F32)
        acc_scr[hh] = jnp.zeros((LANES, t), F32)

    key = lax.broadcasted_iota(jnp.int32, (t, t), 0)
    qry = lax.broadcasted_iota(jnp.int32, (t, t), 1)

    def scores(hh, j):
        rows = pl.ds(pl.multiple_of(j * t, t), t)
        return lax.dot_general(k_ref[0, hh, rows, :], q_ref[0, hh], dims,
                               preferred_element_type=F32)

    st_scr[0] = scores(0, 0)

    pad_rows = lax.broadcasted_iota(jnp.int32, (V_ROWS - HEAD_DIM, t), 0)
    ones_row = jnp.where(pad_rows == 0, 1.0, 0.0).astype(BF16)
    for pr in range(hp // 2):
        vt = v_ref[0, pr].astype(F32).T.astype(BF16)
        for hh in (2 * pr, 2 * pr + 1):
            vt_scr[hh, i, 0:HEAD_DIM, :] = vt[(hh % 2) * HEAD_DIM:(hh % 2 + 1) * HEAD_DIM]
            vt_scr[hh, i, HEAD_DIM:V_ROWS, :] = ones_row

    def read_scores(hh, masked):
        st = st_scr[hh]
        return jnp.where(key <= qry, st, NEG_BIG) if masked else st

    def block(j, masked):
        for hh in range(hp):
            if hh + 1 < hp:
                st_scr[hh + 1] = scores(hh + 1, j)
            if not masked and hh == hp - NEXT_BLOCK_LEAD:
                st_scr[0] = scores(0, j + 1)
            m_old = m_scr[hh]
            m_new = jnp.maximum(m_old, jnp.max(read_scores(hh, masked), axis=0, keepdims=True))
            p = jnp.exp2(read_scores(hh, masked) - m_new).astype(BF16)
            alpha = jnp.exp2(m_old - m_new)
            acc_scr[hh, 0:V_ROWS, :] = alpha * acc_scr[hh, 0:V_ROWS, :] + jnp.dot(
                vt_scr[hh, j], p, preferred_element_type=F32)
            m_scr[hh] = m_new

    def body(jj, carry):
        block(2 * jj, False)
        block(2 * jj + 1, False)
        return carry

    lax.fori_loop(0, i // 2, body, 0)

    @pl.when(i % 2 == 1)
    def _():
        block(i - 1, False)

    block(i, True)
    for pr in range(hp // 2):
        halves = []
        for hh in (2 * pr, 2 * pr + 1):
            inv = 1.0 / acc_scr[hh, HEAD_DIM:HEAD_DIM + 1, :]
            halves.append(acc_scr[hh, 0:HEAD_DIM, :] * inv)
        o_ref[0, :, pr * LANES:(pr + 1) * LANES] = jnp.concatenate(halves, axis=0).T


def _attention(q, k, v, t, hp):
    nb, nh, seq, _ = q.shape
    nq = seq // t
    return pl.pallas_call(
        functools.partial(_attn_body, t=t, hp=hp),
        name="fox_attention",
        grid=(nb, nh // hp, nq),
        in_specs=[
            pl.BlockSpec((1, hp, t, LANES), lambda b, h, i: (b, h, i, 0)),
            pl.BlockSpec((1, hp, seq, LANES), lambda b, h, i: (b, h, 0, 0)),
            pl.BlockSpec((1, hp // 2, t, LANES), lambda b, h, i: (b, h, i, 0)),
        ],
        out_specs=pl.BlockSpec((1, t, hp * HEAD_DIM), lambda b, h, i: (b, i, h)),
        out_shape=jax.ShapeDtypeStruct((nb, seq, D_B), F32),
        scratch_shapes=[
            pltpu.VMEM((hp, seq // t, V_ROWS, t), BF16),
            pltpu.VMEM((hp, 1, t), F32),
            pltpu.VMEM((hp, LANES, t), F32),
            pltpu.VMEM((hp, t, t), F32),
        ],
        compiler_params=pltpu.CompilerParams(
            dimension_semantics=("parallel", "parallel", "arbitrary"),
            vmem_limit_bytes=VMEM_LIMIT),
    )(q, k, v)


def _merge_out_body(x_ref, ya_ref, yb_ref, yc_ref, wga_ref, bga_ref, wg_ref, bg_ref, wa_ref, wb_ref,
                    wc_ref, wo_ref, g_ref, beta_ref, o_ref, *, alpha, sub):
    for s0 in range(0, x_ref.shape[0], sub):
        rs = slice(s0, s0 + sub)
        x = x_ref[rs, :]
        xb = x.astype(BF16)

        def proj(c0, c1):
            return jnp.dot(xb, wg_ref[:, c0:c1], preferred_element_type=F32) + bg_ref[:, c0:c1]

        ga = jnp.dot(xb, wga_ref[...], preferred_element_type=F32) + bga_ref[...]
        ya = (ya_ref[rs, :] * _silu(ga)).astype(BF16)
        yc = (yc_ref[rs, :] * _silu(proj(G_GC, G_ZG))).astype(BF16)
        yb = (yb_ref[rs, :] * _silu(proj(G_GB, G_ZC))).astype(BF16)
        zg = G_ZG
        merged = _sigmoid(proj(zg, zg + D_MODEL)) * jnp.dot(
            ya, wa_ref[...], preferred_element_type=F32)
        merged += _sigmoid(proj(zg + D_MODEL, zg + 2 * D_MODEL)) * jnp.dot(
            yb, wb_ref[...], preferred_element_type=F32)
        merged += _sigmoid(proj(zg + 2 * D_MODEL, zg + 3 * D_MODEL)) * jnp.dot(
            yc, wc_ref[...], preferred_element_type=F32)
        out = jnp.dot(merged.astype(BF16), wo_ref[...], preferred_element_type=F32)
        y = alpha * x + out
        mu = jnp.mean(y, axis=-1, keepdims=True)
        yc0 = y - mu
        var = jnp.mean(yc0 * yc0, axis=-1, keepdims=True)
        o_ref[rs, :] = yc0 * lax.rsqrt(var + LN_EPS) * g_ref[...] + beta_ref[...]


def _merge_out(x2, ya2, yb2, yc2, p, layer, tm, alpha):
    n = x2.shape[0]
    consts = [p[name] for name in ("wga", "bga", "wg", "bg", "wa", "wb", "wc", "wo", "lng", "lnb")]
    row_spec = lambda w: pl.BlockSpec((tm, w), lambda i: (i, 0))
    return pl.pallas_call(
        functools.partial(_merge_out_body, alpha=alpha, sub=min(tm, MERGE_SUB_ROWS)),
        name="merge_out",
        grid=(n // tm,),
        in_specs=[row_spec(D_MODEL), row_spec(D_A), row_spec(D_B), row_spec(D_C)]
        + [_layer_spec(c, layer) for c in consts],
        out_specs=row_spec(D_MODEL),
        out_shape=jax.ShapeDtypeStruct((n, D_MODEL), F32),
        compiler_params=pltpu.CompilerParams(
            dimension_semantics=("parallel",), vmem_limit_bytes=VMEM_LIMIT),
    )(x2, ya2, yb2, yc2, *consts)


def _aug_selector():
    sel = np.zeros((LANES, LANES), np.float32)
    for h in range(N_HEADS):
        for part in range(3):
            sel[part * 8 + h, 8 * h + part] = 1.0
            sel[part * 8 + h, 8 * h + 3 + part] = -1.0
    return jnp.asarray(sel, BF16)


def _block_diag(blocks):
    nl, g, r, c = blocks.shape
    eye = jnp.eye(g, dtype=blocks.dtype)
    return jnp.einsum("lgrc,gk->lgrkc", blocks, eye).reshape(nl, g * r, g * c)


def _stacked_params(nb, w_in, b_in, pool_w, pool_scale, abr, abi, bbr, bbi, c_re, c_im, ssm_d,
                    w_glu, b_glu, w_up_a, w_up_b, w_up_c, w_out, ln_g, ln_b):
    nl = w_in.shape[0]
    offs = np.cumsum([0, D_A, D_A, D_B, D_B, D_B, N_HEADS, D_B, D_C, D_C, 3 * D_MODEL])
    za, ga, zq, zk, zv, zf, gb, zc, gc, zg = [slice(int(offs[i]), int(offs[i + 1])) for i in range(10)]
    scale = LOG2E / math.sqrt(HEAD_DIM)
    fpad = LANES - N_HEADS
    b = b_in
    w = lax.optimization_barrier(w_in.astype(BF16))
    wq = (w_in[..., zq] * scale).astype(BF16)
    w1 = jnp.concatenate([w[..., za], w[..., zc], jnp.pad(w[..., zf], ((0, 0), (0, 0), (0, fpad))),
                          wq, w[..., zk], w[..., zv]], axis=-1)
    b1 = jnp.concatenate([b[:, za], b[:, zc], jnp.pad(b[:, zf], ((0, 0), (0, fpad))),
                          b[:, zq] * scale, b[:, zk], b[:, zv]], axis=-1)
    tail = slice(gb.start, zg.stop)
    row = lambda a: a.reshape(nl, 1, -1).astype(F32)
    tile_nb = lambda a: jnp.broadcast_to(a.reshape(nl, 1, N_STATE), (nl, nb, N_STATE))
    return {
        "w1": w1, "b1": row(b1),
        "pw": _block_diag(pool_w).astype(BF16), "ps": row(pool_scale),
        "are": tile_nb(abr), "aim": tile_nb(abi),
        "bbr": _block_diag(bbr).astype(BF16),
        "bbi": _block_diag(bbi).astype(BF16),
        "cre": _block_diag(jnp.swapaxes(c_re, 2, 3)).astype(BF16),
        "cim": _block_diag(jnp.swapaxes(c_im, 2, 3)).astype(BF16),
        "d": row(ssm_d), "wglu": w_glu.astype(BF16), "bglu": row(b_glu),
        "wga": w[..., ga], "bga": row(b[:, ga]), "wg": w[..., tail], "bg": row(b[:, tail]),
        "wa": w_up_a.astype(BF16), "wb": w_up_b.astype(BF16), "wc": w_up_c.astype(BF16),
        "wo": w_out.astype(BF16), "lng": row(ln_g), "lnb": row(ln_b),
    }


def _trunk(x, w_in, b_in, pool_w, pool_scale, ssm_a_re, ssm_a_im, ssm_log_dt, ssm_b_re, ssm_b_im,
           ssm_c_re, ssm_c_im, ssm_d, w_glu, b_glu, w_up_a, w_up_b, w_up_c, w_out, ln_g, ln_b,
           *, tt, tq, tm, hp=8, alpha=None):
    nb, seq, _ = x.shape
    depth = w_in.shape[0]
    if alpha is None:
        alpha = (2.0 * depth) ** 0.25
    abr, abi, bbr, bbi = _ssm_prep(ssm_a_re, ssm_a_im, ssm_log_dt, ssm_b_re, ssm_b_im)
    p = _stacked_params(nb, w_in, b_in, pool_w, pool_scale, abr, abi, bbr, bbi, ssm_c_re, ssm_c_im,
                        ssm_d, w_glu, b_glu, w_up_a, w_up_b, w_up_c, w_out, ln_g, ln_b)
    sel = _aug_selector()
    h = x
    for l in range(depth):
        ya, yc, q, k, v = _mix_in(h, p, sel, l, tt)
        yb = _attention(q, k, v, tq, hp)
        n = nb * seq
        h = _merge_out(h.reshape(n, D_MODEL), ya.reshape(n, D_A), yb.reshape(n, D_B),
                       yc.reshape(n, D_C), p, l, tm, alpha).reshape(nb, seq, D_MODEL)
    return h


def kernel(x, w_in, b_in, pool_w, pool_scale, ssm_a_re, ssm_a_im, ssm_log_dt, ssm_b_re, ssm_b_im,
           ssm_c_re, ssm_c_im, ssm_d, w_glu, b_glu, w_up_a, w_up_b, w_up_c, w_out, ln_g, ln_b):
    return _trunk(x, w_in, b_in, pool_w, pool_scale, ssm_a_re, ssm_a_im, ssm_log_dt, ssm_b_re,
                  ssm_b_im, ssm_c_re, ssm_c_im, ssm_d, w_glu, b_glu, w_up_a, w_up_b, w_up_c, w_out,
                  ln_g, ln_b, tt=128, tq=512, tm=1024)
```

```python
import functools
import math

import jax
import jax.numpy as jnp
import numpy as np
from jax import lax
from jax.experimental import pallas as pl
from jax.experimental.pallas import tpu as pltpu

F32 = jnp.float32
BF16 = jnp.bfloat16

D_MODEL = 1024
N_HEADS = 8
HEAD_DIM = 64
D_A = 256
D_B = N_HEADS * HEAD_DIM
D_C = 256
POOL_WINDOWS = (2, 4, 8, 16)
POOL_GROUP = 64
SSM_GROUPS = 16
SSM_GROUP = 16
SSM_STATE = 64
N_STATE = SSM_GROUPS * SSM_STATE
LN_EPS = 1e-5
LANES = 128
SUBLANES = 8
NEG_BIG = -1e30
LOG2E = math.log2(math.e)
NEXT_BLOCK_LEAD = 1
MERGE_SUB_ROWS = 1024
V_ROWS = 80

ZA_OFF, ZC_OFF, ZF_OFF, Q_OFF, K_OFF, V_OFF = 0, 256, 512, 640, 1152, 1664
C1 = 2176
C3 = 4096
AUG = HEAD_DIM
POOL_HISTORY = 16

VMEM_LIMIT = 56 * 1024 * 1024


def _sigmoid(x):
    return 0.5 * (jnp.tanh(0.5 * x) + 1.0)


def _silu(x):
    return x * _sigmoid(x)


def _gelu_tanh(x):
    return 0.5 * x * (1.0 + jnp.tanh(math.sqrt(2.0 / math.pi) * (x + 0.044715 * (x * x * x))))


def _log_sigmoid(x):
    return jnp.minimum(x, 0.0) - jnp.log(1.0 + jnp.exp(-jnp.abs(x)))


def _ssm_prep_body(are_ref, aim_ref, ldt_ref, bre_ref, bim_ref, abr_ref, abi_ref, bbr_ref, bbi_ref):
    are = are_ref[...]
    aim = aim_ref[...]
    dt = jnp.exp(ldt_ref[...])
    mag = jnp.exp(are * dt)
    ang = aim * dt
    abr = mag * jnp.cos(ang)
    abi = mag * jnp.sin(ang)
    den = are * are + aim * aim
    nr = abr - 1.0
    cre = (nr * are + abi * aim) / den
    cim = (abi * are - nr * aim) / den
    bre = bre_ref[...]
    bim = bim_ref[...]
    abr_ref[...] = abr
    abi_ref[...] = abi
    bbr_ref[...] = cre * bre - cim * bim
    bbi_ref[...] = cre * bim + cim * bre


def _ssm_prep(a_re, a_im, log_dt, b_re, b_im):
    L = a_re.shape[0]
    rows = L * SSM_GROUPS
    wide = SSM_GROUP * SSM_STATE
    tile = lambda a: jnp.tile(a.reshape(rows, SSM_STATE), (1, SSM_GROUP))
    ldt = jnp.broadcast_to(log_dt.reshape(rows, 1), (rows, wide))
    bt = lambda b: jnp.swapaxes(b, 2, 3).reshape(rows, wide)
    outs = pl.pallas_call(
        _ssm_prep_body,
        name="ssm_prep",
        out_shape=[jax.ShapeDtypeStruct((rows, wide), F32)] * 4,
    )(tile(a_re), tile(a_im), ldt, bt(b_re), bt(b_im))
    abr, abi, bbr, bbi = outs
    abr = abr[:, :SSM_STATE].reshape(L, SSM_GROUPS, SSM_STATE)
    abi = abi[:, :SSM_STATE].reshape(L, SSM_GROUPS, SSM_STATE)
    bbr = bbr.reshape(L, SSM_GROUPS, SSM_GROUP, SSM_STATE)
    bbi = bbi.reshape(L, SSM_GROUPS, SSM_GROUP, SSM_STATE)
    return abr, abi, bbr, bbi


def _mix_in_body(x_ref, w1_ref, b1_ref, pw_ref, ps_ref, are_ref, aim_ref, bbr_ref, bbi_ref,
                 cre_ref, cim_ref, d_ref, wglu_ref, bglu_ref, sel_ref,
                 ya_ref, yc_ref, q_ref, k_ref, v_ref,
                 xb_scr, z_scr, zc_scr, ext_scr, lv2_scr, lv4_scr, lv8_scr, slin_scr, slout_scr, utm_scr,
                 xr_scr, xi_scr, stre_scr, stim_scr, stf_scr, *, tt, nb):
    rows = nb * tt
    pitch = tt + 4
    hrows = POOL_HISTORY * nb
    step = pl.program_id(0)

    @pl.when(step == 0)
    def _():
        ext_scr[0:hrows, :] = jnp.zeros((hrows, D_A), F32)
        stre_scr[...] = jnp.zeros_like(stre_scr)
        stim_scr[...] = jnp.zeros_like(stim_scr)
        stf_scr[...] = jnp.zeros_like(stf_scr)

    xb_scr[...] = x_ref[...].reshape(rows, D_MODEL).astype(BF16)
    n_chunks = (C1 - Q_OFF) // 256

    def project_chunk(chunk):
        c0 = Q_OFF + chunk * 256
        zc_scr[chunk] = (jnp.dot(xb_scr[...], w1_ref[:, c0:c0 + 256],
                                 preferred_element_type=F32) + b1_ref[:, c0:c0 + 256])

    for c0 in range(0, Q_OFF, 256):
        c1 = min(c0 + 256, Q_OFF)
        z_scr[:, c0:c1] = (jnp.dot(xb_scr[...], w1_ref[:, c0:c1], preferred_element_type=F32)
                           + b1_ref[:, c0:c1])
    project_chunk(0)

    for b in range(nb):
        r0 = b * tt
        dst = slice(b * pitch, b * pitch + tt)
        slin_scr[0, dst, :] = z_scr[r0:r0 + tt, ZC_OFF:ZC_OFF + LANES]
        slin_scr[1, dst, :] = z_scr[r0:r0 + tt, ZC_OFF + LANES:ZC_OFF + 2 * LANES]
        slin_scr[2, dst, :] = _log_sigmoid(z_scr[r0:r0 + tt, ZF_OFF:ZF_OFF + LANES])
        slin_scr[3, dst, :] = z_scr[r0:r0 + tt, ZA_OFF:ZA_OFF + LANES]
        slin_scr[4, dst, :] = z_scr[r0:r0 + tt, ZA_OFF + LANES:ZA_OFF + 2 * LANES]
    for t in range(tt):
        r = slice(t * nb, (t + 1) * nb)
        e = slice(hrows + t * nb, hrows + (t + 1) * nb)
        utm_scr[r, 0:LANES] = slin_scr[0, pl.ds(t, nb, stride=pitch), :]
        utm_scr[r, LANES:2 * LANES] = slin_scr[1, pl.ds(t, nb, stride=pitch), :]
        ext_scr[e, 0:LANES] = slin_scr[3, pl.ds(t, nb, stride=pitch), :]
        ext_scr[e, LANES:2 * LANES] = slin_scr[4, pl.ds(t, nb, stride=pitch), :]
    project_chunk(1)

    ext_len = rows + hrows
    lv2_scr[nb:ext_len, :] = ext_scr[nb:ext_len, :] + ext_scr[0:ext_len - nb, :]
    lv4_scr[3 * nb:ext_len, :] = lv2_scr[3 * nb:ext_len, :] + lv2_scr[nb:ext_len - 2 * nb, :]
    lv8_scr[7 * nb:ext_len, :] = lv4_scr[7 * nb:ext_len, :] + lv4_scr[3 * nb:ext_len - 4 * nb, :]
    s16 = lv8_scr[hrows:ext_len, :] + lv8_scr[hrows - 8 * nb:ext_len - 8 * nb, :]
    lane_a = lax.broadcasted_iota(jnp.int32, (rows, D_A), 1)
    time_a = step * tt + lax.broadcasted_iota(jnp.int32, (tt, nb, D_A), 0).reshape(rows, D_A)
    win = jnp.where(lane_a < 64, 2, jnp.where(lane_a < 128, 4, jnp.where(lane_a < 192, 8, 16)))
    cnt = jnp.minimum(time_a + 1, win).astype(F32)
    wsum = jnp.where(lane_a < 64, lv2_scr[hrows:ext_len, :],
                     jnp.where(lane_a < 128, lv4_scr[hrows:ext_len, :],
                               jnp.where(lane_a < 192, lv8_scr[hrows:ext_len, :], s16)))
    pooled = wsum / cnt - ext_scr[hrows:ext_len, :]
    ya = jnp.dot(pooled.astype(BF16), pw_ref[...], preferred_element_type=F32) * ps_ref[...]
    slout_scr[3] = ya[:, 0:LANES]
    slout_scr[4] = ya[:, LANES:2 * LANES]
    ext_scr[0:hrows, :] = ext_scr[rows:ext_len, :]

    u_tm = utm_scr[...]
    ub = u_tm.astype(BF16)
    xr_scr[...] = jnp.dot(ub, bbr_ref[...], preferred_element_type=F32)
    xi_scr[...] = jnp.dot(ub, bbi_ref[...], preferred_element_type=F32)

    ar = are_ref[...]
    ai = aim_ref[...]
    sr, si, f = stre_scr[...], stim_scr[...], stf_scr[...]
    late_chunks = n_chunks - 2
    for t in range(tt):
        if t % (tt // late_chunks) == 0 and t // (tt // late_chunks) < late_chunks:
            project_chunk(2 + t // (tt // late_chunks))
        r = slice(t * nb, (t + 1) * nb)
        nr = ar * sr - ai * si + xr_scr[r, :]
        ni = ar * si + ai * sr + xi_scr[r, :]
        xr_scr[r, :] = nr
        xi_scr[r, :] = ni
        f = f + slin_scr[2, pl.ds(t, nb, stride=pitch), :]
        slout_scr[2, r, :] = f
        sr, si = nr, ni
    stre_scr[...] = sr
    stim_scr[...] = si
    stf_scr[...] = f

    lane_r = lax.broadcasted_iota(jnp.int32, (rows, LANES), 1)
    f_tm = slout_scr[2] * LOG2E
    hi = f_tm.astype(BF16).astype(F32)
    r1 = f_tm - hi
    mid = r1.astype(BF16).astype(F32)
    lo = r1 - mid
    fa = jnp.where(lane_r < 8, hi,
                   jnp.where(lane_r < 16, pltpu.roll(mid, 8, axis=1),
                             jnp.where(lane_r < 24, pltpu.roll(lo, 16, axis=1), 0.0))).astype(BF16)
    slout_scr[2] = jnp.dot(fa, sel_ref[...], preferred_element_type=F32)

    lane = lax.broadcasted_iota(jnp.int32, (tt, LANES), 1)
    in_lo = lane < AUG + 3
    in_hi = (lane >= AUG + 3) & (lane < AUG + 6)
    ones_lo = jnp.where((lane >= AUG) & in_lo, 1.0, 0.0)
    ones_hi = jnp.where(in_hi, 1.0, 0.0)
    out_refs = (q_ref, k_ref)

    def emit_chunk(chunk):
        kind, half = divmod(chunk, 2)
        for b in range(nb):
            if kind == 2:
                for pr in range(2):
                    v_ref[b, 2 * half + pr] = zc_scr[
                        chunk, b * tt:(b + 1) * tt, pr * LANES:(pr + 1) * LANES].astype(BF16)
                continue
            fg = slout_scr[2, pl.ds(b, tt, stride=nb), :]
            for hl in range(4):
                h = 4 * half + hl
                c = (hl // 2) * LANES
                z = zc_scr[chunk, b * tt:(b + 1) * tt, c:c + LANES]
                if hl % 2:
                    z = pltpu.roll(z, HEAD_DIM, axis=1)
                if kind == 0:
                    aug = jnp.where(in_lo, pltpu.roll(fg, AUG - 8 * h, axis=1), ones_hi)
                else:
                    aug = jnp.where(in_hi, pltpu.roll(fg, AUG - 8 * h, axis=1), ones_lo)
                out_refs[kind][b, h] = jnp.where(lane < AUG, z, aug).astype(BF16)

    emit_chunk(0)
    y = jnp.dot(xr_scr[...].astype(BF16), cre_ref[...], preferred_element_type=F32)
    emit_chunk(1)
    y = y - jnp.dot(xi_scr[...].astype(BF16), cim_ref[...], preferred_element_type=F32)
    emit_chunk(2)
    y = _gelu_tanh(y + d_ref[...] * u_tm)
    glu = jnp.dot(y.astype(BF16), wglu_ref[...], preferred_element_type=F32) + bglu_ref[...]
    emit_chunk(3)
    y = y * _sigmoid(glu)
    slout_scr[0] = y[:, 0:LANES]
    slout_scr[1] = y[:, LANES:2 * LANES]
    emit_chunk(4)
    emit_chunk(5)

    for b in range(nb):
        ya_ref[b, :, 0:LANES] = slout_scr[3, pl.ds(b, tt, stride=nb), :]
        ya_ref[b, :, LANES:2 * LANES] = slout_scr[4, pl.ds(b, tt, stride=nb), :]
        yc_ref[b, :, 0:LANES] = slout_scr[0, pl.ds(b, tt, stride=nb), :]
        yc_ref[b, :, LANES:2 * LANES] = slout_scr[1, pl.ds(b, tt, stride=nb), :]


def _const_spec(shape):
    nd = len(shape)
    return pl.BlockSpec(shape, lambda *_: (0,) * nd, pipeline_mode=pl.Buffered(1))


def _layer_spec(stacked, layer):
    nd = stacked.ndim - 1
    return pl.BlockSpec((None,) + stacked.shape[1:], lambda *_: (layer,) + (0,) * nd,
                        pipeline_mode=pl.Buffered(1))


def _mix_in(x, p, sel, layer, tt):
    nb, seq, _ = x.shape
    nt = seq // tt
    rows = nb * tt
    pitch = tt + 4
    hrows = POOL_HISTORY * nb
    consts = [p[name] for name in ("w1", "b1", "pw", "ps", "are", "aim", "bbr", "bbi", "cre", "cim",
                                   "d", "wglu", "bglu")]
    in_specs = [pl.BlockSpec((nb, tt, D_MODEL), lambda i: (0, i, 0))]
    in_specs += [_layer_spec(c, layer) for c in consts] + [_const_spec(sel.shape)]
    out_shape = [
        jax.ShapeDtypeStruct((nb, seq, D_A), F32),
        jax.ShapeDtypeStruct((nb, seq, D_C), F32),
        jax.ShapeDtypeStruct((nb, N_HEADS, seq, LANES), BF16),
        jax.ShapeDtypeStruct((nb, N_HEADS, seq, LANES), BF16),
        jax.ShapeDtypeStruct((nb, N_HEADS // 2, seq, LANES), BF16),
    ]
    out_specs = [
        pl.BlockSpec((nb, tt, D_A), lambda i: (0, i, 0)),
        pl.BlockSpec((nb, tt, D_C), lambda i: (0, i, 0)),
        pl.BlockSpec((nb, N_HEADS, tt, LANES), lambda i: (0, 0, i, 0)),
        pl.BlockSpec((nb, N_HEADS, tt, LANES), lambda i: (0, 0, i, 0)),
        pl.BlockSpec((nb, N_HEADS // 2, tt, LANES), lambda i: (0, 0, i, 0)),
    ]
    scratch = [
        pltpu.VMEM((rows, D_MODEL), BF16),
        pltpu.VMEM((rows, Q_OFF), F32),
        pltpu.VMEM(((C1 - Q_OFF) // 256, rows, 256), F32),
        pltpu.VMEM((rows + hrows, D_A), F32),
        pltpu.VMEM((rows + hrows, D_A), F32),
        pltpu.VMEM((rows + hrows, D_A), F32),
        pltpu.VMEM((rows + hrows, D_A), F32),
        pltpu.VMEM((5, nb * pitch, LANES), F32),
        pltpu.VMEM((5, rows, LANES), F32),
        pltpu.VMEM((rows, D_C), F32),
        pltpu.VMEM((rows, N_STATE), F32),
        pltpu.VMEM((rows, N_STATE), F32),
        pltpu.VMEM((nb, N_STATE), F32),
        pltpu.VMEM((nb, N_STATE), F32),
        pltpu.VMEM((nb, LANES), F32),
    ]
    return pl.pallas_call(
        functools.partial(_mix_in_body, tt=tt, nb=nb),
        name="mix_in",
        grid=(nt,),
        in_specs=in_specs,
        out_specs=out_specs,
        out_shape=out_shape,
        scratch_shapes=scratch,
        compiler_params=pltpu.CompilerParams(
            dimension_semantics=("arbitrary",), vmem_limit_bytes=VMEM_LIMIT),
    )(x, *consts, sel)


def _attn_body(q_ref, k_ref, v_ref, o_ref, vt_scr, m_scr, acc_scr, st_scr, *, t, hp):
    i = pl.program_id(2)
    dims = (((1,), (1,)), ((), ()))

    for hh in range(hp):
        m_scr[hh] = jnp.full((1, t), NEG_BIG, F32)
        acc_scr[hh] = jnp.zeros((LANES, t), F32)

    key = lax.broadcasted_iota(jnp.int32, (t, t), 0)
    qry = lax.broadcasted_iota(jnp.int32, (t, t), 1)

    def scores(hh, j):
        rows = pl.ds(pl.multiple_of(j * t, t), t)
        return lax.dot_general(k_ref[0, hh, rows, :], q_ref[0, hh], dims,
                               preferred_element_type=F32)

    st_scr[0] = scores(0, 0)

    pad_rows = lax.broadcasted_iota(jnp.int32, (V_ROWS - HEAD_DIM, t), 0)
    ones_row = jnp.where(pad_rows == 0, 1.0, 0.0).astype(BF16)
    for pr in range(hp // 2):
        vt = v_ref[0, pr].astype(F32).T.astype(BF16)
        for hh in (2 * pr, 2 * pr + 1):
            vt_scr[hh, i, 0:HEAD_DIM, :] = vt[(hh % 2) * HEAD_DIM:(hh % 2 + 1) * HEAD_DIM]
            vt_scr[hh, i, HEAD_DIM:V_ROWS, :] = ones_row

    def read_scores(hh, masked):
        st = st_scr[hh]
        return jnp.where(key <= qry, st, NEG_BIG) if masked else st

    def block(j, masked):
        for hh in range(hp):
            if hh == 0:
                st_scr[1] = scores(1, j)
            if hh + 2 < hp:
                st_scr[hh + 2] = scores(hh + 2, j)
            if not masked and hh == hp - NEXT_BLOCK_LEAD:
                st_scr[0] = scores(0, j + 1)
            m_old = m_scr[hh]
            m_new = jnp.maximum(m_old, jnp.max(read_scores(hh, masked), axis=0, keepdims=True))
            p = jnp.exp2(read_scores(hh, masked) - m_new).astype(BF16)
            alpha = jnp.exp2(m_old - m_new)
            acc_scr[hh, 0:V_ROWS, :] = alpha * acc_scr[hh, 0:V_ROWS, :] + jnp.dot(
                vt_scr[hh, j], p, preferred_element_type=F32)
            m_scr[hh] = m_new

    def body(jj, carry):
        block(2 * jj, False)
        block(2 * jj + 1, False)
        return carry

    lax.fori_loop(0, i // 2, body, 0)

    @pl.when(i % 2 == 1)
    def _():
        block(i - 1, False)

    block(i, True)
    for pr in range(hp // 2):
        halves = []
        for hh in (2 * pr, 2 * pr + 1):
            inv = 1.0 / acc_scr[hh, HEAD_DIM:HEAD_DIM + 1, :]
            halves.append(acc_scr[hh, 0:HEAD_DIM, :] * inv)
        o_ref[0, :, pr * LANES:(pr + 1) * LANES] = jnp.concatenate(halves, axis=0).T


def _attention(q, k, v, t, hp):
    nb, nh, seq, _ = q.shape
    nq = seq // t
    return pl.pallas_call(
        functools.partial(_attn_body, t=t, hp=hp),
        name="fox_attention",
        grid=(nb, nh // hp, nq),
        in_specs=[
            pl.BlockSpec((1, hp, t, LANES), lambda b, h, i: (b, h, i, 0)),
            pl.BlockSpec((1, hp, seq, LANES), lambda b, h, i: (b, h, 0, 0)),
            pl.BlockSpec((1, hp // 2, t, LANES), lambda b, h, i: (b, h, i, 0)),
        ],
        out_specs=pl.BlockSpec((1, t, hp * HEAD_DIM), lambda b, h, i: (b, i, h)),
        out_shape=jax.ShapeDtypeStruct((nb, seq, D_B), F32),
        scratch_shapes=[
            pltpu.VMEM((hp, seq // t, V_ROWS, t), BF16),
            pltpu.VMEM((hp, 1, t), F32),
            pltpu.VMEM((hp, LANES, t), F32),
            pltpu.VMEM((hp, t, t), F32),
        ],
        compiler_params=pltpu.CompilerParams(
            dimension_semantics=("parallel", "parallel", "arbitrary"),
            vmem_limit_bytes=VMEM_LIMIT),
    )(q, k, v)


def _merge_out_body(x_ref, ya_ref, yb_ref, yc_ref, w3_ref, b3_ref, wa_ref, wb_ref, wc_ref, wo_ref,
                    g_ref, beta_ref, o_ref, *, alpha, sub):
    for s0 in range(0, x_ref.shape[0], sub):
        rs = slice(s0, s0 + sub)
        x = x_ref[rs, :]
        xb = x.astype(BF16)

        def proj(c0, c1):
            return jnp.dot(xb, w3_ref[:, c0:c1], preferred_element_type=F32) + b3_ref[:, c0:c1]

        ya = (ya_ref[rs, :] * _silu(proj(0, 256))).astype(BF16)
        yc = (yc_ref[rs, :] * _silu(proj(256, 512))).astype(BF16)
        yb = (yb_ref[rs, :] * _silu(proj(512, 1024))).astype(BF16)
        merged = _sigmoid(proj(1024, 2048)) * jnp.dot(ya, wa_ref[...], preferred_element_type=F32)
        merged += _sigmoid(proj(2048, 3072)) * jnp.dot(yb, wb_ref[...], preferred_element_type=F32)
        merged += _sigmoid(proj(3072, 4096)) * jnp.dot(yc, wc_ref[...], preferred_element_type=F32)
        out = jnp.dot(merged.astype(BF16), wo_ref[...], preferred_element_type=F32)
        y = alpha * x + out
        mu = jnp.mean(y, axis=-1, keepdims=True)
        yc0 = y - mu
        var = jnp.mean(yc0 * yc0, axis=-1, keepdims=True)
        o_ref[rs, :] = yc0 * lax.rsqrt(var + LN_EPS) * g_ref[...] + beta_ref[...]


def _merge_out(x2, ya2, yb2, yc2, p, layer, tm, alpha):
    n = x2.shape[0]
    consts = [p[name] for name in ("w3", "b3", "wa", "wb", "wc", "wo", "lng", "lnb")]
    row_spec = lambda w: pl.BlockSpec((tm, w), lambda i: (i, 0))
    return pl.pallas_call(
        functools.partial(_merge_out_body, alpha=alpha, sub=min(tm, MERGE_SUB_ROWS)),
        name="merge_out",
        grid=(n // tm,),
        in_specs=[row_spec(D_MODEL), row_spec(D_A), row_spec(D_B), row_spec(D_C)]
        + [_layer_spec(c, layer) for c in consts],
        out_specs=row_spec(D_MODEL),
        out_shape=jax.ShapeDtypeStruct((n, D_MODEL), F32),
        compiler_params=pltpu.CompilerParams(
            dimension_semantics=("parallel",), vmem_limit_bytes=VMEM_LIMIT),
    )(x2, ya2, yb2, yc2, *consts)


def _aug_selector():
    sel = np.zeros((LANES, LANES), np.float32)
    for h in range(N_HEADS):
        for part in range(3):
            sel[part * 8 + h, 8 * h + part] = 1.0
            sel[part * 8 + h, 8 * h + 3 + part] = -1.0
    return jnp.asarray(sel, BF16)


def _block_diag(blocks):
    nl, g, r, c = blocks.shape
    eye = jnp.eye(g, dtype=blocks.dtype)
    return jnp.einsum("lgrc,gk->lgrkc", blocks, eye).reshape(nl, g * r, g * c)


def _stacked_params(nb, w_in, b_in, pool_w, pool_scale, abr, abi, bbr, bbi, c_re, c_im, ssm_d,
                    w_glu, b_glu, w_up_a, w_up_b, w_up_c, w_out, ln_g, ln_b):
    nl = w_in.shape[0]
    offs = np.cumsum([0, D_A, D_A, D_B, D_B, D_B, N_HEADS, D_B, D_C, D_C, 3 * D_MODEL])
    za, ga, zq, zk, zv, zf, gb, zc, gc, zg = [slice(int(offs[i]), int(offs[i + 1])) for i in range(10)]
    scale = LOG2E / math.sqrt(HEAD_DIM)
    fpad = LANES - N_HEADS
    b = b_in
    w = lax.optimization_barrier(w_in.astype(BF16))
    wq = (w_in[..., zq] * scale).astype(BF16)
    w1 = jnp.concatenate([w[..., za], w[..., zc], jnp.pad(w[..., zf], ((0, 0), (0, 0), (0, fpad))),
                          wq, w[..., zk], w[..., zv]], axis=-1)
    b1 = jnp.concatenate([b[:, za], b[:, zc], jnp.pad(b[:, zf], ((0, 0), (0, fpad))),
                          b[:, zq] * scale, b[:, zk], b[:, zv]], axis=-1)
    w3 = jnp.concatenate([w[..., ga], w[..., gc], w[..., gb], w[..., zg]], axis=-1)
    b3 = jnp.concatenate([b[:, ga], b[:, gc], b[:, gb], b[:, zg]], axis=-1)
    row = lambda a: a.reshape(nl, 1, -1).astype(F32)
    tile_nb = lambda a: jnp.broadcast_to(a.reshape(nl, 1, N_STATE), (nl, nb, N_STATE))
    return {
        "w1": w1, "b1": row(b1),
        "pw": _block_diag(pool_w).astype(BF16), "ps": row(pool_scale),
        "are": tile_nb(abr), "aim": tile_nb(abi),
        "bbr": _block_diag(bbr).astype(BF16),
        "bbi": _block_diag(bbi).astype(BF16),
        "cre": _block_diag(jnp.swapaxes(c_re, 2, 3)).astype(BF16),
        "cim": _block_diag(jnp.swapaxes(c_im, 2, 3)).astype(BF16),
        "d": row(ssm_d), "wglu": w_glu.astype(BF16), "bglu": row(b_glu),
        "w3": w3, "b3": row(b3),
        "wa": w_up_a.astype(BF16), "wb": w_up_b.astype(BF16), "wc": w_up_c.astype(BF16),
        "wo": w_out.astype(BF16), "lng": row(ln_g), "lnb": row(ln_b),
    }


def _trunk(x, w_in, b_in, pool_w, pool_scale, ssm_a_re, ssm_a_im, ssm_log_dt, ssm_b_re, ssm_b_im,
           ssm_c_re, ssm_c_im, ssm_d, w_glu, b_glu, w_up_a, w_up_b, w_up_c, w_out, ln_g, ln_b,
           *, tt, tq, tm, hp=8, alpha=None):
    nb, seq, _ = x.shape
    depth = w_in.shape[0]
    if alpha is None:
        alpha = (2.0 * depth) ** 0.25
    abr, abi, bbr, bbi = _ssm_prep(ssm_a_re, ssm_a_im, ssm_log_dt, ssm_b_re, ssm_b_im)
    p = _stacked_params(nb, w_in, b_in, pool_w, pool_scale, abr, abi, bbr, bbi, ssm_c_re, ssm_c_im,
                        ssm_d, w_glu, b_glu, w_up_a, w_up_b, w_up_c, w_out, ln_g, ln_b)
    sel = _aug_selector()
    h = x
    for l in range(depth):
        ya, yc, q, k, v = _mix_in(h, p, sel, l, tt)
        yb = _attention(q, k, v, tq, hp)
        n = nb * seq
        h = _merge_out(h.reshape(n, D_MODEL), ya.reshape(n, D_A), yb.reshape(n, D_B),
                       yc.reshape(n, D_C), p, l, tm, alpha).reshape(nb, seq, D_MODEL)
    return h


def kernel(x, w_in, b_in, pool_w, pool_scale, ssm_a_re, ssm_a_im, ssm_log_dt, ssm_b_re, ssm_b_im,
           ssm_c_re, ssm_c_im, ssm_d, w_glu, b_glu, w_up_a, w_up_b, w_up_c, w_out, ln_g, ln_b):
    return _trunk(x, w_in, b_in, pool_w, pool_scale, ssm_a_re, ssm_a_im, ssm_log_dt, ssm_b_re,
                  ssm_b_im, ssm_c_re, ssm_c_im, ssm_d, w_glu, b_glu, w_up_a, w_up_b, w_up_c, w_out,
                  ln_g, ln_b, tt=128, tq=512, tm=1024)
```
